```python
import jax, jax.numpy as jnp
from jax import lax
import numpy as np

D_MODEL = 2048
BATCH = 16
SEQ = 2048
DEPTH = 2

GRID_W = 64
CTX_LEN = 256
N_MIXERS = 2
N_HEADS = 16
N_KV_HEADS = 4
HEAD_DIM = D_MODEL // N_HEADS
GROUP = N_HEADS // N_KV_HEADS
WINDOW = 128
BLOCK = 128
ROPE_BASE = 10000.0
GLA_HEADS = 4
GLA_KD = D_MODEL // 2
GLA_VD = D_MODEL
GLA_DK = GLA_KD // GLA_HEADS
GLA_DV = GLA_VD // GLA_HEADS
GLA_GATE_RANK = 16
GLA_TAU = 16.0
GLA_CHUNK = 64
MOE_GROUPS = 4
MOE_EXPERTS_PER_GROUP = 8
N_EXPERTS = MOE_GROUPS * MOE_EXPERTS_PER_GROUP
MOE_TOP_K = 2
MOE_HIDDEN = D_MODEL // 4
LN_EPS = 1e-5
RMS_EPS = 1e-6

kernel_name = "hybrid_swa_gla_hiermoe_dit"


def layer_norm(x, g, b):
    xf = x.astype(jnp.float32)
    mu = jnp.mean(xf, axis=-1, keepdims=True)
    var = jnp.mean(jnp.square(xf - mu), axis=-1, keepdims=True)
    return ((xf - mu) * lax.rsqrt(var + LN_EPS) * g + b).astype(x.dtype)


def adaln_mods(cond, w, b):
    m = (jax.nn.silu(cond) @ w + b)[:, None, :]
    return jnp.split(m, 6, axis=-1)


def axial_rope(x, ang_row, ang_col):
    def rot(z, ang):
        z1, z2 = jnp.split(z, 2, axis=-1)
        cos = jnp.cos(ang)[None, :, None, :].astype(z.dtype)
        sin = jnp.sin(ang)[None, :, None, :].astype(z.dtype)
        return jnp.concatenate([z1 * cos - z2 * sin, z2 * cos + z1 * sin], axis=-1)
    xr, xc = jnp.split(x, 2, axis=-1)
    return jnp.concatenate([rot(xr, ang_row), rot(xc, ang_col)], axis=-1)


def windowed_gqa(h, hc, w_qkv, w_o, sink, with_ctx_out):
    B, T, _ = h.shape
    C = hc.shape[1]
    HQ = N_HEADS * HEAD_DIM
    HKV = N_KV_HEADS * HEAD_DIM
    scale = HEAD_DIM ** -0.5
    q, k, v = jnp.split(h @ w_qkv, [HQ, HQ + HKV], axis=-1)
    q = q.reshape(B, T, N_HEADS, HEAD_DIM)
    k = k.reshape(B, T, N_KV_HEADS, HEAD_DIM)
    v = v.reshape(B, T, N_KV_HEADS, HEAD_DIM)
    rows = T // GRID_W
    row, col = jnp.meshgrid(jnp.arange(rows), jnp.arange(GRID_W), indexing="ij")
    half = HEAD_DIM // 2
    inv_freq = ROPE_BASE ** (-jnp.arange(0, half, 2, dtype=jnp.float32) / half)
    ang_row = row.reshape(-1).astype(jnp.float32)[:, None] * inv_freq
    ang_col = col.reshape(-1).astype(jnp.float32)[:, None] * inv_freq
    q = axial_rope(q, ang_row, ang_col)
    k = axial_rope(k, ang_row, ang_col)
    kc, vc = jnp.split(hc @ w_qkv[:, HQ:], 2, axis=-1)
    kc = kc.reshape(B, C, N_KV_HEADS, HEAD_DIM)
    vc = vc.reshape(B, C, N_KV_HEADS, HEAD_DIM)
    nblk = T // BLOCK
    pad = ((0, 0), (BLOCK, BLOCK), (0, 0), (0, 0))
    kp, vp = jnp.pad(k, pad), jnp.pad(v, pad)
    kb = jnp.concatenate([kp[:, j * BLOCK:j * BLOCK + T].reshape(B, nblk, BLOCK, N_KV_HEADS, HEAD_DIM) for j in range(3)], axis=2)
    vb = jnp.concatenate([vp[:, j * BLOCK:j * BLOCK + T].reshape(B, nblk, BLOCK, N_KV_HEADS, HEAD_DIM) for j in range(3)], axis=2)
    a_idx = jnp.arange(BLOCK)[:, None]
    s_idx = jnp.arange(3 * BLOCK)[None, :]
    rel = s_idx - BLOCK - a_idx
    key_pos = jnp.arange(nblk)[:, None, None] * BLOCK - BLOCK + s_idx[None]
    mask = (jnp.abs(rel) <= WINDOW)[None] & (key_pos >= 0) & (key_pos < T)
    qb = q.reshape(B, nblk, BLOCK, N_KV_HEADS, GROUP, HEAD_DIM)
    s_win = jnp.einsum("bnqkgd,bnskd->bkgnqs", qb, kb).astype(jnp.float32) * scale
    s_win = jnp.where(mask, s_win, -jnp.inf)
    s_ctx = jnp.einsum("bnqkgd,bckd->bkgnqc", qb, kc).astype(jnp.float32) * scale
    sink_f = sink.astype(jnp.float32).reshape(1, N_KV_HEADS, GROUP, 1, 1, 1)
    sink_col = jnp.broadcast_to(sink_f, s_win.shape[:-1] + (1,))
    p = jax.nn.softmax(jnp.concatenate([s_win, s_ctx, sink_col], axis=-1), axis=-1).astype(v.dtype)
    p_win, p_ctx = p[..., :3 * BLOCK], p[..., 3 * BLOCK:3 * BLOCK + C]
    o = jnp.einsum("bkgnqs,bnskd->bnqkgd", p_win, vb) + jnp.einsum("bkgnqc,bckd->bnqkgd", p_ctx, vc)
    y = o.reshape(B, T, HQ) @ w_o
    if not with_ctx_out:
        return y, None
    qc = (hc @ w_qkv[:, :HQ]).reshape(B, C, N_KV_HEADS, GROUP, HEAD_DIM)
    sc = jnp.einsum("bqkgd,bckd->bkgqc", qc, kc).astype(jnp.float32) * scale
    sink_c = jnp.broadcast_to(sink_f[..., 0], sc.shape[:-1] + (1,))
    pc = jax.nn.softmax(jnp.concatenate([sc, sink_c], axis=-1), axis=-1)[..., :C].astype(vc.dtype)
    oc = jnp.einsum("bkgqc,bckd->bqkgd", pc, vc)
    yc = oc.reshape(B, C, HQ) @ w_o
    return y, yc


def gla_chunk_scan(q, k, v, lg, s0, with_out):
    B, H, T, _ = k.shape
    n = T // GLA_CHUNK
    to_chunks = lambda t: jnp.moveaxis(t.reshape(B, H, n, GLA_CHUNK, t.shape[-1]), 2, 0)
    tril = jnp.tril(jnp.ones((GLA_CHUNK, GLA_CHUNK), dtype=bool))

    def step(S, inp):
        kc, vc, gc = inp[0], inp[1], inp[2]
        b = jnp.cumsum(gc, axis=-2)
        b_last = b[..., -1:, :]
        S_new = jnp.exp(b_last[..., 0, :])[..., None] * S + jnp.einsum("bhsd,bhsv->bhdv", kc * jnp.exp(b_last - b), vc)
        if not with_out:
            return S_new, None
        qc = inp[3]
        diff = jnp.where(tril[:, :, None], b[..., :, None, :] - b[..., None, :, :], -jnp.inf)
        att = jnp.einsum("bhtd,bhsd,bhtsd->bhts", qc, kc, jnp.exp(diff))
        o = jnp.einsum("bhts,bhsv->bhtv", att, vc) + jnp.einsum("bhtd,bhdv->bhtv", qc * jnp.exp(b), S)
        return S_new, o

    xs = (to_chunks(k), to_chunks(v), to_chunks(lg)) + ((to_chunks(q),) if with_out else ())
    S_fin, o = lax.scan(step, s0, xs)
    if with_out:
        o = jnp.moveaxis(o, 0, 2).reshape(B, H, T, -1)
    return o, S_fin


def gla_bidir(q, k, v, lg_f, lg_b, s0_f, s0_b, with_out):
    flip = lambda t: None if t is None else jnp.flip(t, axis=2)
    o_f, s_f = gla_chunk_scan(q, k, v, lg_f, s0_f, with_out)
    o_b, s_b = gla_chunk_scan(flip(q), flip(k), flip(v), flip(lg_b), s0_b, with_out)
    o = o_f + jnp.flip(o_b, axis=2) if with_out else None
    return o, s_f, s_b


def bidir_gla(h, hc, w_in, gate_w1, gate_w2, gate_b, norm_g, w_o, with_ctx_out):
    def project(z, with_out):
        B, T, _ = z.shape
        heads = lambda t, d: t.reshape(B, T, GLA_HEADS, d).transpose(0, 2, 1, 3).astype(jnp.float32)
        k = heads(z @ w_in[:, GLA_KD:2 * GLA_KD], GLA_DK)
        v = heads(z @ w_in[:, 2 * GLA_KD:2 * GLA_KD + GLA_VD], GLA_DV)
        lg_f, lg_b = (heads(jax.nn.log_sigmoid((z @ gate_w1[d] @ gate_w2[d] + gate_b[d]).astype(jnp.float32)) / GLA_TAU, GLA_DK) for d in range(2))
        if not with_out:
            return None, k, v, lg_f, lg_b, None
        q = heads(z @ w_in[:, :GLA_KD], GLA_DK) * (GLA_DK ** -0.5)
        r = z @ w_in[:, 2 * GLA_KD + GLA_VD:]
        return q, k, v, lg_f, lg_b, r

    def readout(o, r):
        B, _, T, _ = o.shape
        o = o.transpose(0, 2, 1, 3)
        o = o * lax.rsqrt(jnp.mean(jnp.square(o), axis=-1, keepdims=True) + RMS_EPS) * norm_g
        return (o.reshape(B, T, GLA_VD).astype(r.dtype) * jax.nn.silu(r)) @ w_o

    B = h.shape[0]
    zeros = jnp.zeros((B, GLA_HEADS, GLA_DK, GLA_DV), jnp.float32)
    qc, kc, vc, gfc, gbc, rc = project(hc, with_ctx_out)
    oc, s_f, s_b = gla_bidir(qc, kc, vc, gfc, gbc, zeros, zeros, with_ctx_out)
    q, k, v, gf, gb, r = project(h, True)
    o, _, _ = gla_bidir(q, k, v, gf, gb, s_f, s_b, True)
    y = readout(o, r)
    yc = readout(oc, rc) if with_ctx_out else None
    return y, yc


def hier_moe(h, wg, bg, we, be, w_gate, w_up, w_down):
    g_logits = (h @ wg + bg).astype(jnp.float32)
    g_prob = jax.nn.softmax(g_logits, axis=-1)
    g_idx = jnp.argmax(g_logits, axis=-1)
    g_w = jnp.take_along_axis(g_prob, g_idx[:, None], axis=-1)
    e_logits = (h @ we + be).astype(jnp.float32).reshape(-1, MOE_GROUPS, MOE_EXPERTS_PER_GROUP)
    e_in = jnp.take_along_axis(e_logits, g_idx[:, None, None], axis=1)[:, 0]
    top_v, top_i = lax.top_k(e_in, MOE_TOP_K)
    e_w = jax.nn.softmax(top_v, axis=-1) * g_w
    expert_id = g_idx[:, None] * MOE_EXPERTS_PER_GROUP + top_i
    combine = jnp.sum(jax.nn.one_hot(expert_id, N_EXPERTS, dtype=jnp.float32) * e_w[..., None], axis=1).astype(h.dtype)
    out = jnp.zeros_like(h)
    for e in range(N_EXPERTS):
        hid = jax.nn.silu(h @ w_gate[e]) * (h @ w_up[e])
        out = out + combine[:, e:e + 1] * (hid @ w_down[e])
    return out


def setup_inputs(seed: int = 0) -> dict:
    key = jax.random.key(seed)
    ks = iter(jax.random.split(key, 40))
    f32 = jnp.float32
    nrm = lambda shape, s: jax.random.normal(next(ks), shape, f32) * s
    beta = (8 * DEPTH) ** -0.25
    NA = (DEPTH + 1) // 2
    NB = DEPTH // 2
    D = D_MODEL
    return {
        "x": nrm((BATCH, SEQ, D), 1.0),
        "c": nrm((BATCH, D), 1.0),
        "ctx": nrm((BATCH, CTX_LEN, D), 1.0),
        "c_ctx": nrm((D,), 1.0),
        "mod_w": nrm((DEPTH, D, 6 * D), D ** -0.5),
        "mod_b": nrm((DEPTH, 6 * D), 0.02),
        "ln_g": 1.0 + nrm((DEPTH, 2, D), 0.02),
        "ln_b": nrm((DEPTH, 2, D), 0.02),
        "attn_w_qkv": nrm((NA, D, (N_HEADS + 2 * N_KV_HEADS) * HEAD_DIM), D ** -0.5),
        "attn_w_o": nrm((NA, N_HEADS * HEAD_DIM, D), (N_HEADS * HEAD_DIM) ** -0.5 * beta),
        "attn_sink": nrm((NA, N_HEADS), 0.5),
        "gla_w_in": nrm((NB, D, 2 * GLA_KD + 2 * GLA_VD), D ** -0.5),
        "gla_gate_w1": nrm((NB, 2, D, GLA_GATE_RANK), D ** -0.5),
        "gla_gate_w2": nrm((NB, 2, GLA_GATE_RANK, GLA_KD), GLA_GATE_RANK ** -0.5),
        "gla_gate_b": nrm((NB, 2, GLA_KD), 0.1),
        "gla_norm_g": 1.0 + nrm((NB, GLA_DV), 0.02),
        "gla_w_o": nrm((NB, GLA_VD, D), GLA_VD ** -0.5 * beta),
        "moe_group_w": nrm((DEPTH, D, MOE_GROUPS), D ** -0.5),
        "moe_group_b": nrm((DEPTH, MOE_GROUPS), 0.01),
        "moe_router_w": nrm((DEPTH, D, N_EXPERTS), D ** -0.5),
        "moe_router_b": nrm((DEPTH, N_EXPERTS), 0.01),
        "moe_w_gate": nrm((DEPTH, N_EXPERTS, D, MOE_HIDDEN), D ** -0.5),
        "moe_w_up": nrm((DEPTH, N_EXPERTS, D, MOE_HIDDEN), D ** -0.5),
        "moe_w_down": nrm((DEPTH, N_EXPERTS, MOE_HIDDEN, D), MOE_HIDDEN ** -0.5 * beta),
    }


def reference(x, c, ctx, c_ctx, mod_w, mod_b, ln_g, ln_b, attn_w_qkv, attn_w_o, attn_sink,
              gla_w_in, gla_gate_w1, gla_gate_w2, gla_gate_b, gla_norm_g, gla_w_o,
              moe_group_w, moe_group_b, moe_router_w, moe_router_b, moe_w_gate, moe_w_up, moe_w_down):
    alpha = (2 * DEPTH) ** 0.25
    B, T, D = x.shape
    xc = ctx
    for l in range(DEPTH):
        last = l == DEPTH - 1
        j = l // N_MIXERS
        sh1, sc1, g1, sh2, sc2, g2 = adaln_mods(c, mod_w[l], mod_b[l])
        csh1, csc1, cg1, csh2, csc2, cg2 = adaln_mods(c_ctx[None], mod_w[l], mod_b[l])
        h = x * (1 + sc1) + sh1
        hc = xc * (1 + csc1) + csh1
        if l % N_MIXERS == 0:
            y, yc = windowed_gqa(h, hc, attn_w_qkv[j], attn_w_o[j], attn_sink[j], not last)
        else:
            y, yc = bidir_gla(h, hc, gla_w_in[j], gla_gate_w1[j], gla_gate_w2[j], gla_gate_b[j],
                              gla_norm_g[j], gla_w_o[j], not last)
        x = layer_norm(alpha * x + g1 * y, ln_g[l, 0], ln_b[l, 0])
        if last:
            f = hier_moe((x * (1 + sc2) + sh2).reshape(-1, D), moe_group_w[l], moe_group_b[l],
                         moe_router_w[l], moe_router_b[l], moe_w_gate[l], moe_w_up[l], moe_w_down[l])
            x = layer_norm(alpha * x + g2 * f.reshape(B, T, D), ln_g[l, 1], ln_b[l, 1])
        else:
            xc = layer_norm(alpha * xc + cg1 * yc, ln_g[l, 0], ln_b[l, 0])
            tokens = jnp.concatenate([(x * (1 + sc2) + sh2).reshape(-1, D),
                                      (xc * (1 + csc2) + csh2).reshape(-1, D)], axis=0)
            f = hier_moe(tokens, moe_group_w[l], moe_group_b[l], moe_router_w[l], moe_router_b[l],
                         moe_w_gate[l], moe_w_up[l], moe_w_down[l])
            x = layer_norm(alpha * x + g2 * f[:B * T].reshape(B, T, D), ln_g[l, 1], ln_b[l, 1])
            xc = layer_norm(alpha * xc + cg2 * f[B * T:].reshape(xc.shape), ln_g[l, 1], ln_b[l, 1])
    return x
```

```python
import functools

import numpy as np
import jax
import jax.numpy as jnp
from jax import lax
from jax.experimental import pallas as pl
from jax.experimental.pallas import tpu as pltpu

F32 = jnp.float32
BF16 = jnp.bfloat16

N_HEADS = 16
N_KV_HEADS = 4
HEAD_DIM = 128
GROUP = N_HEADS // N_KV_HEADS
WINDOW = 128
GRID_W = 64
ROPE_BASE = 10000.0
GLA_HEADS = 4
GLA_TAU = 16.0
GLA_CHUNK = 64
MOE_GROUPS = 4
MOE_EXPERTS_PER_GROUP = 8
N_EXPERTS = MOE_GROUPS * MOE_EXPERTS_PER_GROUP
LN_EPS = 1e-5
RMS_EPS = 1e-6

V7X_VMEM_LIMIT_BYTES = 56 * 1024 * 1024
LANES = 128
TOKEN_TILE = 256
MOE_TILE = 256
MODS_COL_TILE = 1024


def _cparams(sem):
    return pltpu.CompilerParams(dimension_semantics=sem, vmem_limit_bytes=V7X_VMEM_LIMIT_BYTES)


def _silu(v):
    return v / (1.0 + jnp.exp(-v))


def _mods_body(cond_ref, w_ref, b_ref, o_ref):
    a = _silu(cond_ref[...]).astype(BF16)
    o_ref[0] = jnp.dot(a, w_ref[0].astype(BF16), preferred_element_type=F32) + b_ref[0]


def _adaln_mods(cond, mod_w, mod_b):
    L, D, N6 = mod_w.shape
    M = cond.shape[0]
    tn = MODS_COL_TILE
    return pl.pallas_call(
        _mods_body,
        grid=(L, N6 // tn),
        in_specs=[
            pl.BlockSpec((M, D), lambda l, j: (0, 0)),
            pl.BlockSpec((1, D, tn), lambda l, j: (l, 0, j)),
            pl.BlockSpec((1, 1, tn), lambda l, j: (l, 0, j)),
        ],
        out_specs=pl.BlockSpec((1, M, tn), lambda l, j: (l, 0, j)),
        out_shape=jax.ShapeDtypeStruct((L, M, N6), F32),
        compiler_params=_cparams(("arbitrary", "arbitrary")),
        name="adaln_mods",
    )(cond, mod_w, mod_b.reshape(L, 1, N6))


def _qkv_body(x_ref, mod_ref, w_ref, cos_ref, sin_ref, o_ref, *, n_rot_heads, n_q_heads, scale):
    m = mod_ref[0, 0]
    h = (x_ref[0] * (1.0 + m[1:2]) + m[0:1]).astype(BF16)
    acc = jnp.dot(h, w_ref[...], preferred_element_type=F32)
    cos = cos_ref[...]
    sin = sin_ref[...]
    lane = lax.broadcasted_iota(jnp.int32, cos.shape, 1)
    first = (lane & 32) == 0
    for hd in range(n_rot_heads):
        y = acc[:, hd * HEAD_DIM:(hd + 1) * HEAD_DIM]
        partner = jnp.where(first, pltpu.roll(y, HEAD_DIM - 32, 1), pltpu.roll(y, 32, 1))
        r = y * cos + partner * sin
        if hd < n_q_heads:
            r = r * scale
        o_ref[0, :, hd * HEAD_DIM:(hd + 1) * HEAD_DIM] = r.astype(BF16)
    rest = n_rot_heads * HEAD_DIM
    o_ref[0, :, rest:] = acc[:, rest:].astype(BF16)


def _rope_tables(C, T):
    half = HEAD_DIM // 2
    pos = np.arange(T)
    inv_freq = ROPE_BASE ** (-np.arange(0, half, 2, dtype=np.float32) / half)
    ang_r = (pos // GRID_W).astype(np.float32)[:, None] * inv_freq
    ang_c = (pos % GRID_W).astype(np.float32)[:, None] * inv_freq
    ang_r = jnp.asarray(ang_r, F32)
    ang_c = jnp.asarray(ang_c, F32)
    cos = jnp.concatenate([jnp.cos(ang_r)] * 2 + [jnp.cos(ang_c)] * 2, axis=-1)
    sin = jnp.concatenate([-jnp.sin(ang_r), jnp.sin(ang_r), -jnp.sin(ang_c), jnp.sin(ang_c)], axis=-1)
    cos = jnp.concatenate([jnp.ones((C, HEAD_DIM), F32), cos], axis=0)
    sin = jnp.concatenate([jnp.zeros((C, HEAD_DIM), F32), sin], axis=0)
    return cos, sin


def _qkv_proj(xs, modt, w_bf16, cos, sin, n_ctx_tiles):
    B, S, D = xs.shape
    NO = w_bf16.shape[1]
    tm = TOKEN_TILE
    body = functools.partial(_qkv_body, n_rot_heads=N_HEADS + N_KV_HEADS, n_q_heads=N_HEADS,
                             scale=HEAD_DIM ** -0.5)
    seg = lambda j: jnp.where(j >= n_ctx_tiles, 1, 0)
    return pl.pallas_call(
        body,
        grid=(B, S // tm),
        in_specs=[
            pl.BlockSpec((1, tm, D), lambda b, j: (b, j, 0)),
            pl.BlockSpec((1, 1, 6, D), lambda b, j: (b, seg(j), 0, 0)),
            pl.BlockSpec((D, NO), lambda b, j: (0, 0)),
            pl.BlockSpec((tm, HEAD_DIM), lambda b, j: (j, 0)),
            pl.BlockSpec((tm, HEAD_DIM), lambda b, j: (j, 0)),
        ],
        out_specs=pl.BlockSpec((1, tm, NO), lambda b, j: (b, j, 0)),
        out_shape=jax.ShapeDtypeStruct((B, S, NO), BF16),
        compiler_params=_cparams(("arbitrary", "arbitrary")),
        name="qkv_rope",
    )(xs, modt, w_bf16, cos, sin)


def _attn_body(sink_ref, q_ref, kc_ref, vc_ref, k0_ref, k1_ref, k2_ref, v0_ref, v1_ref, v2_ref, o_ref,
               *, n_ctx_blk, n_lat_blk):
    kh = pl.program_id(1)
    qb = pl.program_id(2)
    blk = WINDOW
    rows = GROUP * blk
    q = jnp.concatenate([q_ref[0, :, g * HEAD_DIM:(g + 1) * HEAD_DIM] for g in range(GROUP)], axis=0)
    row = lax.broadcasted_iota(jnp.int32, (rows, 1), 0)
    sink = jnp.full((rows, 1), sink_ref[kh * GROUP + GROUP - 1], F32)
    for g in range(GROUP - 2, -1, -1):
        sink = jnp.where(row < (g + 1) * blk, sink_ref[kh * GROUP + g], sink)
    nt = (((1,), (1,)), ((), ()))
    s_c = lax.dot_general(q, kc_ref[0], nt, preferred_element_type=F32)
    m_c = jnp.maximum(jnp.max(s_c, axis=1, keepdims=True), sink)

    def finish(m, parts):
        denom = jnp.exp(sink - m)
        acc = jnp.zeros((rows, HEAD_DIM), F32)
        for s, v_ref in parts:
            p = jnp.exp(s - m)
            denom = denom + jnp.sum(p, axis=1, keepdims=True)
            acc = acc + jnp.dot(p.astype(BF16), v_ref[0], preferred_element_type=F32)
        o = acc / denom
        for g in range(GROUP):
            o_ref[0, :, g * HEAD_DIM:(g + 1) * HEAD_DIM] = o[g * blk:(g + 1) * blk].astype(o_ref.dtype)

    @pl.when(qb < n_ctx_blk)
    def _():
        finish(m_c, [(s_c, vc_ref)])

    @pl.when(qb >= n_ctx_blk)
    def _():
        n = qb - n_ctx_blk
        a = lax.broadcasted_iota(jnp.int32, (rows, blk), 0) % blk
        key = lax.broadcasted_iota(jnp.int32, (rows, blk), 1)
        s0 = lax.dot_general(q, k0_ref[0], nt, preferred_element_type=F32)
        s0 = jnp.where((key >= a) & (n > 0), s0, -jnp.inf)
        s1 = lax.dot_general(q, k1_ref[0], nt, preferred_element_type=F32)
        s2 = lax.dot_general(q, k2_ref[0], nt, preferred_element_type=F32)
        s2 = jnp.where((key <= a) & (n < n_lat_blk - 1), s2, -jnp.inf)
        m = m_c
        for s in (s0, s1, s2):
            m = jnp.maximum(m, jnp.max(s, axis=1, keepdims=True))
        finish(m, [(s_c, vc_ref), (s0, v0_ref), (s1, v1_ref), (s2, v2_ref)])


def _attention(qkv, sink, C, T):
    B, S, _ = qkv.shape
    blk = WINDOW
    n_ctx_blk = C // blk
    n_lat_blk = T // blk
    k_col = N_HEADS
    v_col = N_HEADS + N_KV_HEADS

    def win(j):
        def im(b, kh, qb):
            n = jnp.maximum(qb - n_ctx_blk, 0)
            return (b, n_ctx_blk + jnp.clip(n + j - 1, 0, n_lat_blk - 1), 0)
        return im

    def kwin(j):
        return lambda b, kh, qb: win(j)(b, kh, qb)[:2] + (k_col + kh,)

    def vwin(j):
        return lambda b, kh, qb: win(j)(b, kh, qb)[:2] + (v_col + kh,)

    kv_blk = (1, blk, HEAD_DIM)
    body = functools.partial(_attn_body, n_ctx_blk=n_ctx_blk, n_lat_blk=n_lat_blk)
    return pl.pallas_call(
        body,
        grid=(B, N_KV_HEADS, S // blk),
        in_specs=[
            pl.BlockSpec(memory_space=pltpu.SMEM),
            pl.BlockSpec((1, blk, GROUP * HEAD_DIM), lambda b, kh, qb: (b, qb, kh)),
            pl.BlockSpec((1, C, HEAD_DIM), lambda b, kh, qb: (b, 0, k_col + kh)),
            pl.BlockSpec((1, C, HEAD_DIM), lambda b, kh, qb: (b, 0, v_col + kh)),
            pl.BlockSpec(kv_blk, kwin(0)), pl.BlockSpec(kv_blk, kwin(1)), pl.BlockSpec(kv_blk, kwin(2)),
            pl.BlockSpec(kv_blk, vwin(0)), pl.BlockSpec(kv_blk, vwin(1)), pl.BlockSpec(kv_blk, vwin(2)),
        ],
        out_specs=pl.BlockSpec((1, blk, GROUP * HEAD_DIM), lambda b, kh, qb: (b, qb, kh)),
        out_shape=jax.ShapeDtypeStruct((B, S, N_HEADS * HEAD_DIM), BF16),
        compiler_params=_cparams(("arbitrary", "arbitrary", "arbitrary")),
        name="window_attention",
    )(sink, qkv, qkv, qkv, qkv, qkv, qkv, qkv, qkv, qkv)


def _layer_norm_rows(r, g, b):
    mu = jnp.mean(r, axis=-1, keepdims=True)
    rc = r - mu
    var = jnp.mean(rc * rc, axis=-1, keepdims=True)
    return rc * lax.rsqrt(var + LN_EPS) * g + b


def _mix_out_body(*refs, alpha, gla):
    if gla:
        of_ref, ob_ref, r_ref, ng_ref, x_ref, mod_ref, w_ref, lng_ref, lnb_ref, xo_ref, h_ref = refs
        o = of_ref[0, 0].astype(F32) + ob_ref[0, 0].astype(F32)
        dv = ng_ref.shape[1]
        parts = []
        for hd in range(GLA_HEADS):
            oh = o[:, hd * dv:(hd + 1) * dv]
            ms = jnp.mean(oh * oh, axis=-1, keepdims=True)
            parts.append(oh * lax.rsqrt(ms + RMS_EPS) * ng_ref[...])
        a = (jnp.concatenate(parts, axis=1) * _silu(r_ref[0].astype(F32))).astype(BF16)
    else:
        a_ref, x_ref, mod_ref, w_ref, lng_ref, lnb_ref, xo_ref, h_ref = refs
        a = a_ref[0]
    m = mod_ref[0, 0]
    y = jnp.dot(a, w_ref[...], preferred_element_type=F32)
    xn = _layer_norm_rows(alpha * x_ref[0] + m[2:3] * y, lng_ref[...], lnb_ref[...])
    xo_ref[0] = xn
    h_ref[0] = xn * (1.0 + m[4:5]) + m[3:4]


def _attn_out(attn, xs, modt, w_bf16, ln_g, ln_b, n_ctx_tiles, alpha):
    B, S, D = xs.shape
    tm = TOKEN_TILE
    seg = lambda j: jnp.where(j >= n_ctx_tiles, 1, 0)
    tok = pl.BlockSpec((1, tm, D), lambda b, j: (b, j, 0))
    vec = pl.BlockSpec((1, D), lambda b, j: (0, 0))
    return pl.pallas_call(
        functools.partial(_mix_out_body, alpha=alpha, gla=False),
        grid=(B, S // tm),
        in_specs=[tok, tok, pl.BlockSpec((1, 1, 6, D), lambda b, j: (b, seg(j), 0, 0)),
                  pl.BlockSpec((D, D), lambda b, j: (0, 0)), vec, vec],
        out_specs=[tok, tok],
        out_shape=[jax.ShapeDtypeStruct((B, S, D), F32), jax.ShapeDtypeStruct((B, S, D), F32)],
        compiler_params=_cparams(("arbitrary", "arbitrary")),
        name="attn_out_ln",
    )(attn, xs, modt, w_bf16, ln_g.reshape(1, D), ln_b.reshape(1, D))


def _gla_out(o2, proj, norm_g, xs, modt, w_bf16, ln_g, ln_b, n_ctx_tiles, T, alpha):
    B, S, D = xs.shape
    tm = TOKEN_TILE
    dv = D // GLA_HEADS
    r_col = proj.shape[2] // D - 1
    off = n_ctx_tiles
    tok_out = pl.BlockSpec((1, tm, D), lambda b, j: (b, j, 0))
    vec = pl.BlockSpec((1, D), lambda b, j: (0, 0))
    return pl.pallas_call(
        functools.partial(_mix_out_body, alpha=alpha, gla=True),
        grid=(B, T // tm),
        in_specs=[
            pl.BlockSpec((1, 1, tm, D), lambda b, j: (0, b, j + off, 0)),
            pl.BlockSpec((1, 1, tm, D), lambda b, j: (1, b, j + off, 0)),
            pl.BlockSpec((1, tm, D), lambda b, j: (b, j + off, r_col)),
            pl.BlockSpec((1, dv), lambda b, j: (0, 0)),
            pl.BlockSpec((1, tm, D), lambda b, j: (b, j + off, 0)),
            pl.BlockSpec((1, 1, 6, D), lambda b, j: (b, 1, 0, 0)),
            pl.BlockSpec((D, D), lambda b, j: (0, 0)), vec, vec,
        ],
        out_specs=[tok_out, tok_out],
        out_shape=[jax.ShapeDtypeStruct((B, T, D), F32), jax.ShapeDtypeStruct((B, T, D), F32)],
        compiler_params=_cparams(("arbitrary", "arbitrary")),
        name="gla_out_ln",
    )(o2, o2, proj, norm_g.reshape(1, dv), xs, modt, w_bf16, ln_g.reshape(1, D), ln_b.reshape(1, D))


def _router_body(h_ref, w_ref, b_ref, id_ref, wt_ref):
    logits = jnp.dot(h_ref[...].astype(BF16), w_ref[...], preferred_element_type=F32) + b_ref[...]
    lane = lax.broadcasted_iota(jnp.int32, logits.shape, 1)
    neg = -jnp.inf

    def first_max(vals):
        mx = jnp.max(vals, axis=1, keepdims=True)
        idx = jnp.min(jnp.where(vals == mx, lane, LANES), axis=1, keepdims=True)
        return mx, idx

    gmask = lane < MOE_GROUPS
    gl = jnp.where(gmask, logits, neg)
    gmax, gidx = first_max(gl)
    gsum = jnp.sum(jnp.where(gmask, jnp.exp(gl - gmax), 0.0), axis=1, keepdims=True)
    g_w = 1.0 / gsum
    lo = MOE_GROUPS + gidx * MOE_EXPERTS_PER_GROUP
    el = jnp.where((lane >= lo) & (lane < lo + MOE_EXPERTS_PER_GROUP), logits, neg)
    v1, i1 = first_max(el)
    v2, i2 = first_max(jnp.where(lane == i1, neg, el))
    e2 = jnp.exp(v2 - v1)
    w1 = g_w / (1.0 + e2)
    w2 = g_w * e2 / (1.0 + e2)
    id_ref[...] = jnp.where(lane == 0, i1 - MOE_GROUPS, jnp.where(lane == 1, i2 - MOE_GROUPS, 0))
    wt_ref[...] = jnp.where(lane == 0, w1, jnp.where(lane == 1, w2, 0.0))


def _router(h, wr_bf16, br):
    N, D = h.shape
    tm = 512
    return pl.pallas_call(
        _router_body,
        grid=(N // tm,),
        in_specs=[pl.BlockSpec((tm, D), lambda i: (i, 0)),
                  pl.BlockSpec((D, LANES), lambda i: (0, 0)),
                  pl.BlockSpec((1, LANES), lambda i: (0, 0))],
        out_specs=[pl.BlockSpec((tm, LANES), lambda i: (i, 0)), pl.BlockSpec((tm, LANES), lambda i: (i, 0))],
        out_shape=[jax.ShapeDtypeStruct((N, LANES), jnp.int32), jax.ShapeDtypeStruct((N, LANES), F32)],
        compiler_params=_cparams(("arbitrary",)),
        name="moe_router",
    )(h, wr_bf16, br)


def _moe_body(te_ref, nv_ref, nu_ref, tok_ref, tokn_ref, dst_ref, rw_ref, h_hbm, wg_ref, wu_ref, wd_ref,
              y_hbm, xbuf, ybuf, wgb, wub, wdb, gsem, ssem):
    i = pl.program_id(0)
    n_used = nu_ref[0]
    slot = i % 2

    def gather(idx_ref, n_rows, s):
        def issue(r, c):
            pltpu.make_async_copy(h_hbm.at[pl.ds(idx_ref[0, 0, r], 1), :],
                                  xbuf.at[s, pl.ds(r, 1), :], gsem.at[s]).start()
            return c
        lax.fori_loop(0, n_rows, issue, 0)

    def gather_wait(n_rows, s):
        n_rows = pl.multiple_of(n_rows, 8)
        pltpu.make_async_copy(h_hbm.at[pl.ds(0, n_rows), :], xbuf.at[s, pl.ds(0, n_rows), :], gsem.at[s]).wait()

    def scatter(n_rows, s):
        def issue(r, c):
            pltpu.make_async_copy(ybuf.at[s, pl.ds(r, 1), :],
                                  y_hbm.at[pl.ds(dst_ref[0, 0, r], 1), :], ssem.at[s]).start()
            return c
        lax.fori_loop(0, n_rows, issue, 0)

    def scatter_wait(n_rows, s):
        n_rows = pl.multiple_of(n_rows, 8)
        pltpu.make_async_copy(ybuf.at[s, pl.ds(0, n_rows), :], y_hbm.at[pl.ds(0, n_rows), :], ssem.at[s]).wait()

    @pl.when(i == 0)
    def _():
        xbuf[...] = jnp.zeros(xbuf.shape, xbuf.dtype)
        tm = xbuf.shape[1]
        spare = y_hbm.shape[0] - 2 * tm
        for s in range(2):
            cp = pltpu.make_async_copy(xbuf.at[s], y_hbm.at[pl.ds(spare + s * tm, tm), :], ssem.at[s])
            cp.start()
            cp.wait()
        gather(tok_ref, nv_ref[0], 0)

    @pl.when(i + 1 < n_used)
    def _():
        gather(tokn_ref, nv_ref[i + 1], 1 - slot)

    @pl.when(i < n_used)
    def _():
        nv = nv_ref[i]
        gather_wait(nv, slot)

        @pl.when((i == 0) | (te_ref[i] != te_ref[jnp.maximum(i - 1, 0)]))
        def _():
            wgb[...] = wg_ref[0].astype(BF16)
            wub[...] = wu_ref[0].astype(BF16)
            wdb[...] = wd_ref[0].astype(BF16)

        @pl.when(i >= 2)
        def _():
            scatter_wait(nv_ref[jnp.maximum(i - 2, 0)], slot)

        x = xbuf[slot].astype(BF16)
        g = jnp.dot(x, wgb[...], preferred_element_type=F32)
        u = jnp.dot(x, wub[...], preferred_element_type=F32)
        hid = (_silu(g) * u).astype(BF16)
        y = jnp.dot(hid, wdb[...], preferred_element_type=F32)
        ybuf[slot] = y * rw_ref[:, 0:1]
        scatter(nv, slot)

        @pl.when(i == n_used - 1)
        def _():
            @pl.when(i >= 1)
            def _():
                scatter_wait(nv_ref[jnp.maximum(i - 1, 0)], 1 - slot)
            scatter_wait(nv, slot)


def _moe_experts(h, tile_expert, tile_valid, n_used, row_tok, row_dst, row_w, w_gate, w_up, w_down):
    N, D = h.shape
    E, _, Hd = w_gate.shape
    tm = MOE_TILE
    n_tiles = tile_expert.shape[0]
    idx_blk = lambda f: pl.BlockSpec((1, 1, tm), f, memory_space=pltpu.SMEM)
    grid_spec = pltpu.PrefetchScalarGridSpec(
        num_scalar_prefetch=3,
        grid=(n_tiles,),
        in_specs=[
            idx_blk(lambda i, te, nv, nu: (i, 0, 0)),
            idx_blk(lambda i, te, nv, nu: (jnp.minimum(i + 1, n_tiles - 1), 0, 0)),
            idx_blk(lambda i, te, nv, nu: (i, 0, 0)),
            pl.BlockSpec((tm, LANES), lambda i, te, nv, nu: (i, 0)),
            pl.BlockSpec(memory_space=pl.ANY),
            pl.BlockSpec((1, D, Hd), lambda i, te, nv, nu: (te[i], 0, 0)),
            pl.BlockSpec((1, D, Hd), lambda i, te, nv, nu: (te[i], 0, 0)),
            pl.BlockSpec((1, Hd, D), lambda i, te, nv, nu: (te[i], 0, 0)),
        ],
        out_specs=pl.BlockSpec(memory_space=pl.ANY),
        scratch_shapes=[
            pltpu.VMEM((2, tm, D), F32), pltpu.VMEM((2, tm, D), F32),
            pltpu.VMEM((D, Hd), BF16), pltpu.VMEM((D, Hd), BF16), pltpu.VMEM((Hd, D), BF16),
            pltpu.SemaphoreType.DMA((2,)), pltpu.SemaphoreType.DMA((2,)),
        ],
    )
    return pl.pallas_call(
        _moe_body,
        grid_spec=grid_spec,
        out_shape=jax.ShapeDtypeStruct((2 * N + 2 * tm, D), F32),
        compiler_params=_cparams(("arbitrary",)),
        name="moe_experts",
    )(tile_expert, tile_valid, n_used, row_tok, row_tok, row_dst, row_w, h, w_gate, w_up, w_down)


def _moe_plan(ids, wts, N):
    tm = MOE_TILE
    A = 2 * N
    n_tiles = A // tm + N_EXPERTS
    P = n_tiles * tm
    e_flat = ids.reshape(A)
    a_idx = jnp.arange(A, dtype=jnp.int32)
    e_s, a_s, w_s = lax.sort((e_flat, a_idx, wts.reshape(A)), num_keys=1, is_stable=True)
    bounds = jnp.searchsorted(e_s, jnp.arange(N_EXPERTS + 1, dtype=jnp.int32), side="left").astype(jnp.int32)
    counts = bounds[1:] - bounds[:-1]
    tiles_per = (counts + tm - 1) // tm
    tile_end = jnp.cumsum(tiles_per)
    tile_start = tile_end - tiles_per
    n_used = tile_end[-1]
    t_idx = jnp.arange(n_tiles, dtype=jnp.int32)
    te = jnp.minimum(jnp.searchsorted(tile_end, t_idx, side="right"), N_EXPERTS - 1).astype(jnp.int32)
    last_e = te[jnp.maximum(n_used - 1, 0)]
    te = jnp.where(t_idx < n_used, te, last_e)
    in_tile = t_idx - tile_start[te]
    tile_valid = jnp.where(t_idx < n_used, jnp.clip(counts[te] - in_tile * tm, 0, tm), 0).astype(jnp.int32)
    r = jnp.arange(P, dtype=jnp.int32)
    rt = r // tm
    src = bounds[te[rt]] + in_tile[rt] * tm + r % tm
    valid = (r % tm) < tile_valid[rt]
    src = jnp.where(valid, src, 0)
    a_r = a_s[src]
    row_tok = jnp.where(valid, a_r // 2, 0).astype(jnp.int32)
    spare = A + (rt % 2) * tm + r % tm
    row_dst = jnp.where(valid, (a_r % 2) * N + a_r // 2, spare).astype(jnp.int32)
    row_w = jnp.where(valid, w_s[src], 0.0)
    row_w = jnp.broadcast_to(row_w[:, None], (P, LANES))
    tile_rows = (tile_valid + 7) // 8 * 8
    return (te, tile_rows, n_used.reshape(1).astype(jnp.int32),
            row_tok.reshape(n_tiles, 1, tm), row_dst.reshape(n_tiles, 1, tm), row_w)


def _moe_out_body(y0_ref, y1_ref, x_ref, mod_ref, lng_ref, lnb_ref, o_ref, *, alpha):
    m = mod_ref[0, 0]
    f = y0_ref[...] + y1_ref[...]
    o_ref[0] = _layer_norm_rows(alpha * x_ref[0] + m[5:6] * f, lng_ref[...], lnb_ref[...])


def _moe_out(y, xs, modt, ln_g, ln_b, n_ctx_tiles, alpha):
    B, R, D = xs.shape
    tm = TOKEN_TILE
    rt = R // tm
    if n_ctx_tiles < 0:
        seg = lambda j: 1
    else:
        seg = lambda j: jnp.where(j >= n_ctx_tiles, 1, 0)
    tok = pl.BlockSpec((1, tm, D), lambda b, j: (b, j, 0))
    vec = pl.BlockSpec((1, D), lambda b, j: (0, 0))
    return pl.pallas_call(
        functools.partial(_moe_out_body, alpha=alpha),
        grid=(B, R // tm),
        in_specs=[pl.BlockSpec((tm, D), lambda b, j: (b * rt + j, 0)),
                  pl.BlockSpec((tm, D), lambda b, j: (B * rt + b * rt + j, 0)),
                  tok, pl.BlockSpec((1, 1, 6, D), lambda b, j: (b, seg(j), 0, 0)), vec, vec],
        out_specs=tok,
        out_shape=jax.ShapeDtypeStruct((B, R, D), F32),
        compiler_params=_cparams(("arbitrary", "arbitrary")),
        name="moe_out_ln",
    )(y, y, xs, modt, ln_g.reshape(1, D), ln_b.reshape(1, D))


def _hier_moe(h, xs, modt, wg, bg, we, be, w_gate, w_up, w_down, ln_g, ln_b, n_ctx_tiles, alpha):
    B, R, D = xs.shape
    N = B * R
    hf = h.reshape(N, D)
    pad = LANES - MOE_GROUPS - N_EXPERTS
    wr = jnp.concatenate([wg, we, jnp.zeros((D, pad), F32)], axis=1).astype(BF16)
    br = jnp.concatenate([bg, be, jnp.zeros((pad,), F32)]).reshape(1, LANES)
    ids, wts = _router(hf, wr, br)
    plan = _moe_plan(ids[:, :2], wts[:, :2], N)
    y = _moe_experts(hf, *plan, w_gate, w_up, w_down)
    return _moe_out(y, xs, modt, ln_g, ln_b, n_ctx_tiles, alpha)


def _gla_proj_body(x_ref, mod_ref, w_ref, o_ref, *, q_cols, q_scale):
    m = mod_ref[0, 0]
    z = (x_ref[0] * (1.0 + m[1:2]) + m[0:1]).astype(BF16)
    acc = jnp.dot(z, w_ref[...], preferred_element_type=F32)
    half = pl.program_id(0)

    @pl.when(half == 0)
    def _():
        o_ref[0, :, :q_cols] = (acc[:, :q_cols] * q_scale).astype(BF16)
        o_ref[0, :, q_cols:] = acc[:, q_cols:].astype(BF16)

    @pl.when(half != 0)
    def _():
        o_ref[0] = acc.astype(BF16)


def _gla_proj(xs, modt, w_bf16, n_ctx_tiles, q_cols, q_scale):
    B, S, D = xs.shape
    NO = w_bf16.shape[1]
    tm = TOKEN_TILE
    seg = lambda j: jnp.where(j >= n_ctx_tiles, 1, 0)
    return pl.pallas_call(
        functools.partial(_gla_proj_body, q_cols=q_cols, q_scale=q_scale),
        grid=(2, B, S // tm),
        in_specs=[
            pl.BlockSpec((1, tm, D), lambda hf, b, j: (b, j, 0)),
            pl.BlockSpec((1, 1, 6, D), lambda hf, b, j: (b, seg(j), 0, 0)),
            pl.BlockSpec((D, NO // 2), lambda hf, b, j: (0, hf)),
        ],
        out_specs=pl.BlockSpec((1, tm, NO // 2), lambda hf, b, j: (b, j, hf)),
        out_shape=jax.ShapeDtypeStruct((B, S, NO), BF16),
        compiler_params=_cparams(("arbitrary", "arbitrary", "arbitrary")),
        name="gla_proj",
    )(xs, modt, w_bf16)


def _gla_gate_body(x_ref, mod_ref, w1_ref, w2_ref, b_ref, o_ref):
    m = mod_ref[0, 0]
    z = (x_ref[0] * (1.0 + m[1:2]) + m[0:1]).astype(BF16)
    t = jnp.dot(z, w1_ref[...], preferred_element_type=F32)
    pre = jnp.dot(t.astype(BF16), w2_ref[...], preferred_element_type=F32) + b_ref[...]
    o_ref[0] = (jnp.minimum(pre, 0.0) - jnp.log(1.0 + jnp.exp(-jnp.abs(pre)))) * (1.0 / GLA_TAU)


def _gla_gates(xs, modt, w1, w2, gb, n_ctx_tiles):
    B, S, D = xs.shape
    rank = w1.shape[2]
    kd = w2.shape[2]
    tm = TOKEN_TILE
    w1c = jnp.zeros((D, LANES), F32).at[:, :rank].set(w1[0]).at[:, rank:2 * rank].set(w1[1]).astype(BF16)
    w2c = (jnp.zeros((LANES, 2 * kd), F32).at[:rank, :kd].set(w2[0]).at[rank:2 * rank, kd:].set(w2[1])
           .astype(BF16))
    seg = lambda j: jnp.where(j >= n_ctx_tiles, 1, 0)
    return pl.pallas_call(
        _gla_gate_body,
        grid=(B, S // tm),
        in_specs=[
            pl.BlockSpec((1, tm, D), lambda b, j: (b, j, 0)),
            pl.BlockSpec((1, 1, 6, D), lambda b, j: (b, seg(j), 0, 0)),
            pl.BlockSpec((D, LANES), lambda b, j: (0, 0)),
            pl.BlockSpec((LANES, 2 * kd), lambda b, j: (0, 0)),
            pl.BlockSpec((1, 2 * kd), lambda b, j: (0, 0)),
        ],
        out_specs=pl.BlockSpec((1, tm, 2 * kd), lambda b, j: (b, j, 0)),
        out_shape=jax.ShapeDtypeStruct((B, S, 2 * kd), F32),
        compiler_params=_cparams(("arbitrary", "arbitrary")),
        name="gla_gates",
    )(xs, modt, w1c, w2c, gb.reshape(1, 2 * kd))


GLA_LEVELS = (32, 16, 8, 4, 2, 1)


def _gla_constants():
    L = GLA_CHUNK
    t = np.arange(L)[:, None]
    u = np.arange(L)[None, :]
    mats, masks = [], []
    for bwd in (False, True):
        blocks = []
        blocks.append((u >= t) if bwd else (u <= t))
        blocks.append((u < t) if bwd else (u > t))
        q_lv, k_lv, m_lv = [], [], []
        for m in GLA_LEVELS:
            base = (t // (2 * m)) * (2 * m)
            ubase = (u // (2 * m)) * (2 * m)
            t_hi = (t % (2 * m)) >= m
            if not bwd:
                r = base + m - 1
                q_lv.append(t_hi & (u > r) & (u <= t))
                k_lv.append(~t_hi & (u > t) & (u <= r))
                m_lv.append(t_hi & ((u % (2 * m)) < m) & (base == ubase))
            else:
                r = base + m
                q_lv.append(~t_hi & (u >= t) & (u < r))
                k_lv.append(t_hi & (u >= r) & (u < t))
                m_lv.append(~t_hi & ((u % (2 * m)) >= m) & (base == ubase))
        blocks += q_lv + k_lv
        blocks.append(np.ones((8, L), bool))
        a = np.concatenate(blocks, axis=0).astype(np.float32)
        mats.append(np.concatenate([a, a, a], axis=1))
        m_lv.append(t == u)
        masks.append(np.concatenate(m_lv, axis=0).astype(np.float32))
    return np.stack(mats), np.stack(masks)


def _gla_scan_body(q_ref, k_ref, v_ref, g_ref, a_ref, mk_ref, o_ref, st_ref):
    L = GLA_CHUNK
    nlv = len(GLA_LEVELS)

    @pl.when(pl.program_id(3) == 0)
    def _():
        st_ref[...] = jnp.zeros(st_ref.shape, st_ref.dtype)

    q = q_ref[0]
    k = k_ref[0]
    v = v_ref[0]
    g = g_ref[0]
    g_hi = g.astype(BF16)
    r1 = g - g_hi.astype(F32)
    g_mid = r1.astype(BF16)
    g_lo = (r1 - g_mid.astype(F32)).astype(BF16)
    gs = jnp.concatenate([g_hi, g_mid, g_lo], axis=0)
    ex = jnp.exp(jnp.dot(a_ref[0], gs, preferred_element_type=F32))
    qf = q.astype(F32)
    kf = k.astype(F32)
    q_inter = (qf * ex[0:L]).astype(BF16)
    k_state = (kf * ex[L:2 * L]).astype(BF16)
    dec = ex[(2 + 2 * nlv) * L:(2 + 2 * nlv) * L + 1]
    nt = (((1,), (1,)), ((), ()))
    att = lax.dot_general(q, k, nt, preferred_element_type=F32) * mk_ref[0, nlv * L:(nlv + 1) * L]
    for lv in range(nlv):
        ql = (qf * ex[(2 + lv) * L:(3 + lv) * L]).astype(BF16)
        kl = (kf * ex[(2 + nlv + lv) * L:(3 + nlv + lv) * L]).astype(BF16)
        att = att + lax.dot_general(ql, kl, nt, preferred_element_type=F32) * mk_ref[0, lv * L:(lv + 1) * L]
    st = st_ref[...]
    o = jnp.dot(att.astype(BF16), v, preferred_element_type=F32)
    o = o + lax.dot_general(q_inter, st.astype(BF16), nt, preferred_element_type=F32)
    o_ref[0, 0] = o.astype(o_ref.dtype)
    upd = lax.dot_general(v, k_state, (((0,), (0,)), ((), ())), preferred_element_type=F32)
    st_ref[...] = st * dec + upd


def _gla_scan(proj, lg, C, T):
    B, S, NO = proj.shape
    kd = lg.shape[2] // 2
    D = NO - 2 * kd
    D = D // 2
    dk = kd // GLA_HEADS
    dv = D // GLA_HEADS
    L = GLA_CHUNK
    n_ctx = C // L
    n_all = S // L
    amat, masks = _gla_constants()
    amat = jnp.asarray(amat, BF16)
    masks = jnp.asarray(masks, F32)

    def chunk(d, c):
        back = jnp.where(c < n_ctx, n_ctx - 1 - c, n_ctx + n_all - 1 - c)
        return jnp.where(d == 0, c, back)

    k_col = kd // dk
    v_col = 2 * kd // dv
    g_cols = kd // dk
    return pl.pallas_call(
        _gla_scan_body,
        grid=(B, GLA_HEADS, 2, n_all),
        in_specs=[
            pl.BlockSpec((1, L, dk), lambda b, h, d, c: (b, chunk(d, c), h)),
            pl.BlockSpec((1, L, dk), lambda b, h, d, c: (b, chunk(d, c), k_col + h)),
            pl.BlockSpec((1, L, dv), lambda b, h, d, c: (b, chunk(d, c), v_col + h)),
            pl.BlockSpec((1, L, dk), lambda b, h, d, c: (b, chunk(d, c), d * g_cols + h)),
            pl.BlockSpec((1,) + amat.shape[1:], lambda b, h, d, c: (d, 0, 0)),
            pl.BlockSpec((1,) + masks.shape[1:], lambda b, h, d, c: (d, 0, 0)),
        ],
        out_specs=pl.BlockSpec((1, 1, L, dv), lambda b, h, d, c: (d, b, chunk(d, c), h)),
        out_shape=jax.ShapeDtypeStruct((2, B, S, D), BF16),
        scratch_shapes=[pltpu.VMEM((dv, dk), F32)],
        compiler_params=_cparams(("arbitrary", "arbitrary", "arbitrary", "arbitrary")),
        name="gla_scan",
    )(proj, proj, proj, lg, amat, masks)


def kernel(x, c, ctx, c_ctx, mod_w, mod_b, ln_g, ln_b, attn_w_qkv, attn_w_o, attn_sink,
           gla_w_in, gla_gate_w1, gla_gate_w2, gla_gate_b, gla_norm_g, gla_w_o,
           moe_group_w, moe_group_b, moe_router_w, moe_router_b, moe_w_gate, moe_w_up, moe_w_down):
    B, T, D = x.shape
    C = ctx.shape[1]
    depth = mod_w.shape[0]
    assert depth == 2 and D == N_HEADS * HEAD_DIM
    assert C % TOKEN_TILE == 0 and T % TOKEN_TILE == 0 and T % GRID_W == 0
    alpha = (2 * depth) ** 0.25
    n_ctx_tiles = C // TOKEN_TILE

    m_rows = -(-(B + 1) // 8) * 8
    cond = jnp.zeros((m_rows, D), F32).at[:B].set(c).at[B].set(c_ctx)
    mods = _adaln_mods(cond, mod_w, mod_b)

    def mod_table(l):
        lat = mods[l, :B].reshape(B, 1, 6, D)
        cm = jnp.broadcast_to(mods[l, B].reshape(1, 1, 6, D), (B, 1, 6, D))
        return jnp.concatenate([cm, lat], axis=1)

    xs = jnp.concatenate([ctx, x], axis=1)

    modt = mod_table(0)
    cos, sin = _rope_tables(C, T)
    qkv = _qkv_proj(xs, modt, attn_w_qkv[0].astype(BF16), cos, sin, n_ctx_tiles)
    attn = _attention(qkv, attn_sink[0], C, T)
    xs, h = _attn_out(attn, xs, modt, attn_w_o[0].astype(BF16), ln_g[0, 0], ln_b[0, 0], n_ctx_tiles, alpha)
    xs = _hier_moe(h, xs, modt, moe_group_w[0], moe_group_b[0], moe_router_w[0], moe_router_b[0],
                   moe_w_gate[0], moe_w_up[0], moe_w_down[0], ln_g[0, 1], ln_b[0, 1], n_ctx_tiles, alpha)

    modt = mod_table(1)
    kd = gla_gate_w2.shape[3]
    dk = kd // GLA_HEADS
    proj = _gla_proj(xs, modt, gla_w_in[0].astype(BF16), n_ctx_tiles, kd, dk ** -0.5)
    lg = _gla_gates(xs, modt, gla_gate_w1[0], gla_gate_w2[0], gla_gate_b[0], n_ctx_tiles)
    o2 = _gla_scan(proj, lg, C, T)
    xl, h = _gla_out(o2, proj, gla_norm_g[0], xs, modt, gla_w_o[0].astype(BF16), ln_g[1, 0], ln_b[1, 0],
                     n_ctx_tiles, T, alpha)
    return _hier_moe(h, xl, modt, moe_group_w[1], moe_group_b[1], moe_router_w[1], moe_router_b[1],
                     moe_w_gate[1], moe_w_up[1], moe_w_down[1], ln_g[1, 1], ln_b[1, 1], -1, alpha)
```

```python
import functools

import numpy as np
import jax
import jax.numpy as jnp
from jax import lax
from jax.experimental import pallas as pl
from jax.experimental.pallas import tpu as pltpu

F32 = jnp.float32
BF16 = jnp.bfloat16

N_HEADS = 16
N_KV_HEADS = 4
HEAD_DIM = 128
GROUP = N_HEADS // N_KV_HEADS
WINDOW = 128
GRID_W = 64
ROPE_BASE = 10000.0
GLA_HEADS = 4
GLA_TAU = 16.0
GLA_CHUNK = 64
MOE_GROUPS = 4
MOE_EXPERTS_PER_GROUP = 8
N_EXPERTS = MOE_GROUPS * MOE_EXPERTS_PER_GROUP
LN_EPS = 1e-5
RMS_EPS = 1e-6

V7X_VMEM_LIMIT_BYTES = 56 * 1024 * 1024
LANES = 128
TOKEN_TILE = 256
MOE_TILE = 256
ROUTER_TILE = 512
MODS_COL_TILE = 1024


def _cparams(sem):
    return pltpu.CompilerParams(dimension_semantics=sem, vmem_limit_bytes=V7X_VMEM_LIMIT_BYTES)


def _silu(v):
    return v / (1.0 + jnp.exp(-v))


def _mods_body(cond_ref, w_ref, b_ref, o_ref):
    a = _silu(cond_ref[...]).astype(BF16)
    o_ref[0] = jnp.dot(a, w_ref[0].astype(BF16), preferred_element_type=F32) + b_ref[0]


def _adaln_mods(cond, mod_w, mod_b):
    L, D, N6 = mod_w.shape
    M = cond.shape[0]
    tn = MODS_COL_TILE
    return pl.pallas_call(
        _mods_body,
        grid=(L, N6 // tn),
        in_specs=[
            pl.BlockSpec((M, D), lambda l, j: (0, 0)),
            pl.BlockSpec((1, D, tn), lambda l, j: (l, 0, j)),
            pl.BlockSpec((1, 1, tn), lambda l, j: (l, 0, j)),
        ],
        out_specs=pl.BlockSpec((1, M, tn), lambda l, j: (l, 0, j)),
        out_shape=jax.ShapeDtypeStruct((L, M, N6), F32),
        compiler_params=_cparams(("arbitrary", "arbitrary")),
        name="adaln_mods",
    )(cond, mod_w, mod_b.reshape(L, 1, N6))


def _stream_rows(c_ref, x_ref, n_ctx_tiles):
    return jnp.where(pl.program_id(1) < n_ctx_tiles, c_ref[0], x_ref[0])


def _stream_specs(tm, D, n_ctx_tiles):
    return [pl.BlockSpec((1, tm, D), lambda b, j: (b, jnp.minimum(j, n_ctx_tiles - 1), 0)),
            pl.BlockSpec((1, tm, D), lambda b, j: (b, jnp.maximum(j - n_ctx_tiles, 0), 0))]


def _qkv_body(c_ref, x_ref, mod_ref, w_ref, cos_ref, sin_ref, o_ref, *, n_ctx_tiles, n_rot_heads, n_q_heads,
              scale):
    m = mod_ref[0, 0]
    h = (_stream_rows(c_ref, x_ref, n_ctx_tiles) * (1.0 + m[1:2]) + m[0:1]).astype(BF16)
    acc = jnp.dot(h, w_ref[...], preferred_element_type=F32)
    cos = cos_ref[...]
    sin = sin_ref[...]
    lane = lax.broadcasted_iota(jnp.int32, cos.shape, 1)
    first = (lane & 32) == 0
    for hd in range(n_rot_heads):
        y = acc[:, hd * HEAD_DIM:(hd + 1) * HEAD_DIM]
        partner = jnp.where(first, pltpu.roll(y, HEAD_DIM - 32, 1), pltpu.roll(y, 32, 1))
        r = y * cos + partner * sin
        if hd < n_q_heads:
            r = r * scale
        o_ref[0, :, hd * HEAD_DIM:(hd + 1) * HEAD_DIM] = r.astype(BF16)
    rest = n_rot_heads * HEAD_DIM
    o_ref[0, :, rest:] = acc[:, rest:].astype(BF16)


def _rope_tables(C, T):
    half = HEAD_DIM // 2
    pos = np.arange(T)
    inv_freq = ROPE_BASE ** (-np.arange(0, half, 2, dtype=np.float32) / half)
    ang_r = (pos // GRID_W).astype(np.float32)[:, None] * inv_freq
    ang_c = (pos % GRID_W).astype(np.float32)[:, None] * inv_freq
    ang_r = jnp.asarray(ang_r, F32)
    ang_c = jnp.asarray(ang_c, F32)
    cos = jnp.concatenate([jnp.cos(ang_r)] * 2 + [jnp.cos(ang_c)] * 2, axis=-1)
    sin = jnp.concatenate([-jnp.sin(ang_r), jnp.sin(ang_r), -jnp.sin(ang_c), jnp.sin(ang_c)], axis=-1)
    cos = jnp.concatenate([jnp.ones((C, HEAD_DIM), F32), cos], axis=0)
    sin = jnp.concatenate([jnp.zeros((C, HEAD_DIM), F32), sin], axis=0)
    return cos, sin


def _qkv_proj(ctx, x, modt, w_bf16, cos, sin, n_ctx_tiles):
    B, T, D = x.shape
    S = ctx.shape[1] + T
    NO = w_bf16.shape[1]
    tm = TOKEN_TILE
    body = functools.partial(_qkv_body, n_ctx_tiles=n_ctx_tiles, n_rot_heads=N_HEADS + N_KV_HEADS,
                             n_q_heads=N_HEADS, scale=HEAD_DIM ** -0.5)
    seg = lambda j: jnp.where(j >= n_ctx_tiles, 1, 0)
    return pl.pallas_call(
        body,
        grid=(B, S // tm),
        in_specs=_stream_specs(tm, D, n_ctx_tiles) + [
            pl.BlockSpec((1, 1, 6, D), lambda b, j: (b, seg(j), 0, 0)),
            pl.BlockSpec((D, NO), lambda b, j: (0, 0)),
            pl.BlockSpec((tm, HEAD_DIM), lambda b, j: (j, 0)),
            pl.BlockSpec((tm, HEAD_DIM), lambda b, j: (j, 0)),
        ],
        out_specs=pl.BlockSpec((1, tm, NO), lambda b, j: (b, j, 0)),
        out_shape=jax.ShapeDtypeStruct((B, S, NO), BF16),
        compiler_params=_cparams(("arbitrary", "arbitrary")),
        name="qkv_rope",
    )(ctx, x, modt, w_bf16, cos, sin)


def _attn_body(sink_ref, q_ref, kc_ref, vc_ref, k0_ref, k1_ref, k2_ref, v0_ref, v1_ref, v2_ref, o_ref,
               *, n_ctx_blk, n_lat_blk):
    kh = pl.program_id(1)
    qb = pl.program_id(2)
    blk = WINDOW
    rows = GROUP * blk
    q = jnp.concatenate([q_ref[0, :, g * HEAD_DIM:(g + 1) * HEAD_DIM] for g in range(GROUP)], axis=0)
    row = lax.broadcasted_iota(jnp.int32, (rows, 1), 0)
    sink = jnp.full((rows, 1), sink_ref[kh * GROUP + GROUP - 1], F32)
    for g in range(GROUP - 2, -1, -1):
        sink = jnp.where(row < (g + 1) * blk, sink_ref[kh * GROUP + g], sink)
    nt = (((1,), (1,)), ((), ()))
    s_c = lax.dot_general(q, kc_ref[0], nt, preferred_element_type=F32)
    m_c = jnp.maximum(jnp.max(s_c, axis=1, keepdims=True), sink)

    def finish(m, parts):
        denom = jnp.exp(sink - m)
        acc = jnp.zeros((rows, HEAD_DIM), F32)
        for s, v_ref in parts:
            p = jnp.exp(s - m)
            denom = denom + jnp.sum(p, axis=1, keepdims=True)
            acc = acc + jnp.dot(p.astype(BF16), v_ref[0], preferred_element_type=F32)
        o = acc / denom
        for g in range(GROUP):
            o_ref[0, :, g * HEAD_DIM:(g + 1) * HEAD_DIM] = o[g * blk:(g + 1) * blk].astype(o_ref.dtype)

    @pl.when(qb < n_ctx_blk)
    def _():
        finish(m_c, [(s_c, vc_ref)])

    @pl.when(qb >= n_ctx_blk)
    def _():
        n = qb - n_ctx_blk
        a = lax.broadcasted_iota(jnp.int32, (rows, blk), 0) % blk
        key = lax.broadcasted_iota(jnp.int32, (rows, blk), 1)
        s0 = lax.dot_general(q, k0_ref[0], nt, preferred_element_type=F32)
        s0 = jnp.where((key >= a) & (n > 0), s0, -jnp.inf)
        s1 = lax.dot_general(q, k1_ref[0], nt, preferred_element_type=F32)
        s2 = lax.dot_general(q, k2_ref[0], nt, preferred_element_type=F32)
        s2 = jnp.where((key <= a) & (n < n_lat_blk - 1), s2, -jnp.inf)
        m = m_c
        for s in (s0, s1, s2):
            m = jnp.maximum(m, jnp.max(s, axis=1, keepdims=True))
        finish(m, [(s_c, vc_ref), (s0, v0_ref), (s1, v1_ref), (s2, v2_ref)])


def _attention(qkv, sink, C, T):
    B, S, _ = qkv.shape
    blk = WINDOW
    n_ctx_blk = C // blk
    n_lat_blk = T // blk
    k_col = N_HEADS
    v_col = N_HEADS + N_KV_HEADS

    def win(j):
        def im(b, kh, qb):
            n = jnp.maximum(qb - n_ctx_blk, 0)
            return (b, n_ctx_blk + jnp.clip(n + j - 1, 0, n_lat_blk - 1), 0)
        return im

    def kwin(j):
        return lambda b, kh, qb: win(j)(b, kh, qb)[:2] + (k_col + kh,)

    def vwin(j):
        return lambda b, kh, qb: win(j)(b, kh, qb)[:2] + (v_col + kh,)

    kv_blk = (1, blk, HEAD_DIM)
    body = functools.partial(_attn_body, n_ctx_blk=n_ctx_blk, n_lat_blk=n_lat_blk)
    return pl.pallas_call(
        body,
        grid=(B, N_KV_HEADS, S // blk),
        in_specs=[
            pl.BlockSpec(memory_space=pltpu.SMEM),
            pl.BlockSpec((1, blk, GROUP * HEAD_DIM), lambda b, kh, qb: (b, qb, kh)),
            pl.BlockSpec((1, C, HEAD_DIM), lambda b, kh, qb: (b, 0, k_col + kh)),
            pl.BlockSpec((1, C, HEAD_DIM), lambda b, kh, qb: (b, 0, v_col + kh)),
            pl.BlockSpec(kv_blk, kwin(0)), pl.BlockSpec(kv_blk, kwin(1)), pl.BlockSpec(kv_blk, kwin(2)),
            pl.BlockSpec(kv_blk, vwin(0)), pl.BlockSpec(kv_blk, vwin(1)), pl.BlockSpec(kv_blk, vwin(2)),
        ],
        out_specs=pl.BlockSpec((1, blk, GROUP * HEAD_DIM), lambda b, kh, qb: (b, qb, kh)),
        out_shape=jax.ShapeDtypeStruct((B, S, N_HEADS * HEAD_DIM), BF16),
        compiler_params=_cparams(("arbitrary", "arbitrary", "arbitrary")),
        name="window_attention",
    )(sink, qkv, qkv, qkv, qkv, qkv, qkv, qkv, qkv, qkv)


def _layer_norm_rows(r, g, b):
    mu = jnp.mean(r, axis=-1, keepdims=True)
    rc = r - mu
    var = jnp.mean(rc * rc, axis=-1, keepdims=True)
    return rc * lax.rsqrt(var + LN_EPS) * g + b


def _mix_out_body(*refs, alpha, gla, n_ctx_tiles):
    if gla:
        of_ref, ob_ref, r_ref, ng_ref, x_ref, mod_ref, w_ref, lng_ref, lnb_ref, xo_ref, h_ref = refs
        x = x_ref[0]
        o = of_ref[0].astype(F32) + ob_ref[0].astype(F32)
        dv = ng_ref.shape[1]
        parts = []
        for hd in range(GLA_HEADS):
            oh = o[:, hd * dv:(hd + 1) * dv]
            ms = jnp.mean(oh * oh, axis=-1, keepdims=True)
            parts.append(oh * lax.rsqrt(ms + RMS_EPS) * ng_ref[...])
        a = (jnp.concatenate(parts, axis=1) * _silu(r_ref[0].astype(F32))).astype(BF16)
    else:
        a_ref, c_ref, x_ref, mod_ref, w_ref, lng_ref, lnb_ref, xo_ref, h_ref = refs
        a = a_ref[0]
        x = _stream_rows(c_ref, x_ref, n_ctx_tiles)
    m = mod_ref[0, 0]
    y = jnp.dot(a, w_ref[...], preferred_element_type=F32)
    xn = _layer_norm_rows(alpha * x + m[2:3] * y, lng_ref[...], lnb_ref[...])
    xo_ref[0] = xn
    h_ref[0] = xn * (1.0 + m[4:5]) + m[3:4]


def _attn_out(attn, ctx, x, modt, w_bf16, ln_g, ln_b, n_ctx_tiles, alpha):
    B, S, D = attn.shape
    tm = TOKEN_TILE
    seg = lambda j: jnp.where(j >= n_ctx_tiles, 1, 0)
    tok = pl.BlockSpec((1, tm, D), lambda b, j: (b, j, 0))
    vec = pl.BlockSpec((1, D), lambda b, j: (0, 0))
    return pl.pallas_call(
        functools.partial(_mix_out_body, alpha=alpha, gla=False, n_ctx_tiles=n_ctx_tiles),
        grid=(B, S // tm),
        in_specs=[tok] + _stream_specs(tm, D, n_ctx_tiles) + [
                  pl.BlockSpec((1, 1, 6, D), lambda b, j: (b, seg(j), 0, 0)),
                  pl.BlockSpec((D, D), lambda b, j: (0, 0)), vec, vec],
        out_specs=[tok, tok],
        out_shape=[jax.ShapeDtypeStruct((B, S, D), F32), jax.ShapeDtypeStruct((B, S, D), F32)],
        compiler_params=_cparams(("arbitrary", "arbitrary")),
        name="attn_out_ln",
    )(attn, ctx, x, modt, w_bf16, ln_g.reshape(1, D), ln_b.reshape(1, D))


def _gla_out(o2, proj, norm_g, xs, modt, w_bf16, ln_g, ln_b, n_ctx_tiles, T, alpha):
    B, S, D = xs.shape
    tm = TOKEN_TILE
    dv = D // GLA_HEADS
    r_col = proj.shape[2] // D - 1
    off = n_ctx_tiles
    tok_out = pl.BlockSpec((1, tm, D), lambda b, j: (b, j, 0))
    vec = pl.BlockSpec((1, D), lambda b, j: (0, 0))
    return pl.pallas_call(
        functools.partial(_mix_out_body, alpha=alpha, gla=True, n_ctx_tiles=n_ctx_tiles),
        grid=(B, T // tm),
        in_specs=[
            pl.BlockSpec((1, tm, D), lambda b, j: (b, j + off, 0)),
            pl.BlockSpec((1, tm, D), lambda b, j: (b, j + off, 0)),
            pl.BlockSpec((1, tm, D), lambda b, j: (b, j + off, r_col)),
            pl.BlockSpec((1, dv), lambda b, j: (0, 0)),
            pl.BlockSpec((1, tm, D), lambda b, j: (b, j + off, 0)),
            pl.BlockSpec((1, 1, 6, D), lambda b, j: (b, 1, 0, 0)),
            pl.BlockSpec((D, D), lambda b, j: (0, 0)), vec, vec,
        ],
        out_specs=[tok_out, tok_out],
        out_shape=[jax.ShapeDtypeStruct((B, T, D), F32), jax.ShapeDtypeStruct((B, T, D), F32)],
        compiler_params=_cparams(("arbitrary", "arbitrary")),
        name="gla_out_ln",
    )(o2[0], o2[1], proj, norm_g.reshape(1, dv), xs, modt, w_bf16, ln_g.reshape(1, D), ln_b.reshape(1, D))


def _router_body(h_ref, w_ref, b_ref, id_ref, wt_ref):
    logits = jnp.dot(h_ref[...].astype(BF16), w_ref[...], preferred_element_type=F32) + b_ref[...]
    lane = lax.broadcasted_iota(jnp.int32, logits.shape, 1)
    neg = -jnp.inf

    def first_max(vals):
        mx = jnp.max(vals, axis=1, keepdims=True)
        idx = jnp.min(jnp.where(vals == mx, lane, LANES), axis=1, keepdims=True)
        return mx, idx

    gmask = lane < MOE_GROUPS
    gl = jnp.where(gmask, logits, neg)
    gmax, gidx = first_max(gl)
    gsum = jnp.sum(jnp.where(gmask, jnp.exp(gl - gmax), 0.0), axis=1, keepdims=True)
    g_w = 1.0 / gsum
    lo = MOE_GROUPS + gidx * MOE_EXPERTS_PER_GROUP
    el = jnp.where((lane >= lo) & (lane < lo + MOE_EXPERTS_PER_GROUP), logits, neg)
    v1, i1 = first_max(el)
    v2, i2 = first_max(jnp.where(lane == i1, neg, el))
    e2 = jnp.exp(v2 - v1)
    w1 = g_w / (1.0 + e2)
    w2 = g_w * e2 / (1.0 + e2)
    id_ref[...] = jnp.where(lane == 0, i1 - MOE_GROUPS, jnp.where(lane == 1, i2 - MOE_GROUPS, 0))
    wt_ref[...] = jnp.where(lane == 0, w1, jnp.where(lane == 1, w2, 0.0))


def _router(h, wr_bf16, br):
    N, D = h.shape
    tm = ROUTER_TILE
    return pl.pallas_call(
        _router_body,
        grid=(N // tm,),
        in_specs=[pl.BlockSpec((tm, D), lambda i: (i, 0)),
                  pl.BlockSpec((D, LANES), lambda i: (0, 0)),
                  pl.BlockSpec((1, LANES), lambda i: (0, 0))],
        out_specs=[pl.BlockSpec((tm, LANES), lambda i: (i, 0)), pl.BlockSpec((tm, LANES), lambda i: (i, 0))],
        out_shape=[jax.ShapeDtypeStruct((N, LANES), jnp.int32), jax.ShapeDtypeStruct((N, LANES), F32)],
        compiler_params=_cparams(("arbitrary",)),
        name="moe_router",
    )(h, wr_bf16, br)


def _moe_body(te_ref, nv_ref, nu_ref, tok_ref, tokn_ref, dst_ref, rw_ref, h_hbm, wg_ref, wu_ref, wd_ref,
              y_hbm, xbuf, ybuf, wgb, wub, wdb, gsem, ssem):
    i = pl.program_id(0)
    n_used = nu_ref[0]
    slot = i % 2

    unroll = 8

    def gather(idx_ref, n_rows, s):
        def issue(r8, c):
            for u in range(unroll):
                r = r8 * unroll + u
                pltpu.make_async_copy(h_hbm.at[pl.ds(idx_ref[0, 0, r], 1), :],
                                      xbuf.at[s, pl.ds(r, 1), :], gsem.at[s]).start()
            return c
        lax.fori_loop(0, n_rows // unroll, issue, 0)

    def gather_wait(n_rows, s):
        n_rows = pl.multiple_of(n_rows, 8)
        pltpu.make_async_copy(h_hbm.at[pl.ds(0, n_rows), :], xbuf.at[s, pl.ds(0, n_rows), :], gsem.at[s]).wait()

    def scatter(n_rows, s):
        def issue(r8, c):
            for u in range(unroll):
                r = r8 * unroll + u
                pltpu.make_async_copy(ybuf.at[s, pl.ds(r, 1), :],
                                      y_hbm.at[pl.ds(dst_ref[0, 0, r], 1), :], ssem.at[s]).start()
            return c
        lax.fori_loop(0, n_rows // unroll, issue, 0)

    def scatter_wait(n_rows, s):
        n_rows = pl.multiple_of(n_rows, 8)
        pltpu.make_async_copy(ybuf.at[s, pl.ds(0, n_rows), :], y_hbm.at[pl.ds(0, n_rows), :], ssem.at[s]).wait()

    @pl.when(i == 0)
    def _():
        xbuf[...] = jnp.zeros(xbuf.shape, xbuf.dtype)
        tm = xbuf.shape[1]
        spare = y_hbm.shape[0] - 2 * tm
        for s in range(2):
            cp = pltpu.make_async_copy(xbuf.at[s], y_hbm.at[pl.ds(spare + s * tm, tm), :], ssem.at[s])
            cp.start()
            cp.wait()
        gather(tok_ref, nv_ref[0], 0)

    @pl.when(i + 1 < n_used)
    def _():
        gather(tokn_ref, nv_ref[i + 1], 1 - slot)

    @pl.when(i < n_used)
    def _():
        nv = nv_ref[i]
        gather_wait(nv, slot)

        @pl.when((i == 0) | (te_ref[i] != te_ref[jnp.maximum(i - 1, 0)]))
        def _():
            wgb[...] = wg_ref[0, 0].astype(BF16)
            wub[...] = wu_ref[0, 0].astype(BF16)
            wdb[...] = wd_ref[0, 0].astype(BF16)

        @pl.when(i >= 2)
        def _():
            scatter_wait(nv_ref[jnp.maximum(i - 2, 0)], slot)

        x = xbuf[slot].astype(BF16)
        g = jnp.dot(x, wgb[...], preferred_element_type=F32)
        u = jnp.dot(x, wub[...], preferred_element_type=F32)
        hid = (_silu(g) * u).astype(BF16)
        y = jnp.dot(hid, wdb[...], preferred_element_type=F32)
        ybuf[slot] = y * rw_ref[:, 0:1]
        scatter(nv, slot)

        @pl.when(i == n_used - 1)
        def _():
            @pl.when(i >= 1)
            def _():
                scatter_wait(nv_ref[jnp.maximum(i - 1, 0)], 1 - slot)
            scatter_wait(nv, slot)


def _moe_experts(h, tile_expert, tile_valid, n_used, row_tok, row_dst, row_w, w_gate, w_up, w_down, layer):
    N, D = h.shape
    _, E, _, Hd = w_gate.shape
    tm = MOE_TILE
    n_tiles = tile_expert.shape[0]
    idx_blk = lambda f: pl.BlockSpec((1, 1, tm), f, memory_space=pltpu.SMEM)
    grid_spec = pltpu.PrefetchScalarGridSpec(
        num_scalar_prefetch=3,
        grid=(n_tiles,),
        in_specs=[
            idx_blk(lambda i, te, nv, nu: (i, 0, 0)),
            idx_blk(lambda i, te, nv, nu: (jnp.minimum(i + 1, n_tiles - 1), 0, 0)),
            idx_blk(lambda i, te, nv, nu: (i, 0, 0)),
            pl.BlockSpec((tm, LANES), lambda i, te, nv, nu: (i, 0)),
            pl.BlockSpec(memory_space=pl.ANY),
            pl.BlockSpec((1, 1, D, Hd), lambda i, te, nv, nu: (layer, te[i], 0, 0)),
            pl.BlockSpec((1, 1, D, Hd), lambda i, te, nv, nu: (layer, te[i], 0, 0)),
            pl.BlockSpec((1, 1, Hd, D), lambda i, te, nv, nu: (layer, te[i], 0, 0)),
        ],
        out_specs=pl.BlockSpec(memory_space=pl.ANY),
        scratch_shapes=[
            pltpu.VMEM((2, tm, D), F32), pltpu.VMEM((2, tm, D), F32),
            pltpu.VMEM((D, Hd), BF16), pltpu.VMEM((D, Hd), BF16), pltpu.VMEM((Hd, D), BF16),
            pltpu.SemaphoreType.DMA((2,)), pltpu.SemaphoreType.DMA((2,)),
        ],
    )
    return pl.pallas_call(
        _moe_body,
        grid_spec=grid_spec,
        out_shape=jax.ShapeDtypeStruct((2 * N + 2 * tm, D), F32),
        compiler_params=_cparams(("arbitrary",)),
        name="moe_experts",
    )(tile_expert, tile_valid, n_used, row_tok, row_tok, row_dst, row_w, h, w_gate, w_up, w_down)


def _moe_plan(ids, wts, N):
    tm = MOE_TILE
    E = N_EXPERTS
    A = 2 * N
    n_tiles = A // tm + E
    e_flat = ids.reshape(A)
    counts = jnp.sum((e_flat[:, None] == jnp.arange(E, dtype=jnp.int32)[None, :]).astype(jnp.int32), axis=0)
    fill = (-counts) % tm
    fj = jnp.arange(tm, dtype=jnp.int32)[None, :]
    fe = jnp.arange(E, dtype=jnp.int32)[:, None]
    fkey = jnp.where(fj < fill[:, None], fe, E).reshape(E * tm)
    a_idx = jnp.arange(A, dtype=jnp.int32)
    n_fill = E * tm
    keys = jnp.concatenate([e_flat, fkey])
    tok = jnp.concatenate([a_idx // 2, jnp.zeros((n_fill,), jnp.int32)])
    dst = jnp.concatenate([(a_idx % 2) * N + a_idx // 2, jnp.full((n_fill,), -1, jnp.int32)])
    wgt = jnp.concatenate([wts.reshape(A), jnp.zeros((n_fill,), F32)])
    key_s, tok_s, dst_s, w_s = lax.sort((keys, tok, dst, wgt), num_keys=1, is_stable=True)
    n_used = jnp.sum(counts + fill) // tm
    t_idx = jnp.arange(n_tiles, dtype=jnp.int32)
    te = jnp.minimum(key_s.reshape(n_tiles, tm)[:, 0], E - 1)
    last_e = jnp.max(jnp.where(t_idx < n_used, te, 0))
    te = jnp.where(t_idx < n_used, te, last_e).astype(jnp.int32)
    dst2 = dst_s.reshape(n_tiles, tm)
    tile_valid = jnp.sum((dst2 >= 0).astype(jnp.int32), axis=1)
    tile_rows = jnp.where(t_idx < n_used, (tile_valid + 7) // 8 * 8, 0).astype(jnp.int32)
    spare = A + (t_idx[:, None] % 2) * tm + fj
    row_dst = jnp.where(dst2 >= 0, dst2, spare).astype(jnp.int32)
    row_w = jnp.broadcast_to(w_s[:, None], (n_tiles * tm, LANES))
    return (te, tile_rows, n_used.reshape(1).astype(jnp.int32),
            tok_s.reshape(n_tiles, 1, tm), row_dst.reshape(n_tiles, 1, tm), row_w)


def _moe_out_body(y0_ref, y1_ref, x_ref, mod_ref, lng_ref, lnb_ref, o_ref, *, alpha):
    m = mod_ref[0, 0]
    f = y0_ref[...] + y1_ref[...]
    o_ref[0] = _layer_norm_rows(alpha * x_ref[0] + m[5:6] * f, lng_ref[...], lnb_ref[...])


def _moe_out(y, xs, modt, ln_g, ln_b, n_ctx_tiles, alpha):
    B, R, D = xs.shape
    tm = TOKEN_TILE
    rt = R // tm
    if n_ctx_tiles < 0:
        seg = lambda j: 1
    else:
        seg = lambda j: jnp.where(j >= n_ctx_tiles, 1, 0)
    tok = pl.BlockSpec((1, tm, D), lambda b, j: (b, j, 0))
    vec = pl.BlockSpec((1, D), lambda b, j: (0, 0))
    return pl.pallas_call(
        functools.partial(_moe_out_body, alpha=alpha),
        grid=(B, R // tm),
        in_specs=[pl.BlockSpec((tm, D), lambda b, j: (b * rt + j, 0)),
                  pl.BlockSpec((tm, D), lambda b, j: (B * rt + b * rt + j, 0)),
                  tok, pl.BlockSpec((1, 1, 6, D), lambda b, j: (b, seg(j), 0, 0)), vec, vec],
        out_specs=tok,
        out_shape=jax.ShapeDtypeStruct((B, R, D), F32),
        compiler_params=_cparams(("arbitrary", "arbitrary")),
        name="moe_out_ln",
    )(y, y, xs, modt, ln_g.reshape(1, D), ln_b.reshape(1, D))


def _hier_moe(h, xs, modt, wg, bg, we, be, w_gate, w_up, w_down, layer, ln_g, ln_b, n_ctx_tiles, alpha):
    B, R, D = xs.shape
    N = B * R
    hf = h.reshape(N, D)
    pad = LANES - MOE_GROUPS - N_EXPERTS
    wr = jnp.concatenate([wg, we, jnp.zeros((D, pad), F32)], axis=1).astype(BF16)
    br = jnp.concatenate([bg, be, jnp.zeros((pad,), F32)]).reshape(1, LANES)
    ids, wts = _router(hf, wr, br)
    plan = _moe_plan(ids[:, :2], wts[:, :2], N)
    y = _moe_experts(hf, *plan, w_gate, w_up, w_down, layer)
    return _moe_out(y, xs, modt, ln_g, ln_b, n_ctx_tiles, alpha)


def _gla_proj_body(x_ref, mod_ref, w_ref, o_ref, *, q_cols, q_scale):
    m = mod_ref[0, 0]
    z = (x_ref[0] * (1.0 + m[1:2]) + m[0:1]).astype(BF16)
    acc = jnp.dot(z, w_ref[...], preferred_element_type=F32)
    half = pl.program_id(0)

    @pl.when(half == 0)
    def _():
        o_ref[0, :, :q_cols] = (acc[:, :q_cols] * q_scale).astype(BF16)
        o_ref[0, :, q_cols:] = acc[:, q_cols:].astype(BF16)

    @pl.when(half != 0)
    def _():
        o_ref[0] = acc.astype(BF16)


def _gla_proj(xs, modt, w_bf16, n_ctx_tiles, q_cols, q_scale):
    B, S, D = xs.shape
    NO = w_bf16.shape[1]
    tm = TOKEN_TILE
    seg = lambda j: jnp.where(j >= n_ctx_tiles, 1, 0)
    return pl.pallas_call(
        functools.partial(_gla_proj_body, q_cols=q_cols, q_scale=q_scale),
        grid=(2, B, S // tm),
        in_specs=[
            pl.BlockSpec((1, tm, D), lambda hf, b, j: (b, j, 0)),
            pl.BlockSpec((1, 1, 6, D), lambda hf, b, j: (b, seg(j), 0, 0)),
            pl.BlockSpec((D, NO // 2), lambda hf, b, j: (0, hf)),
        ],
        out_specs=pl.BlockSpec((1, tm, NO // 2), lambda hf, b, j: (b, j, hf)),
        out_shape=jax.ShapeDtypeStruct((B, S, NO), BF16),
        compiler_params=_cparams(("arbitrary", "arbitrary", "arbitrary")),
        name="gla_proj",
    )(xs, modt, w_bf16)


def _gla_gate_body(x_ref, mod_ref, w1_ref, w2_ref, b_ref, o_ref):
    m = mod_ref[0, 0]
    z = (x_ref[0] * (1.0 + m[1:2]) + m[0:1]).astype(BF16)
    t = jnp.dot(z, w1_ref[...], preferred_element_type=F32)
    pre = jnp.dot(t.astype(BF16), w2_ref[...], preferred_element_type=F32) + b_ref[...]
    o_ref[0] = (jnp.minimum(pre, 0.0) - jnp.log(1.0 + jnp.exp(-jnp.abs(pre)))) * (1.0 / GLA_TAU)


def _gla_gates(xs, modt, w1, w2, gb, n_ctx_tiles):
    B, S, D = xs.shape
    rank = w1.shape[2]
    kd = w2.shape[2]
    tm = TOKEN_TILE
    w1c = jnp.zeros((D, LANES), F32).at[:, :rank].set(w1[0]).at[:, rank:2 * rank].set(w1[1]).astype(BF16)
    w2c = (jnp.zeros((LANES, 2 * kd), F32).at[:rank, :kd].set(w2[0]).at[rank:2 * rank, kd:].set(w2[1])
           .astype(BF16))
    seg = lambda j: jnp.where(j >= n_ctx_tiles, 1, 0)
    return pl.pallas_call(
        _gla_gate_body,
        grid=(B, S // tm),
        in_specs=[
            pl.BlockSpec((1, tm, D), lambda b, j: (b, j, 0)),
            pl.BlockSpec((1, 1, 6, D), lambda b, j: (b, seg(j), 0, 0)),
            pl.BlockSpec((D, LANES), lambda b, j: (0, 0)),
            pl.BlockSpec((LANES, 2 * kd), lambda b, j: (0, 0)),
            pl.BlockSpec((1, 2 * kd), lambda b, j: (0, 0)),
        ],
        out_specs=pl.BlockSpec((1, tm, 2 * kd), lambda b, j: (b, j, 0)),
        out_shape=jax.ShapeDtypeStruct((B, S, 2 * kd), F32),
        compiler_params=_cparams(("arbitrary", "arbitrary")),
        name="gla_gates",
    )(xs, modt, w1c, w2c, gb.reshape(1, 2 * kd))


GLA_LEVELS = (32, 16, 8, 4, 2, 1)


def _gla_constants():
    L = GLA_CHUNK
    t = np.arange(L)[:, None]
    u = np.arange(L)[None, :]
    mats, masks = [], []
    for bwd in (False, True):
        blocks = []
        blocks.append((u >= t) if bwd else (u <= t))
        blocks.append((u < t) if bwd else (u > t))
        q_lv, k_lv, m_lv = [], [], []
        for m in GLA_LEVELS:
            base = (t // (2 * m)) * (2 * m)
            ubase = (u // (2 * m)) * (2 * m)
            t_hi = (t % (2 * m)) >= m
            if not bwd:
                r = base + m - 1
                q_lv.append(t_hi & (u > r) & (u <= t))
                k_lv.append(~t_hi & (u > t) & (u <= r))
                m_lv.append(t_hi & ((u % (2 * m)) < m) & (base == ubase))
            else:
                r = base + m
                q_lv.append(~t_hi & (u >= t) & (u < r))
                k_lv.append(t_hi & (u >= r) & (u < t))
                m_lv.append(~t_hi & ((u % (2 * m)) >= m) & (base == ubase))
        blocks += [ql | kl for ql, kl in zip(q_lv, k_lv)]
        blocks.append(np.ones((8, L), bool))
        a = np.concatenate(blocks, axis=0).astype(np.float32)
        mats.append(np.concatenate([a, a, a], axis=1))
        m_lv.append(t == u)
        masks.append(np.concatenate(m_lv, axis=0).astype(np.float32))
    return np.stack(mats), np.stack(masks)


LOG2_E = 1.4426950408889634


def _gla_chunk_step(q_ref, k_ref, v_ref, g_ref, a_ref, mk_ref, o_ref, st_ref):
    L = GLA_CHUNK
    nlv = len(GLA_LEVELS)
    q = q_ref[0]
    k = k_ref[0]
    v = v_ref[0]
    g = g_ref[0] * LOG2_E
    g_hi = g.astype(BF16)
    r1 = g - g_hi.astype(F32)
    g_mid = r1.astype(BF16)
    g_lo = (r1 - g_mid.astype(F32)).astype(BF16)
    gs = jnp.concatenate([g_hi, g_mid, g_lo], axis=0)
    ex = jnp.exp2(jnp.dot(a_ref[...], gs, preferred_element_type=F32))
    qf = q.astype(F32)
    kf = k.astype(F32)
    q_inter = (qf * ex[0:L]).astype(BF16)
    k_state = (kf * ex[L:2 * L]).astype(BF16)
    dec = ex[(2 + nlv) * L:(2 + nlv) * L + 1]
    nt = (((1,), (1,)), ((), ()))
    att = lax.dot_general(q, k, nt, preferred_element_type=F32) * mk_ref[nlv * L:(nlv + 1) * L]
    for lv in range(nlv):
        f = ex[(2 + lv) * L:(3 + lv) * L]
        ql = (qf * f).astype(BF16)
        kl = (kf * f).astype(BF16)
        att = att + lax.dot_general(ql, kl, nt, preferred_element_type=F32) * mk_ref[lv * L:(lv + 1) * L]
    st = st_ref[...]
    o = jnp.dot(att.astype(BF16), v, preferred_element_type=F32)
    o = o + lax.dot_general(q_inter, st.astype(BF16), nt, preferred_element_type=F32)
    o_ref[0] = o.astype(o_ref.dtype)
    upd = lax.dot_general(v, k_state, (((0,), (0,)), ((), ())), preferred_element_type=F32)
    st_ref[...] = st * dec + upd


def _gla_scan_body(qf_ref, kf_ref, vf_ref, gf_ref, qb_ref, kb_ref, vb_ref, gb_ref,
                   af_ref, ab_ref, mf_ref, mb_ref, of_ref, ob_ref, stf_ref, stb_ref):
    @pl.when(pl.program_id(2) == 0)
    def _():
        stf_ref[...] = jnp.zeros(stf_ref.shape, stf_ref.dtype)
        stb_ref[...] = jnp.zeros(stb_ref.shape, stb_ref.dtype)

    _gla_chunk_step(qf_ref, kf_ref, vf_ref, gf_ref, af_ref, mf_ref, of_ref, stf_ref)
    _gla_chunk_step(qb_ref, kb_ref, vb_ref, gb_ref, ab_ref, mb_ref, ob_ref, stb_ref)


def _gla_scan(proj, lg, C, T):
    B, S, NO = proj.shape
    kd = lg.shape[2] // 2
    D = NO - 2 * kd
    D = D // 2
    dk = kd // GLA_HEADS
    dv = D // GLA_HEADS
    L = GLA_CHUNK
    n_ctx = C // L
    n_all = S // L
    amat, masks = _gla_constants()
    amat = jnp.asarray(amat, BF16)
    masks = jnp.asarray(masks, F32)

    def fwd(c):
        return c

    def bwd(c):
        return jnp.where(c < n_ctx, n_ctx - 1 - c, n_ctx + n_all - 1 - c)

    k_col = kd // dk
    v_col = 2 * kd // dv
    g_cols = kd // dk

    def specs(chunk, d):
        return [
            pl.BlockSpec((1, L, dk), lambda b, h, c: (b, chunk(c), h)),
            pl.BlockSpec((1, L, dk), lambda b, h, c: (b, chunk(c), k_col + h)),
            pl.BlockSpec((1, L, dv), lambda b, h, c: (b, chunk(c), v_col + h)),
            pl.BlockSpec((1, L, dk), lambda b, h, c: (b, chunk(c), d * g_cols + h)),
        ]

    const = lambda arr: pl.BlockSpec(arr.shape[1:], lambda b, h, c: (0, 0))
    out = jax.ShapeDtypeStruct((B, S, D), BF16)
    return pl.pallas_call(
        _gla_scan_body,
        grid=(B, GLA_HEADS, n_all),
        in_specs=specs(fwd, 0) + specs(bwd, 1) + [const(amat), const(amat), const(masks), const(masks)],
        out_specs=[pl.BlockSpec((1, L, dv), lambda b, h, c: (b, fwd(c), h)),
                   pl.BlockSpec((1, L, dv), lambda b, h, c: (b, bwd(c), h))],
        out_shape=[out, out],
        scratch_shapes=[pltpu.VMEM((dv, dk), F32), pltpu.VMEM((dv, dk), F32)],
        compiler_params=_cparams(("arbitrary", "arbitrary", "arbitrary")),
        name="gla_scan",
    )(proj, proj, proj, lg, proj, proj, proj, lg, amat[0], amat[1], masks[0], masks[1])


def kernel(x, c, ctx, c_ctx, mod_w, mod_b, ln_g, ln_b, attn_w_qkv, attn_w_o, attn_sink,
           gla_w_in, gla_gate_w1, gla_gate_w2, gla_gate_b, gla_norm_g, gla_w_o,
           moe_group_w, moe_group_b, moe_router_w, moe_router_b, moe_w_gate, moe_w_up, moe_w_down):
    B, T, D = x.shape
    C = ctx.shape[1]
    depth = mod_w.shape[0]
    assert depth == 2 and D == N_HEADS * HEAD_DIM
    assert C % TOKEN_TILE == 0 and T % TOKEN_TILE == 0 and T % GRID_W == 0
    alpha = (2 * depth) ** 0.25
    n_ctx_tiles = C // TOKEN_TILE

    m_rows = -(-(B + 1) // 8) * 8
    cond = jnp.zeros((m_rows, D), F32).at[:B].set(c).at[B].set(c_ctx)
    mods = _adaln_mods(cond, mod_w, mod_b)

    def mod_table(l):
        lat = mods[l, :B].reshape(B, 1, 6, D)
        cm = jnp.broadcast_to(mods[l, B].reshape(1, 1, 6, D), (B, 1, 6, D))
        return jnp.concatenate([cm, lat], axis=1)

    modt = mod_table(0)
    cos, sin = _rope_tables(C, T)
    qkv = _qkv_proj(ctx, x, modt, attn_w_qkv[0].astype(BF16), cos, sin, n_ctx_tiles)
    attn = _attention(qkv, attn_sink[0], C, T)
    xs, h = _attn_out(attn, ctx, x, modt, attn_w_o[0].astype(BF16), ln_g[0, 0], ln_b[0, 0], n_ctx_tiles,
                      alpha)
    xs = _hier_moe(h, xs, modt, moe_group_w[0], moe_group_b[0], moe_router_w[0], moe_router_b[0],
                   moe_w_gate, moe_w_up, moe_w_down, 0, ln_g[0, 1], ln_b[0, 1], n_ctx_tiles, alpha)

    modt = mod_table(1)
    kd = gla_gate_w2.shape[3]
    dk = kd // GLA_HEADS
    proj = _gla_proj(xs, modt, gla_w_in[0].astype(BF16), n_ctx_tiles, kd, dk ** -0.5)
    lg = _gla_gates(xs, modt, gla_gate_w1[0], gla_gate_w2[0], gla_gate_b[0], n_ctx_tiles)
    o2 = _gla_scan(proj, lg, C, T)
    xl, h = _gla_out(o2, proj, gla_norm_g[0], xs, modt, gla_w_o[0].astype(BF16), ln_g[1, 0], ln_b[1, 0],
                     n_ctx_tiles, T, alpha)
    return _hier_moe(h, xl, modt, moe_group_w[1], moe_group_b[1], moe_router_w[1], moe_router_b[1],
                     moe_w_gate, moe_w_up, moe_w_down, 1, ln_g[1, 1], ln_b[1, 1], -1, alpha)
```

```python
import functools

import numpy as np
import jax
import jax.numpy as jnp
from jax import lax
from jax.experimental import pallas as pl
from jax.experimental.pallas import tpu as pltpu

F32 = jnp.float32
BF16 = jnp.bfloat16

N_HEADS = 16
N_KV_HEADS = 4
HEAD_DIM = 128
GROUP = N_HEADS // N_KV_HEADS
WINDOW = 128
GRID_W = 64
ROPE_BASE = 10000.0
GLA_HEADS = 4
GLA_TAU = 16.0
GLA_CHUNK = 64
MOE_GROUPS = 4
MOE_EXPERTS_PER_GROUP = 8
N_EXPERTS = MOE_GROUPS * MOE_EXPERTS_PER_GROUP
LN_EPS = 1e-5
RMS_EPS = 1e-6

V7X_VMEM_LIMIT_BYTES = 56 * 1024 * 1024
LANES = 128
TOKEN_TILE = 256
MOE_TILE = 256
MODS_COL_TILE = 1024


def _cparams(sem):
    return pltpu.CompilerParams(dimension_semantics=sem, vmem_limit_bytes=V7X_VMEM_LIMIT_BYTES)


def _silu(v):
    return v / (1.0 + jnp.exp(-v))


def _mods_body(cond_ref, w_ref, b_ref, o_ref):
    a = _silu(cond_ref[...]).astype(BF16)
    o_ref[0] = jnp.dot(a, w_ref[0].astype(BF16), preferred_element_type=F32) + b_ref[0]


def _adaln_mods(cond, mod_w, mod_b):
    L, D, N6 = mod_w.shape
    M = cond.shape[0]
    tn = MODS_COL_TILE
    return pl.pallas_call(
        _mods_body,
        grid=(L, N6 // tn),
        in_specs=[
            pl.BlockSpec((M, D), lambda l, j: (0, 0)),
            pl.BlockSpec((1, D, tn), lambda l, j: (l, 0, j)),
            pl.BlockSpec((1, 1, tn), lambda l, j: (l, 0, j)),
        ],
        out_specs=pl.BlockSpec((1, M, tn), lambda l, j: (l, 0, j)),
        out_shape=jax.ShapeDtypeStruct((L, M, N6), F32),
        compiler_params=_cparams(("arbitrary", "arbitrary")),
        name="adaln_mods",
    )(cond, mod_w, mod_b.reshape(L, 1, N6))


def _stream_rows(c_ref, x_ref, n_ctx_tiles):
    return jnp.where(pl.program_id(1) < n_ctx_tiles, c_ref[0], x_ref[0])


def _stream_specs(tm, D, n_ctx_tiles):
    return [pl.BlockSpec((1, tm, D), lambda b, j: (b, jnp.minimum(j, n_ctx_tiles - 1), 0)),
            pl.BlockSpec((1, tm, D), lambda b, j: (b, jnp.maximum(j - n_ctx_tiles, 0), 0))]


def _qkv_body(c_ref, x_ref, mod_ref, w_ref, cos_ref, sin_ref, o_ref, *, n_ctx_tiles, n_rot_heads, n_q_heads,
              scale):
    m = mod_ref[0, 0]
    h = (_stream_rows(c_ref, x_ref, n_ctx_tiles) * (1.0 + m[1:2]) + m[0:1]).astype(BF16)
    acc = jnp.dot(h, w_ref[...], preferred_element_type=F32)
    cos = cos_ref[...]
    sin = sin_ref[...]
    lane = lax.broadcasted_iota(jnp.int32, cos.shape, 1)
    first = (lane & 32) == 0
    for hd in range(n_rot_heads):
        y = acc[:, hd * HEAD_DIM:(hd + 1) * HEAD_DIM]
        partner = jnp.where(first, pltpu.roll(y, HEAD_DIM - 32, 1), pltpu.roll(y, 32, 1))
        r = y * cos + partner * sin
        if hd < n_q_heads:
            r = r * scale
        o_ref[0, :, hd * HEAD_DIM:(hd + 1) * HEAD_DIM] = r.astype(BF16)
    rest = n_rot_heads * HEAD_DIM
    o_ref[0, :, rest:] = acc[:, rest:].astype(BF16)


def _rope_tables(C, T):
    half = HEAD_DIM // 2
    pos = np.arange(T)
    inv_freq = ROPE_BASE ** (-np.arange(0, half, 2, dtype=np.float32) / half)
    ang_r = (pos // GRID_W).astype(np.float32)[:, None] * inv_freq
    ang_c = (pos % GRID_W).astype(np.float32)[:, None] * inv_freq
    ang_r = jnp.asarray(ang_r, F32)
    ang_c = jnp.asarray(ang_c, F32)
    cos = jnp.concatenate([jnp.cos(ang_r)] * 2 + [jnp.cos(ang_c)] * 2, axis=-1)
    sin = jnp.concatenate([-jnp.sin(ang_r), jnp.sin(ang_r), -jnp.sin(ang_c), jnp.sin(ang_c)], axis=-1)
    cos = jnp.concatenate([jnp.ones((C, HEAD_DIM), F32), cos], axis=0)
    sin = jnp.concatenate([jnp.zeros((C, HEAD_DIM), F32), sin], axis=0)
    return cos, sin


def _qkv_proj(ctx, x, modt, w_bf16, cos, sin, n_ctx_tiles):
    B, T, D = x.shape
    S = ctx.shape[1] + T
    NO = w_bf16.shape[1]
    tm = TOKEN_TILE
    body = functools.partial(_qkv_body, n_ctx_tiles=n_ctx_tiles, n_rot_heads=N_HEADS + N_KV_HEADS,
                             n_q_heads=N_HEADS, scale=HEAD_DIM ** -0.5)
    seg = lambda j: jnp.where(j >= n_ctx_tiles, 1, 0)
    return pl.pallas_call(
        body,
        grid=(B, S // tm),
        in_specs=_stream_specs(tm, D, n_ctx_tiles) + [
            pl.BlockSpec((1, 1, 6, D), lambda b, j: (b, seg(j), 0, 0)),
            pl.BlockSpec((D, NO), lambda b, j: (0, 0)),
            pl.BlockSpec((tm, HEAD_DIM), lambda b, j: (j, 0)),
            pl.BlockSpec((tm, HEAD_DIM), lambda b, j: (j, 0)),
        ],
        out_specs=pl.BlockSpec((1, tm, NO), lambda b, j: (b, j, 0)),
        out_shape=jax.ShapeDtypeStruct((B, S, NO), BF16),
        compiler_params=_cparams(("arbitrary", "arbitrary")),
        name="qkv_rope",
    )(ctx, x, modt, w_bf16, cos, sin)


ATTN_KV_PER_STEP = 2


def _attn_one_head(sink_ref, head, q, kc, vc, k_win, v_win, n, n_lat_blk, latent):
    blk = WINDOW
    rows = GROUP * blk
    row = lax.broadcasted_iota(jnp.int32, (rows, 1), 0)
    sink = jnp.full((rows, 1), sink_ref[head * GROUP + GROUP - 1], F32)
    for g in range(GROUP - 2, -1, -1):
        sink = jnp.where(row < (g + 1) * blk, sink_ref[head * GROUP + g], sink)
    nt = (((1,), (1,)), ((), ()))
    s_c = lax.dot_general(q, kc, nt, preferred_element_type=F32)
    scores = [(s_c, vc)]
    if latent:
        a = lax.broadcasted_iota(jnp.int32, (rows, blk), 0) % blk
        key = lax.broadcasted_iota(jnp.int32, (rows, blk), 1)
        s0 = lax.dot_general(q, k_win[0], nt, preferred_element_type=F32)
        s0 = jnp.where((key >= a) & (n > 0), s0, -jnp.inf)
        s1 = lax.dot_general(q, k_win[1], nt, preferred_element_type=F32)
        s2 = lax.dot_general(q, k_win[2], nt, preferred_element_type=F32)
        s2 = jnp.where((key <= a) & (n < n_lat_blk - 1), s2, -jnp.inf)
        scores += [(s0, v_win[0]), (s1, v_win[1]), (s2, v_win[2])]
    lane_tiles = [s[:, j * LANES:(j + 1) * LANES] for s, _ in scores for j in range(s.shape[1] // LANES)]
    m_el = lane_tiles[0]
    for t in lane_tiles[1:]:
        m_el = jnp.maximum(m_el, t)
    m = jnp.maximum(jnp.max(m_el, axis=1, keepdims=True), sink)
    acc = jnp.zeros((rows, HEAD_DIM), F32)
    l_el = jnp.zeros((rows, LANES), F32)
    for s, v in scores:
        p = jnp.exp(s - m)
        for j in range(s.shape[1] // LANES):
            l_el = l_el + p[:, j * LANES:(j + 1) * LANES]
        acc = acc + jnp.dot(p.astype(BF16), v, preferred_element_type=F32)
    denom = jnp.exp(sink - m) + jnp.sum(l_el, axis=1, keepdims=True)
    return acc / denom


def _attn_body(sink_ref, q_ref, kc_ref, vc_ref, k0_ref, k1_ref, k2_ref, v0_ref, v1_ref, v2_ref, o_ref,
               *, n_ctx_blk, n_lat_blk):
    kh0 = pl.program_id(1) * ATTN_KV_PER_STEP
    qb = pl.program_id(2)
    blk = WINDOW
    gw = GROUP * HEAD_DIM

    def run(latent):
        n = qb - n_ctx_blk
        for i in range(ATTN_KV_PER_STEP):
            hs = slice(i * HEAD_DIM, (i + 1) * HEAD_DIM)
            q = jnp.concatenate([q_ref[0, :, i * gw + g * HEAD_DIM:i * gw + (g + 1) * HEAD_DIM]
                                 for g in range(GROUP)], axis=0)
            k_win = [r[0, :, hs] for r in (k0_ref, k1_ref, k2_ref)] if latent else None
            v_win = [r[0, :, hs] for r in (v0_ref, v1_ref, v2_ref)] if latent else None
            o = _attn_one_head(sink_ref, kh0 + i, q, kc_ref[0, :, hs], vc_ref[0, :, hs], k_win, v_win,
                               n, n_lat_blk, latent)
            for g in range(GROUP):
                o_ref[0, :, i * gw + g * HEAD_DIM:i * gw + (g + 1) * HEAD_DIM] = (
                    o[g * blk:(g + 1) * blk].astype(o_ref.dtype))

    @pl.when(qb < n_ctx_blk)
    def _():
        run(False)

    @pl.when(qb >= n_ctx_blk)
    def _():
        run(True)


def _attention(qkv, sink, C, T):
    B, S, _ = qkv.shape
    blk = WINDOW
    n_ctx_blk = C // blk
    n_lat_blk = T // blk
    kvs = ATTN_KV_PER_STEP
    kvw = kvs * HEAD_DIM
    k_col = N_HEADS // kvs
    v_col = (N_HEADS + N_KV_HEADS) // kvs

    def win(j, col):
        def im(b, kh, qb):
            n = jnp.maximum(qb - n_ctx_blk, 0)
            return (b, n_ctx_blk + jnp.clip(n + j - 1, 0, n_lat_blk - 1), col + kh)
        return im

    kv_blk = (1, blk, kvw)
    q_blk = (1, blk, kvs * GROUP * HEAD_DIM)
    body = functools.partial(_attn_body, n_ctx_blk=n_ctx_blk, n_lat_blk=n_lat_blk)
    return pl.pallas_call(
        body,
        grid=(B, N_KV_HEADS // kvs, S // blk),
        in_specs=[
            pl.BlockSpec(memory_space=pltpu.SMEM),
            pl.BlockSpec(q_blk, lambda b, kh, qb: (b, qb, kh)),
            pl.BlockSpec((1, C, kvw), lambda b, kh, qb: (b, 0, k_col + kh)),
            pl.BlockSpec((1, C, kvw), lambda b, kh, qb: (b, 0, v_col + kh)),
            pl.BlockSpec(kv_blk, win(0, k_col)), pl.BlockSpec(kv_blk, win(1, k_col)),
            pl.BlockSpec(kv_blk, win(2, k_col)),
            pl.BlockSpec(kv_blk, win(0, v_col)), pl.BlockSpec(kv_blk, win(1, v_col)),
            pl.BlockSpec(kv_blk, win(2, v_col)),
        ],
        out_specs=pl.BlockSpec(q_blk, lambda b, kh, qb: (b, qb, kh)),
        out_shape=jax.ShapeDtypeStruct((B, S, N_HEADS * HEAD_DIM), BF16),
        compiler_params=_cparams(("arbitrary", "arbitrary", "arbitrary")),
        name="window_attention",
    )(sink, qkv, qkv, qkv, qkv, qkv, qkv, qkv, qkv, qkv)


def _layer_norm_rows(r, g, b):
    mu = jnp.mean(r, axis=-1, keepdims=True)
    rc = r - mu
    var = jnp.mean(rc * rc, axis=-1, keepdims=True)
    return rc * lax.rsqrt(var + LN_EPS) * g + b


def _route(logits):
    lane = lax.broadcasted_iota(jnp.int32, logits.shape, 1)
    neg = -jnp.inf

    def first_max(vals):
        mx = jnp.max(vals, axis=1, keepdims=True)
        idx = jnp.min(jnp.where(vals == mx, lane, LANES), axis=1, keepdims=True)
        return mx, idx

    gmask = lane < MOE_GROUPS
    gl = jnp.where(gmask, logits, neg)
    gmax, gidx = first_max(gl)
    gsum = jnp.sum(jnp.where(gmask, jnp.exp(gl - gmax), 0.0), axis=1, keepdims=True)
    g_w = 1.0 / gsum
    lo = MOE_GROUPS + gidx * MOE_EXPERTS_PER_GROUP
    el = jnp.where((lane >= lo) & (lane < lo + MOE_EXPERTS_PER_GROUP), logits, neg)
    v1, i1 = first_max(el)
    v2, i2 = first_max(jnp.where(lane == i1, neg, el))
    e2 = jnp.exp(v2 - v1)
    w1 = g_w / (1.0 + e2)
    w2 = g_w * e2 / (1.0 + e2)
    ids = jnp.where(lane == 0, i1 - MOE_GROUPS, jnp.where(lane == 1, i2 - MOE_GROUPS, 0))
    wts = jnp.where(lane == 0, w1, jnp.where(lane == 1, w2, 0.0))
    return ids, wts


def _mix_out_body(*refs, alpha, gla, n_ctx_tiles):
    wr_ref, br_ref, xo_ref, h_ref, id_ref, wt_ref = refs[-6:]
    refs = refs[:-6]
    if gla:
        of_ref, ob_ref, r_ref, ng_ref, x_ref, mod_ref, w_ref, lng_ref, lnb_ref = refs
        x = x_ref[0]
        o = of_ref[0].astype(F32) + ob_ref[0].astype(F32)
        dv = ng_ref.shape[1]
        parts = []
        for hd in range(GLA_HEADS):
            oh = o[:, hd * dv:(hd + 1) * dv]
            ms = jnp.mean(oh * oh, axis=-1, keepdims=True)
            parts.append(oh * lax.rsqrt(ms + RMS_EPS) * ng_ref[...])
        a = (jnp.concatenate(parts, axis=1) * _silu(r_ref[0].astype(F32))).astype(BF16)
    else:
        a_ref, c_ref, x_ref, mod_ref, w_ref, lng_ref, lnb_ref = refs
        a = a_ref[0]
        x = _stream_rows(c_ref, x_ref, n_ctx_tiles)
    m = mod_ref[0, 0]
    y = jnp.dot(a, w_ref[...], preferred_element_type=F32)
    xn = _layer_norm_rows(alpha * x + m[2:3] * y, lng_ref[...], lnb_ref[...])
    xo_ref[0] = xn
    h = xn * (1.0 + m[4:5]) + m[3:4]
    h_ref[0] = h
    ids, wts = _route(jnp.dot(h.astype(BF16), wr_ref[...], preferred_element_type=F32) + br_ref[...])
    id_ref[0] = ids
    wt_ref[0] = wts


def _router_params(wg, bg, we, be):
    D = wg.shape[0]
    pad = LANES - MOE_GROUPS - N_EXPERTS
    wr = jnp.concatenate([wg, we, jnp.zeros((D, pad), F32)], axis=1).astype(BF16)
    br = jnp.concatenate([bg, be, jnp.zeros((pad,), F32)]).reshape(1, LANES)
    return wr, br


def _mix_out_tail(B, R, D, tm):
    tok = pl.BlockSpec((1, tm, D), lambda b, j: (b, j, 0))
    lane_tile = pl.BlockSpec((1, tm, LANES), lambda b, j: (b, j, 0))
    in_specs = [pl.BlockSpec((D, LANES), lambda b, j: (0, 0)), pl.BlockSpec((1, LANES), lambda b, j: (0, 0))]
    out_specs = [tok, tok, lane_tile, lane_tile]
    out_shape = [jax.ShapeDtypeStruct((B, R, D), F32), jax.ShapeDtypeStruct((B, R, D), F32),
                 jax.ShapeDtypeStruct((B, R, LANES), jnp.int32), jax.ShapeDtypeStruct((B, R, LANES), F32)]
    return in_specs, out_specs, out_shape


def _attn_out(attn, ctx, x, modt, w_bf16, ln_g, ln_b, router, n_ctx_tiles, alpha):
    B, S, D = attn.shape
    tm = TOKEN_TILE
    seg = lambda j: jnp.where(j >= n_ctx_tiles, 1, 0)
    tok = pl.BlockSpec((1, tm, D), lambda b, j: (b, j, 0))
    vec = pl.BlockSpec((1, D), lambda b, j: (0, 0))
    r_in, out_specs, out_shape = _mix_out_tail(B, S, D, tm)
    return pl.pallas_call(
        functools.partial(_mix_out_body, alpha=alpha, gla=False, n_ctx_tiles=n_ctx_tiles),
        grid=(B, S // tm),
        in_specs=[tok] + _stream_specs(tm, D, n_ctx_tiles) + [
                  pl.BlockSpec((1, 1, 6, D), lambda b, j: (b, seg(j), 0, 0)),
                  pl.BlockSpec((D, D), lambda b, j: (0, 0)), vec, vec] + r_in,
        out_specs=out_specs,
        out_shape=out_shape,
        compiler_params=_cparams(("arbitrary", "arbitrary")),
        name="attn_out_ln",
    )(attn, ctx, x, modt, w_bf16, ln_g.reshape(1, D), ln_b.reshape(1, D), *router)


def _gla_out(o2, proj, norm_g, xs, modt, w_bf16, ln_g, ln_b, router, n_ctx_tiles, T, alpha):
    B, S, D = xs.shape
    tm = TOKEN_TILE
    dv = D // GLA_HEADS
    r_col = proj.shape[2] // D - 1
    off = n_ctx_tiles
    vec = pl.BlockSpec((1, D), lambda b, j: (0, 0))
    r_in, out_specs, out_shape = _mix_out_tail(B, T, D, tm)
    return pl.pallas_call(
        functools.partial(_mix_out_body, alpha=alpha, gla=True, n_ctx_tiles=n_ctx_tiles),
        grid=(B, T // tm),
        in_specs=[
            pl.BlockSpec((1, tm, D), lambda b, j: (b, j + off, 0)),
            pl.BlockSpec((1, tm, D), lambda b, j: (b, j + off, 0)),
            pl.BlockSpec((1, tm, D), lambda b, j: (b, j + off, r_col)),
            pl.BlockSpec((1, dv), lambda b, j: (0, 0)),
            pl.BlockSpec((1, tm, D), lambda b, j: (b, j + off, 0)),
            pl.BlockSpec((1, 1, 6, D), lambda b, j: (b, 1, 0, 0)),
            pl.BlockSpec((D, D), lambda b, j: (0, 0)), vec, vec,
        ] + r_in,
        out_specs=out_specs,
        out_shape=out_shape,
        compiler_params=_cparams(("arbitrary", "arbitrary")),
        name="gla_out_ln",
    )(o2[0], o2[1], proj, norm_g.reshape(1, dv), xs, modt, w_bf16, ln_g.reshape(1, D), ln_b.reshape(1, D),
      *router)


def _moe_body(te_ref, nv_ref, nu_ref, tok_ref, tokn_ref, dst_ref, rw_ref, h_hbm, wg_ref, wu_ref, wd_ref,
              y_hbm, xbuf, ybuf, wgb, wub, wdb, gsem, ssem):
    i = pl.program_id(0)
    n_used = nu_ref[0]
    slot = i % 2

    unroll = 8

    def gather(idx_ref, n_rows, s):
        def issue(r8, c):
            for u in range(unroll):
                r = r8 * unroll + u
                pltpu.make_async_copy(h_hbm.at[pl.ds(idx_ref[0, 0, r], 1), :],
                                      xbuf.at[s, pl.ds(r, 1), :], gsem.at[s]).start(priority=u % 2)
            return c
        lax.fori_loop(0, n_rows // unroll, issue, 0)

    def gather_wait(n_rows, s):
        n_rows = pl.multiple_of(n_rows, 8)
        pltpu.make_async_copy(h_hbm.at[pl.ds(0, n_rows), :], xbuf.at[s, pl.ds(0, n_rows), :], gsem.at[s]).wait()

    def scatter(n_rows, s):
        def issue(r8, c):
            for u in range(unroll):
                r = r8 * unroll + u
                pltpu.make_async_copy(ybuf.at[s, pl.ds(r, 1), :],
                                      y_hbm.at[pl.ds(dst_ref[0, 0, r], 1), :], ssem.at[s]).start(priority=u % 2)
            return c
        lax.fori_loop(0, n_rows // unroll, issue, 0)

    def scatter_wait(n_rows, s):
        n_rows = pl.multiple_of(n_rows, 8)
        pltpu.make_async_copy(ybuf.at[s, pl.ds(0, n_rows), :], y_hbm.at[pl.ds(0, n_rows), :], ssem.at[s]).wait()

    @pl.when(i == 0)
    def _():
        xbuf[...] = jnp.zeros(xbuf.shape, xbuf.dtype)
        tm = xbuf.shape[1]
        spare = y_hbm.shape[0] - 2 * tm
        for s in range(2):
            cp = pltpu.make_async_copy(xbuf.at[s], y_hbm.at[pl.ds(spare + s * tm, tm), :], ssem.at[s])
            cp.start()
            cp.wait()
        gather(tok_ref, nv_ref[0], 0)

    @pl.when(i + 1 < n_used)
    def _():
        gather(tokn_ref, nv_ref[i + 1], 1 - slot)

    @pl.when(i < n_used)
    def _():
        nv = nv_ref[i]
        gather_wait(nv, slot)

        @pl.when((i == 0) | (te_ref[i] != te_ref[jnp.maximum(i - 1, 0)]))
        def _():
            wgb[...] = wg_ref[0, 0].astype(BF16)
            wub[...] = wu_ref[0, 0].astype(BF16)
            wdb[...] = wd_ref[0, 0].astype(BF16)

        @pl.when(i >= 2)
        def _():
            scatter_wait(nv_ref[jnp.maximum(i - 2, 0)], slot)

        x = xbuf[slot].astype(BF16)
        g = jnp.dot(x, wgb[...], preferred_element_type=F32)
        u = jnp.dot(x, wub[...], preferred_element_type=F32)
        hid = (_silu(g) * u).astype(BF16)
        y = jnp.dot(hid, wdb[...], preferred_element_type=F32)
        ybuf[slot] = y * rw_ref[:, 0:1]
        scatter(nv, slot)

        @pl.when(i == n_used - 1)
        def _():
            @pl.when(i >= 1)
            def _():
                scatter_wait(nv_ref[jnp.maximum(i - 1, 0)], 1 - slot)
            scatter_wait(nv, slot)


def _moe_experts(h, tile_expert, tile_valid, n_used, row_tok, row_dst, row_w, w_gate, w_up, w_down, layer):
    N, D = h.shape
    _, E, _, Hd = w_gate.shape
    tm = MOE_TILE
    n_tiles = tile_expert.shape[0]
    idx_blk = lambda f: pl.BlockSpec((1, 1, tm), f, memory_space=pltpu.SMEM)
    grid_spec = pltpu.PrefetchScalarGridSpec(
        num_scalar_prefetch=3,
        grid=(n_tiles,),
        in_specs=[
            idx_blk(lambda i, te, nv, nu: (i, 0, 0)),
            idx_blk(lambda i, te, nv, nu: (jnp.minimum(i + 1, n_tiles - 1), 0, 0)),
            idx_blk(lambda i, te, nv, nu: (i, 0, 0)),
            pl.BlockSpec((tm, LANES), lambda i, te, nv, nu: (i, 0)),
            pl.BlockSpec(memory_space=pl.ANY),
            pl.BlockSpec((1, 1, D, Hd), lambda i, te, nv, nu: (layer, te[i], 0, 0)),
            pl.BlockSpec((1, 1, D, Hd), lambda i, te, nv, nu: (layer, te[i], 0, 0)),
            pl.BlockSpec((1, 1, Hd, D), lambda i, te, nv, nu: (layer, te[i], 0, 0)),
        ],
        out_specs=pl.BlockSpec(memory_space=pl.ANY),
        scratch_shapes=[
            pltpu.VMEM((2, tm, D), F32), pltpu.VMEM((2, tm, D), F32),
            pltpu.VMEM((D, Hd), BF16), pltpu.VMEM((D, Hd), BF16), pltpu.VMEM((Hd, D), BF16),
            pltpu.SemaphoreType.DMA((2,)), pltpu.SemaphoreType.DMA((2,)),
        ],
    )
    return pl.pallas_call(
        _moe_body,
        grid_spec=grid_spec,
        out_shape=jax.ShapeDtypeStruct((2 * N + 2 * tm, D), F32),
        compiler_params=_cparams(("arbitrary",)),
        name="moe_experts",
    )(tile_expert, tile_valid, n_used, row_tok, row_tok, row_dst, row_w, h, w_gate, w_up, w_down)


def _moe_plan(ids, wts, N):
    tm = MOE_TILE
    E = N_EXPERTS
    A = 2 * N
    n_tiles = A // tm + E
    e_flat = ids.reshape(A)
    counts = jnp.sum((e_flat[:, None] == jnp.arange(E, dtype=jnp.int32)[None, :]).astype(jnp.int32), axis=0)
    fill = (-counts) % tm
    fj = jnp.arange(tm, dtype=jnp.int32)[None, :]
    fe = jnp.arange(E, dtype=jnp.int32)[:, None]
    fkey = jnp.where(fj < fill[:, None], fe, E).reshape(E * tm)
    a_idx = jnp.arange(A, dtype=jnp.int32)
    n_fill = E * tm
    keys = jnp.concatenate([e_flat, fkey])
    tok = jnp.concatenate([a_idx // 2, jnp.zeros((n_fill,), jnp.int32)])
    dst = jnp.concatenate([(a_idx % 2) * N + a_idx // 2, jnp.full((n_fill,), -1, jnp.int32)])
    wgt = jnp.concatenate([wts.reshape(A), jnp.zeros((n_fill,), F32)])
    key_s, tok_s, dst_s, w_s = lax.sort((keys, tok, dst, wgt), num_keys=1, is_stable=True)
    n_used = jnp.sum(counts + fill) // tm
    t_idx = jnp.arange(n_tiles, dtype=jnp.int32)
    te = jnp.minimum(key_s.reshape(n_tiles, tm)[:, 0], E - 1)
    last_e = jnp.max(jnp.where(t_idx < n_used, te, 0))
    te = jnp.where(t_idx < n_used, te, last_e).astype(jnp.int32)
    dst2 = dst_s.reshape(n_tiles, tm)
    tile_valid = jnp.sum((dst2 >= 0).astype(jnp.int32), axis=1)
    tile_rows = jnp.where(t_idx < n_used, (tile_valid + 7) // 8 * 8, 0).astype(jnp.int32)
    spare = A + (t_idx[:, None] % 2) * tm + fj
    row_dst = jnp.where(dst2 >= 0, dst2, spare).astype(jnp.int32)
    row_w = jnp.broadcast_to(w_s[:, None], (n_tiles * tm, LANES))
    return (te, tile_rows, n_used.reshape(1).astype(jnp.int32),
            tok_s.reshape(n_tiles, 1, tm), row_dst.reshape(n_tiles, 1, tm), row_w)


def _moe_out_body(y0_ref, y1_ref, x_ref, mod_ref, lng_ref, lnb_ref, o_ref, *, alpha):
    m = mod_ref[0, 0]
    f = y0_ref[...] + y1_ref[...]
    o_ref[0] = _layer_norm_rows(alpha * x_ref[0] + m[5:6] * f, lng_ref[...], lnb_ref[...])


def _moe_out(y, xs, modt, ln_g, ln_b, n_ctx_tiles, alpha):
    B, R, D = xs.shape
    tm = TOKEN_TILE
    rt = R // tm
    if n_ctx_tiles < 0:
        seg = lambda j: 1
    else:
        seg = lambda j: jnp.where(j >= n_ctx_tiles, 1, 0)
    tok = pl.BlockSpec((1, tm, D), lambda b, j: (b, j, 0))
    vec = pl.BlockSpec((1, D), lambda b, j: (0, 0))
    return pl.pallas_call(
        functools.partial(_moe_out_body, alpha=alpha),
        grid=(B, R // tm),
        in_specs=[pl.BlockSpec((tm, D), lambda b, j: (b * rt + j, 0)),
                  pl.BlockSpec((tm, D), lambda b, j: (B * rt + b * rt + j, 0)),
                  tok, pl.BlockSpec((1, 1, 6, D), lambda b, j: (b, seg(j), 0, 0)), vec, vec],
        out_specs=tok,
        out_shape=jax.ShapeDtypeStruct((B, R, D), F32),
        compiler_params=_cparams(("arbitrary", "arbitrary")),
        name="moe_out_ln",
    )(y, y, xs, modt, ln_g.reshape(1, D), ln_b.reshape(1, D))


def _hier_moe(h, ids, wts, xs, modt, w_gate, w_up, w_down, layer, ln_g, ln_b, n_ctx_tiles, alpha):
    B, R, D = xs.shape
    N = B * R
    plan = _moe_plan(ids.reshape(N, LANES)[:, :2], wts.reshape(N, LANES)[:, :2], N)
    y = _moe_experts(h.reshape(N, D), *plan, w_gate, w_up, w_down, layer)
    return _moe_out(y, xs, modt, ln_g, ln_b, n_ctx_tiles, alpha)


def _gla_proj_body(x_ref, mod_ref, w_ref, o_ref, *, q_cols, q_scale):
    m = mod_ref[0, 0]
    z = (x_ref[0] * (1.0 + m[1:2]) + m[0:1]).astype(BF16)
    acc = jnp.dot(z, w_ref[...], preferred_element_type=F32)
    half = pl.program_id(0)

    @pl.when(half == 0)
    def _():
        o_ref[0, :, :q_cols] = (acc[:, :q_cols] * q_scale).astype(BF16)
        o_ref[0, :, q_cols:] = acc[:, q_cols:].astype(BF16)

    @pl.when(half != 0)
    def _():
        o_ref[0] = acc.astype(BF16)


def _gla_proj(xs, modt, w_bf16, n_ctx_tiles, q_cols, q_scale):
    B, S, D = xs.shape
    NO = w_bf16.shape[1]
    tm = TOKEN_TILE
    seg = lambda j: jnp.where(j >= n_ctx_tiles, 1, 0)
    return pl.pallas_call(
        functools.partial(_gla_proj_body, q_cols=q_cols, q_scale=q_scale),
        grid=(2, B, S // tm),
        in_specs=[
            pl.BlockSpec((1, tm, D), lambda hf, b, j: (b, j, 0)),
            pl.BlockSpec((1, 1, 6, D), lambda hf, b, j: (b, seg(j), 0, 0)),
            pl.BlockSpec((D, NO // 2), lambda hf, b, j: (0, hf)),
        ],
        out_specs=pl.BlockSpec((1, tm, NO // 2), lambda hf, b, j: (b, j, hf)),
        out_shape=jax.ShapeDtypeStruct((B, S, NO), BF16),
        compiler_params=_cparams(("arbitrary", "arbitrary", "arbitrary")),
        name="gla_proj",
    )(xs, modt, w_bf16)


def _gla_gate_body(x_ref, mod_ref, w1_ref, w2_ref, b_ref, o_ref):
    m = mod_ref[0, 0]
    z = (x_ref[0] * (1.0 + m[1:2]) + m[0:1]).astype(BF16)
    t = jnp.dot(z, w1_ref[...], preferred_element_type=F32)
    pre = jnp.dot(t.astype(BF16), w2_ref[...], preferred_element_type=F32) + b_ref[...]
    o_ref[0] = (jnp.minimum(pre, 0.0) - jnp.log(1.0 + jnp.exp(-jnp.abs(pre)))) * (1.0 / GLA_TAU)


def _gla_gates(xs, modt, w1, w2, gb, n_ctx_tiles):
    B, S, D = xs.shape
    rank = w1.shape[2]
    kd = w2.shape[2]
    tm = TOKEN_TILE
    w1c = jnp.zeros((D, LANES), F32).at[:, :rank].set(w1[0]).at[:, rank:2 * rank].set(w1[1]).astype(BF16)
    w2c = (jnp.zeros((LANES, 2 * kd), F32).at[:rank, :kd].set(w2[0]).at[rank:2 * rank, kd:].set(w2[1])
           .astype(BF16))
    seg = lambda j: jnp.where(j >= n_ctx_tiles, 1, 0)
    return pl.pallas_call(
        _gla_gate_body,
        grid=(B, S // tm),
        in_specs=[
            pl.BlockSpec((1, tm, D), lambda b, j: (b, j, 0)),
            pl.BlockSpec((1, 1, 6, D), lambda b, j: (b, seg(j), 0, 0)),
            pl.BlockSpec((D, LANES), lambda b, j: (0, 0)),
            pl.BlockSpec((LANES, 2 * kd), lambda b, j: (0, 0)),
            pl.BlockSpec((1, 2 * kd), lambda b, j: (0, 0)),
        ],
        out_specs=pl.BlockSpec((1, tm, 2 * kd), lambda b, j: (b, j, 0)),
        out_shape=jax.ShapeDtypeStruct((B, S, 2 * kd), F32),
        compiler_params=_cparams(("arbitrary", "arbitrary")),
        name="gla_gates",
    )(xs, modt, w1c, w2c, gb.reshape(1, 2 * kd))


GLA_LEVELS = (32, 16, 8, 4, 2, 1)


def _gla_constants():
    L = GLA_CHUNK
    t = np.arange(L)[:, None]
    u = np.arange(L)[None, :]
    mats, masks = [], []
    for bwd in (False, True):
        blocks = []
        blocks.append((u >= t) if bwd else (u <= t))
        blocks.append((u < t) if bwd else (u > t))
        q_lv, k_lv, m_lv = [], [], []
        for m in GLA_LEVELS:
            base = (t // (2 * m)) * (2 * m)
            ubase = (u // (2 * m)) * (2 * m)
            t_hi = (t % (2 * m)) >= m
            if not bwd:
                r = base + m - 1
                q_lv.append(t_hi & (u > r) & (u <= t))
                k_lv.append(~t_hi & (u > t) & (u <= r))
                m_lv.append(t_hi & ((u % (2 * m)) < m) & (base == ubase))
            else:
                r = base + m
                q_lv.append(~t_hi & (u >= t) & (u < r))
                k_lv.append(t_hi & (u >= r) & (u < t))
                m_lv.append(~t_hi & ((u % (2 * m)) >= m) & (base == ubase))
        blocks += [ql | kl for ql, kl in zip(q_lv, k_lv)]
        blocks.append(np.ones((8, L), bool))
        a = np.concatenate(blocks, axis=0).astype(np.float32)
        mats.append(np.concatenate([a, a, a], axis=1))
        m_lv.append(t == u)
        masks.append(np.concatenate(m_lv, axis=0).astype(np.float32))
    return np.stack(mats), np.stack(masks)


LOG2_E = 1.4426950408889634


def _gla_chunk_step(q, k, v, g, a_ref, mk_ref, st_ref, hd):
    L = GLA_CHUNK
    nlv = len(GLA_LEVELS)
    g = g * LOG2_E
    g_hi = g.astype(BF16)
    r1 = g - g_hi.astype(F32)
    g_mid = r1.astype(BF16)
    g_lo = (r1 - g_mid.astype(F32)).astype(BF16)
    gs = jnp.concatenate([g_hi, g_mid, g_lo], axis=0)
    ex = jnp.exp2(jnp.dot(a_ref[...], gs, preferred_element_type=F32))
    qf = q.astype(F32)
    kf = k.astype(F32)
    q_inter = (qf * ex[0:L]).astype(BF16)
    k_state = (kf * ex[L:2 * L]).astype(BF16)
    dec = ex[(2 + nlv) * L:(2 + nlv) * L + 1]
    nt = (((1,), (1,)), ((), ()))
    att = lax.dot_general(q, k, nt, preferred_element_type=F32) * mk_ref[nlv * L:(nlv + 1) * L]
    for lv in range(nlv):
        f = ex[(2 + lv) * L:(3 + lv) * L]
        ql = (qf * f).astype(BF16)
        kl = (kf * f).astype(BF16)
        att = att + lax.dot_general(ql, kl, nt, preferred_element_type=F32) * mk_ref[lv * L:(lv + 1) * L]
    st = st_ref[hd]
    o = jnp.dot(att.astype(BF16), v, preferred_element_type=F32)
    o = o + lax.dot_general(q_inter, st.astype(BF16), nt, preferred_element_type=F32)
    upd = lax.dot_general(v, k_state, (((0,), (0,)), ((), ())), preferred_element_type=F32)
    st_ref[hd] = st * dec + upd
    return o


def _gla_scan_body(qf_ref, kf_ref, vf_ref, gf_ref, qb_ref, kb_ref, vb_ref, gb_ref,
                   af_ref, ab_ref, mf_ref, mb_ref, of_ref, ob_ref, stf_ref, stb_ref):
    @pl.when(pl.program_id(1) == 0)
    def _():
        stf_ref[...] = jnp.zeros(stf_ref.shape, stf_ref.dtype)
        stb_ref[...] = jnp.zeros(stb_ref.shape, stb_ref.dtype)

    _, dv, dk = stf_ref.shape
    for hd in range(GLA_HEADS):
        ks = slice(hd * dk, (hd + 1) * dk)
        vs = slice(hd * dv, (hd + 1) * dv)
        o = _gla_chunk_step(qf_ref[0, :, ks], kf_ref[0, :, ks], vf_ref[0, :, vs], gf_ref[0, :, ks],
                            af_ref, mf_ref, stf_ref, hd)
        of_ref[0, :, vs] = o.astype(of_ref.dtype)
        o = _gla_chunk_step(qb_ref[0, :, ks], kb_ref[0, :, ks], vb_ref[0, :, vs], gb_ref[0, :, ks],
                            ab_ref, mb_ref, stb_ref, hd)
        ob_ref[0, :, vs] = o.astype(ob_ref.dtype)


def _gla_scan(proj, lg, C, T):
    B, S, NO = proj.shape
    kd = lg.shape[2] // 2
    D = NO - 2 * kd
    D = D // 2
    dk = kd // GLA_HEADS
    dv = D // GLA_HEADS
    L = GLA_CHUNK
    n_ctx = C // L
    n_all = S // L
    amat, masks = _gla_constants()
    amat = jnp.asarray(amat, BF16)
    masks = jnp.asarray(masks, F32)

    def fwd(c):
        return c

    def bwd(c):
        return jnp.where(c < n_ctx, n_ctx - 1 - c, n_ctx + n_all - 1 - c)

    assert 2 * kd == D

    def specs(chunk, d):
        return [
            pl.BlockSpec((1, L, kd), lambda b, c: (b, chunk(c), 0)),
            pl.BlockSpec((1, L, kd), lambda b, c: (b, chunk(c), 1)),
            pl.BlockSpec((1, L, D), lambda b, c: (b, chunk(c), 1)),
            pl.BlockSpec((1, L, kd), lambda b, c: (b, chunk(c), d)),
        ]

    const = lambda arr: pl.BlockSpec(arr.shape[1:], lambda b, c: (0, 0))
    out = jax.ShapeDtypeStruct((B, S, D), BF16)
    state = pltpu.VMEM((GLA_HEADS, dv, dk), F32)
    return pl.pallas_call(
        _gla_scan_body,
        grid=(B, n_all),
        in_specs=specs(fwd, 0) + specs(bwd, 1) + [const(amat), const(amat), const(masks), const(masks)],
        out_specs=[pl.BlockSpec((1, L, D), lambda b, c: (b, fwd(c), 0)),
                   pl.BlockSpec((1, L, D), lambda b, c: (b, bwd(c), 0))],
        out_shape=[out, out],
        scratch_shapes=[state, state],
        compiler_params=_cparams(("arbitrary", "arbitrary")),
        name="gla_scan",
    )(proj, proj, proj, lg, proj, proj, proj, lg, amat[0], amat[1], masks[0], masks[1])


def kernel(x, c, ctx, c_ctx, mod_w, mod_b, ln_g, ln_b, attn_w_qkv, attn_w_o, attn_sink,
           gla_w_in, gla_gate_w1, gla_gate_w2, gla_gate_b, gla_norm_g, gla_w_o,
           moe_group_w, moe_group_b, moe_router_w, moe_router_b, moe_w_gate, moe_w_up, moe_w_down):
    B, T, D = x.shape
    C = ctx.shape[1]
    depth = mod_w.shape[0]
    assert depth == 2 and D == N_HEADS * HEAD_DIM
    assert C % TOKEN_TILE == 0 and T % TOKEN_TILE == 0 and T % GRID_W == 0
    alpha = (2 * depth) ** 0.25
    n_ctx_tiles = C // TOKEN_TILE

    m_rows = -(-(B + 1) // 8) * 8
    cond = jnp.zeros((m_rows, D), F32).at[:B].set(c).at[B].set(c_ctx)
    mods = _adaln_mods(cond, mod_w, mod_b)

    def mod_table(l):
        lat = mods[l, :B].reshape(B, 1, 6, D)
        cm = jnp.broadcast_to(mods[l, B].reshape(1, 1, 6, D), (B, 1, 6, D))
        return jnp.concatenate([cm, lat], axis=1)

    modt = mod_table(0)
    cos, sin = _rope_tables(C, T)
    qkv = _qkv_proj(ctx, x, modt, attn_w_qkv[0].astype(BF16), cos, sin, n_ctx_tiles)
    attn = _attention(qkv, attn_sink[0], C, T)
    router = _router_params(moe_group_w[0], moe_group_b[0], moe_router_w[0], moe_router_b[0])
    xs, h, ids, wts = _attn_out(attn, ctx, x, modt, attn_w_o[0].astype(BF16), ln_g[0, 0], ln_b[0, 0], router,
                                n_ctx_tiles, alpha)
    xs = _hier_moe(h, ids, wts, xs, modt, moe_w_gate, moe_w_up, moe_w_down, 0, ln_g[0, 1], ln_b[0, 1],
                   n_ctx_tiles, alpha)

    modt = mod_table(1)
    kd = gla_gate_w2.shape[3]
    dk = kd // GLA_HEADS
    proj = _gla_proj(xs, modt, gla_w_in[0].astype(BF16), n_ctx_tiles, kd, dk ** -0.5)
    lg = _gla_gates(xs, modt, gla_gate_w1[0], gla_gate_w2[0], gla_gate_b[0], n_ctx_tiles)
    o2 = _gla_scan(proj, lg, C, T)
    router = _router_params(moe_group_w[1], moe_group_b[1], moe_router_w[1], moe_router_b[1])
    xl, h, ids, wts = _gla_out(o2, proj, gla_norm_g[0], xs, modt, gla_w_o[0].astype(BF16), ln_g[1, 0],
                               ln_b[1, 0], router, n_ctx_tiles, T, alpha)
    return _hier_moe(h, ids, wts, xl, modt, moe_w_gate, moe_w_up, moe_w_down, 1, ln_g[1, 1], ln_b[1, 1],
                     -1, alpha)
```

```python
import functools

import numpy as np
import jax
import jax.numpy as jnp
from jax import lax
from jax.experimental import pallas as pl
from jax.experimental.pallas import tpu as pltpu

F32 = jnp.float32
BF16 = jnp.bfloat16

N_HEADS = 16
N_KV_HEADS = 4
HEAD_DIM = 128
GROUP = N_HEADS // N_KV_HEADS
WINDOW = 128
GRID_W = 64
ROPE_BASE = 10000.0
GLA_HEADS = 4
GLA_TAU = 16.0
GLA_CHUNK = 64
MOE_GROUPS = 4
MOE_EXPERTS_PER_GROUP = 8
N_EXPERTS = MOE_GROUPS * MOE_EXPERTS_PER_GROUP
LN_EPS = 1e-5
RMS_EPS = 1e-6

V7X_VMEM_LIMIT_BYTES = 56 * 1024 * 1024
LANES = 128
TOKEN_TILE = 256
MOE_TILE = 256
MODS_COL_TILE = 1024


def _cparams(sem):
    return pltpu.CompilerParams(dimension_semantics=sem, vmem_limit_bytes=V7X_VMEM_LIMIT_BYTES)


def _silu(v):
    return v / (1.0 + jnp.exp(-v))


def _pack_bf16_pairs(v):
    n = v.shape[1] // 2
    lo = pltpu.bitcast(v[:, :n].astype(BF16).astype(F32), jnp.uint32)
    hi = pltpu.bitcast(v[:, n:].astype(BF16).astype(F32), jnp.uint32)
    return hi | (lo >> 16)


def _unpack_bf16_pairs(w):
    lo = pltpu.bitcast(w << 16, F32)
    hi = pltpu.bitcast(w & jnp.uint32(0xFFFF0000), F32)
    return lo, hi


def _mods_body(cond_ref, w_ref, b_ref, o_ref):
    a = _silu(cond_ref[...]).astype(BF16)
    o_ref[0] = jnp.dot(a, w_ref[0].astype(BF16), preferred_element_type=F32) + b_ref[0]


def _adaln_mods(cond, mod_w, mod_b):
    L, D, N6 = mod_w.shape
    M = cond.shape[0]
    tn = MODS_COL_TILE
    return pl.pallas_call(
        _mods_body,
        grid=(L, N6 // tn),
        in_specs=[
            pl.BlockSpec((M, D), lambda l, j: (0, 0)),
            pl.BlockSpec((1, D, tn), lambda l, j: (l, 0, j)),
            pl.BlockSpec((1, 1, tn), lambda l, j: (l, 0, j)),
        ],
        out_specs=pl.BlockSpec((1, M, tn), lambda l, j: (l, 0, j)),
        out_shape=jax.ShapeDtypeStruct((L, M, N6), F32),
        compiler_params=_cparams(("arbitrary", "arbitrary")),
        name="adaln_mods",
    )(cond, mod_w, mod_b.reshape(L, 1, N6))


def _stream_rows(c_ref, x_ref, n_ctx_tiles):
    return jnp.where(pl.program_id(1) < n_ctx_tiles, c_ref[0], x_ref[0])


def _stream_specs(tm, D, n_ctx_tiles):
    return [pl.BlockSpec((1, tm, D), lambda b, j: (b, jnp.minimum(j, n_ctx_tiles - 1), 0)),
            pl.BlockSpec((1, tm, D), lambda b, j: (b, jnp.maximum(j - n_ctx_tiles, 0), 0))]


def _qkv_body(c_ref, x_ref, mod_ref, w_ref, cos_ref, sin_ref, o_ref, *, n_ctx_tiles, n_rot_heads, n_q_heads,
              scale):
    m = mod_ref[0, 0]
    h = (_stream_rows(c_ref, x_ref, n_ctx_tiles) * (1.0 + m[1:2]) + m[0:1]).astype(BF16)
    acc = jnp.dot(h, w_ref[...], preferred_element_type=F32)
    cos = cos_ref[...]
    sin = sin_ref[...]
    lane = lax.broadcasted_iota(jnp.int32, cos.shape, 1)
    first = (lane & 32) == 0
    for hd in range(n_rot_heads):
        y = acc[:, hd * HEAD_DIM:(hd + 1) * HEAD_DIM]
        partner = jnp.where(first, pltpu.roll(y, HEAD_DIM - 32, 1), pltpu.roll(y, 32, 1))
        r = y * cos + partner * sin
        if hd < n_q_heads:
            r = r * scale
        o_ref[0, :, hd * HEAD_DIM:(hd + 1) * HEAD_DIM] = r.astype(BF16)
    rest = n_rot_heads * HEAD_DIM
    o_ref[0, :, rest:] = acc[:, rest:].astype(BF16)


def _rope_tables(C, T):
    half = HEAD_DIM // 2
    pos = np.arange(T)
    inv_freq = ROPE_BASE ** (-np.arange(0, half, 2, dtype=np.float32) / half)
    ang_r = (pos // GRID_W).astype(np.float32)[:, None] * inv_freq
    ang_c = (pos % GRID_W).astype(np.float32)[:, None] * inv_freq
    ang_r = jnp.asarray(ang_r, F32)
    ang_c = jnp.asarray(ang_c, F32)
    cos = jnp.concatenate([jnp.cos(ang_r)] * 2 + [jnp.cos(ang_c)] * 2, axis=-1)
    sin = jnp.concatenate([-jnp.sin(ang_r), jnp.sin(ang_r), -jnp.sin(ang_c), jnp.sin(ang_c)], axis=-1)
    cos = jnp.concatenate([jnp.ones((C, HEAD_DIM), F32), cos], axis=0)
    sin = jnp.concatenate([jnp.zeros((C, HEAD_DIM), F32), sin], axis=0)
    return cos, sin


def _qkv_proj(ctx, x, modt, w_bf16, cos, sin, n_ctx_tiles):
    B, T, D = x.shape
    S = ctx.shape[1] + T
    NO = w_bf16.shape[1]
    tm = TOKEN_TILE
    body = functools.partial(_qkv_body, n_ctx_tiles=n_ctx_tiles, n_rot_heads=N_HEADS + N_KV_HEADS,
                             n_q_heads=N_HEADS, scale=HEAD_DIM ** -0.5)
    seg = lambda j: jnp.where(j >= n_ctx_tiles, 1, 0)
    return pl.pallas_call(
        body,
        grid=(B, S // tm),
        in_specs=_stream_specs(tm, D, n_ctx_tiles) + [
            pl.BlockSpec((1, 1, 6, D), lambda b, j: (b, seg(j), 0, 0)),
            pl.BlockSpec((D, NO), lambda b, j: (0, 0)),
            pl.BlockSpec((tm, HEAD_DIM), lambda b, j: (j, 0)),
            pl.BlockSpec((tm, HEAD_DIM), lambda b, j: (j, 0)),
        ],
        out_specs=pl.BlockSpec((1, tm, NO), lambda b, j: (b, j, 0)),
        out_shape=jax.ShapeDtypeStruct((B, S, NO), BF16),
        compiler_params=_cparams(("arbitrary", "arbitrary")),
        name="qkv_rope",
    )(ctx, x, modt, w_bf16, cos, sin)


ATTN_KV_PER_STEP = 2


def _attn_one_head(sink_ref, head, q, kc, vc, k_win, v_win, n, n_lat_blk, latent):
    blk = WINDOW
    rows = GROUP * blk
    row = lax.broadcasted_iota(jnp.int32, (rows, 1), 0)
    sink = jnp.full((rows, 1), sink_ref[head * GROUP + GROUP - 1], F32)
    for g in range(GROUP - 2, -1, -1):
        sink = jnp.where(row < (g + 1) * blk, sink_ref[head * GROUP + g], sink)
    nt = (((1,), (1,)), ((), ()))
    s_c = lax.dot_general(q, kc, nt, preferred_element_type=F32)
    scores = [(s_c, vc)]
    if latent:
        a = lax.broadcasted_iota(jnp.int32, (rows, blk), 0) % blk
        key = lax.broadcasted_iota(jnp.int32, (rows, blk), 1)
        s0 = lax.dot_general(q, k_win[0], nt, preferred_element_type=F32)
        s0 = jnp.where((key >= a) & (n > 0), s0, -jnp.inf)
        s1 = lax.dot_general(q, k_win[1], nt, preferred_element_type=F32)
        s2 = lax.dot_general(q, k_win[2], nt, preferred_element_type=F32)
        s2 = jnp.where((key <= a) & (n < n_lat_blk - 1), s2, -jnp.inf)
        scores += [(s0, v_win[0]), (s1, v_win[1]), (s2, v_win[2])]
    lane_tiles = [s[:, j * LANES:(j + 1) * LANES] for s, _ in scores for j in range(s.shape[1] // LANES)]
    m_el = lane_tiles[0]
    for t in lane_tiles[1:]:
        m_el = jnp.maximum(m_el, t)
    m = jnp.maximum(jnp.max(m_el, axis=1, keepdims=True), sink)
    acc = jnp.zeros((rows, HEAD_DIM), F32)
    l_el = jnp.zeros((rows, LANES), F32)
    for s, v in scores:
        p = jnp.exp(s - m)
        for j in range(s.shape[1] // LANES):
            l_el = l_el + p[:, j * LANES:(j + 1) * LANES]
        acc = acc + jnp.dot(p.astype(BF16), v, preferred_element_type=F32)
    denom = jnp.exp(sink - m) + jnp.sum(l_el, axis=1, keepdims=True)
    return acc / denom


def _attn_body(sink_ref, q_ref, kc_ref, vc_ref, k0_ref, k1_ref, k2_ref, v0_ref, v1_ref, v2_ref, o_ref,
               *, n_ctx_blk, n_lat_blk):
    kh0 = pl.program_id(1) * ATTN_KV_PER_STEP
    qb = pl.program_id(2)
    blk = WINDOW
    gw = GROUP * HEAD_DIM

    def run(latent):
        n = qb - n_ctx_blk
        for i in range(ATTN_KV_PER_STEP):
            hs = slice(i * HEAD_DIM, (i + 1) * HEAD_DIM)
            q = jnp.concatenate([q_ref[0, :, i * gw + g * HEAD_DIM:i * gw + (g + 1) * HEAD_DIM]
                                 for g in range(GROUP)], axis=0)
            k_win = [r[0, :, hs] for r in (k0_ref, k1_ref, k2_ref)] if latent else None
            v_win = [r[0, :, hs] for r in (v0_ref, v1_ref, v2_ref)] if latent else None
            o = _attn_one_head(sink_ref, kh0 + i, q, kc_ref[0, :, hs], vc_ref[0, :, hs], k_win, v_win,
                               n, n_lat_blk, latent)
            for g in range(GROUP):
                o_ref[0, :, i * gw + g * HEAD_DIM:i * gw + (g + 1) * HEAD_DIM] = (
                    o[g * blk:(g + 1) * blk].astype(o_ref.dtype))

    @pl.when(qb < n_ctx_blk)
    def _():
        run(False)

    @pl.when(qb >= n_ctx_blk)
    def _():
        run(True)


def _attention(qkv, sink, C, T):
    B, S, _ = qkv.shape
    blk = WINDOW
    n_ctx_blk = C // blk
    n_lat_blk = T // blk
    kvs = ATTN_KV_PER_STEP
    kvw = kvs * HEAD_DIM
    k_col = N_HEADS // kvs
    v_col = (N_HEADS + N_KV_HEADS) // kvs

    def win(j, col):
        def im(b, kh, qb):
            n = jnp.maximum(qb - n_ctx_blk, 0)
            return (b, n_ctx_blk + jnp.clip(n + j - 1, 0, n_lat_blk - 1), col + kh)
        return im

    kv_blk = (1, blk, kvw)
    q_blk = (1, blk, kvs * GROUP * HEAD_DIM)
    body = functools.partial(_attn_body, n_ctx_blk=n_ctx_blk, n_lat_blk=n_lat_blk)
    return pl.pallas_call(
        body,
        grid=(B, N_KV_HEADS // kvs, S // blk),
        in_specs=[
            pl.BlockSpec(memory_space=pltpu.SMEM),
            pl.BlockSpec(q_blk, lambda b, kh, qb: (b, qb, kh)),
            pl.BlockSpec((1, C, kvw), lambda b, kh, qb: (b, 0, k_col + kh)),
            pl.BlockSpec((1, C, kvw), lambda b, kh, qb: (b, 0, v_col + kh)),
            pl.BlockSpec(kv_blk, win(0, k_col)), pl.BlockSpec(kv_blk, win(1, k_col)),
            pl.BlockSpec(kv_blk, win(2, k_col)),
            pl.BlockSpec(kv_blk, win(0, v_col)), pl.BlockSpec(kv_blk, win(1, v_col)),
            pl.BlockSpec(kv_blk, win(2, v_col)),
        ],
        out_specs=pl.BlockSpec(q_blk, lambda b, kh, qb: (b, qb, kh)),
        out_shape=jax.ShapeDtypeStruct((B, S, N_HEADS * HEAD_DIM), BF16),
        compiler_params=_cparams(("arbitrary", "arbitrary", "arbitrary")),
        name="window_attention",
    )(sink, qkv, qkv, qkv, qkv, qkv, qkv, qkv, qkv, qkv)


def _layer_norm_rows(r, g, b):
    mu = jnp.mean(r, axis=-1, keepdims=True)
    rc = r - mu
    var = jnp.mean(rc * rc, axis=-1, keepdims=True)
    return rc * lax.rsqrt(var + LN_EPS) * g + b


def _route(logits):
    lane = lax.broadcasted_iota(jnp.int32, logits.shape, 1)
    neg = -jnp.inf

    def first_max(vals):
        mx = jnp.max(vals, axis=1, keepdims=True)
        idx = jnp.min(jnp.where(vals == mx, lane, LANES), axis=1, keepdims=True)
        return mx, idx

    gmask = lane < MOE_GROUPS
    gl = jnp.where(gmask, logits, neg)
    gmax, gidx = first_max(gl)
    gsum = jnp.sum(jnp.where(gmask, jnp.exp(gl - gmax), 0.0), axis=1, keepdims=True)
    g_w = 1.0 / gsum
    lo = MOE_GROUPS + gidx * MOE_EXPERTS_PER_GROUP
    el = jnp.where((lane >= lo) & (lane < lo + MOE_EXPERTS_PER_GROUP), logits, neg)
    v1, i1 = first_max(el)
    v2, i2 = first_max(jnp.where(lane == i1, neg, el))
    e2 = jnp.exp(v2 - v1)
    w1 = g_w / (1.0 + e2)
    w2 = g_w * e2 / (1.0 + e2)
    ids = jnp.where(lane == 0, i1 - MOE_GROUPS, jnp.where(lane == 1, i2 - MOE_GROUPS, 0))
    wts = jnp.where(lane == 0, w1, jnp.where(lane == 1, w2, 0.0))
    return ids, wts


def _mix_out_body(*refs, alpha, gla, n_ctx_tiles):
    wr_ref, br_ref, xo_ref, h_ref, id_ref, wt_ref = refs[-6:]
    refs = refs[:-6]
    if gla:
        of_ref, ob_ref, r_ref, ng_ref, x_ref, mod_ref, w_ref, lng_ref, lnb_ref = refs
        x = x_ref[0]
        o = of_ref[0].astype(F32) + ob_ref[0].astype(F32)
        dv = ng_ref.shape[1]
        parts = []
        for hd in range(GLA_HEADS):
            oh = o[:, hd * dv:(hd + 1) * dv]
            ms = jnp.mean(oh * oh, axis=-1, keepdims=True)
            parts.append(oh * lax.rsqrt(ms + RMS_EPS) * ng_ref[...])
        a = (jnp.concatenate(parts, axis=1) * _silu(r_ref[0].astype(F32))).astype(BF16)
    else:
        a_ref, c_ref, x_ref, mod_ref, w_ref, lng_ref, lnb_ref = refs
        a = a_ref[0]
        x = _stream_rows(c_ref, x_ref, n_ctx_tiles)
    m = mod_ref[0, 0]
    y = jnp.dot(a, w_ref[...], preferred_element_type=F32)
    xn = _layer_norm_rows(alpha * x + m[2:3] * y, lng_ref[...], lnb_ref[...])
    xo_ref[0] = xn
    h = xn * (1.0 + m[4:5]) + m[3:4]
    h_ref[0] = _pack_bf16_pairs(h)
    ids, wts = _route(jnp.dot(h.astype(BF16), wr_ref[...], preferred_element_type=F32) + br_ref[...])
    id_ref[0] = ids
    wt_ref[0] = wts


def _router_params(wg, bg, we, be):
    D = wg.shape[0]
    pad = LANES - MOE_GROUPS - N_EXPERTS
    wr = jnp.concatenate([wg, we, jnp.zeros((D, pad), F32)], axis=1).astype(BF16)
    br = jnp.concatenate([bg, be, jnp.zeros((pad,), F32)]).reshape(1, LANES)
    return wr, br


def _mix_out_tail(B, R, D, tm):
    tok = pl.BlockSpec((1, tm, D), lambda b, j: (b, j, 0))
    packed = pl.BlockSpec((1, tm, D // 2), lambda b, j: (b, j, 0))
    lane_tile = pl.BlockSpec((1, tm, LANES), lambda b, j: (b, j, 0))
    in_specs = [pl.BlockSpec((D, LANES), lambda b, j: (0, 0)), pl.BlockSpec((1, LANES), lambda b, j: (0, 0))]
    out_specs = [tok, packed, lane_tile, lane_tile]
    out_shape = [jax.ShapeDtypeStruct((B, R, D), F32), jax.ShapeDtypeStruct((B, R, D // 2), jnp.uint32),
                 jax.ShapeDtypeStruct((B, R, LANES), jnp.int32), jax.ShapeDtypeStruct((B, R, LANES), F32)]
    return in_specs, out_specs, out_shape


def _attn_out(attn, ctx, x, modt, w_bf16, ln_g, ln_b, router, n_ctx_tiles, alpha):
    B, S, D = attn.shape
    tm = TOKEN_TILE
    seg = lambda j: jnp.where(j >= n_ctx_tiles, 1, 0)
    tok = pl.BlockSpec((1, tm, D), lambda b, j: (b, j, 0))
    vec = pl.BlockSpec((1, D), lambda b, j: (0, 0))
    r_in, out_specs, out_shape = _mix_out_tail(B, S, D, tm)
    return pl.pallas_call(
        functools.partial(_mix_out_body, alpha=alpha, gla=False, n_ctx_tiles=n_ctx_tiles),
        grid=(B, S // tm),
        in_specs=[tok] + _stream_specs(tm, D, n_ctx_tiles) + [
                  pl.BlockSpec((1, 1, 6, D), lambda b, j: (b, seg(j), 0, 0)),
                  pl.BlockSpec((D, D), lambda b, j: (0, 0)), vec, vec] + r_in,
        out_specs=out_specs,
        out_shape=out_shape,
        compiler_params=_cparams(("arbitrary", "arbitrary")),
        name="attn_out_ln",
    )(attn, ctx, x, modt, w_bf16, ln_g.reshape(1, D), ln_b.reshape(1, D), *router)


def _gla_out(o2, proj, norm_g, xs, modt, w_bf16, ln_g, ln_b, router, n_ctx_tiles, T, alpha):
    B, S, D = xs.shape
    tm = TOKEN_TILE
    dv = D // GLA_HEADS
    r_col = proj.shape[2] // D - 1
    off = n_ctx_tiles
    vec = pl.BlockSpec((1, D), lambda b, j: (0, 0))
    r_in, out_specs, out_shape = _mix_out_tail(B, T, D, tm)
    return pl.pallas_call(
        functools.partial(_mix_out_body, alpha=alpha, gla=True, n_ctx_tiles=n_ctx_tiles),
        grid=(B, T // tm),
        in_specs=[
            pl.BlockSpec((1, tm, D), lambda b, j: (b, j + off, 0)),
            pl.BlockSpec((1, tm, D), lambda b, j: (b, j + off, 0)),
            pl.BlockSpec((1, tm, D), lambda b, j: (b, j + off, r_col)),
            pl.BlockSpec((1, dv), lambda b, j: (0, 0)),
            pl.BlockSpec((1, tm, D), lambda b, j: (b, j + off, 0)),
            pl.BlockSpec((1, 1, 6, D), lambda b, j: (b, 1, 0, 0)),
            pl.BlockSpec((D, D), lambda b, j: (0, 0)), vec, vec,
        ] + r_in,
        out_specs=out_specs,
        out_shape=out_shape,
        compiler_params=_cparams(("arbitrary", "arbitrary")),
        name="gla_out_ln",
    )(o2[0], o2[1], proj, norm_g.reshape(1, dv), xs, modt, w_bf16, ln_g.reshape(1, D), ln_b.reshape(1, D),
      *router)


def _moe_body(te_ref, nv_ref, nu_ref, tok_ref, tokn_ref, dst_ref, rw_ref, h_hbm, wg_ref, wu_ref, wd_ref,
              y_hbm, xbuf, ybuf, wgb, wub, wdb, gsem, ssem):
    i = pl.program_id(0)
    n_used = nu_ref[0]
    slot = i % 2

    unroll = 8

    def gather(idx_ref, n_rows, s):
        def issue(r8, c):
            for u in range(unroll):
                r = r8 * unroll + u
                pltpu.make_async_copy(h_hbm.at[pl.ds(idx_ref[0, 0, r], 1), :],
                                      xbuf.at[s, pl.ds(r, 1), :], gsem.at[s]).start(priority=u % 2)
            return c
        lax.fori_loop(0, n_rows // unroll, issue, 0)

    def gather_wait(n_rows, s):
        n_rows = pl.multiple_of(n_rows, 8)
        pltpu.make_async_copy(h_hbm.at[pl.ds(0, n_rows), :], xbuf.at[s, pl.ds(0, n_rows), :], gsem.at[s]).wait()

    def scatter(n_rows, s):
        def issue(r8, c):
            for u in range(unroll):
                r = r8 * unroll + u
                pltpu.make_async_copy(ybuf.at[s, pl.ds(r, 1), :],
                                      y_hbm.at[pl.ds(dst_ref[0, 0, r], 1), :], ssem.at[s]).start(priority=u % 2)
            return c
        lax.fori_loop(0, n_rows // unroll, issue, 0)

    def scatter_wait(n_rows, s):
        n_rows = pl.multiple_of(n_rows, 8)
        pltpu.make_async_copy(ybuf.at[s, pl.ds(0, n_rows), :], y_hbm.at[pl.ds(0, n_rows), :], ssem.at[s]).wait()

    @pl.when(i == 0)
    def _():
        xbuf[...] = jnp.zeros(xbuf.shape, xbuf.dtype)
        tm = xbuf.shape[1]
        spare = y_hbm.shape[0] - 2 * tm
        for s in range(2):
            cp = pltpu.make_async_copy(xbuf.at[s], y_hbm.at[pl.ds(spare + s * tm, tm), :], ssem.at[s])
            cp.start()
            cp.wait()
        gather(tok_ref, nv_ref[0], 0)

    @pl.when(i + 1 < n_used)
    def _():
        gather(tokn_ref, nv_ref[i + 1], 1 - slot)

    @pl.when(i < n_used)
    def _():
        nv = nv_ref[i]
        gather_wait(nv, slot)

        @pl.when((i == 0) | (te_ref[i] != te_ref[jnp.maximum(i - 1, 0)]))
        def _():
            wgb[...] = wg_ref[0, 0].astype(BF16)
            wub[...] = wu_ref[0, 0].astype(BF16)
            wdb[...] = wd_ref[0, 0].astype(BF16)

        @pl.when(i >= 2)
        def _():
            scatter_wait(nv_ref[jnp.maximum(i - 2, 0)], slot)

        x = jnp.concatenate(_unpack_bf16_pairs(xbuf[slot]), axis=1).astype(BF16)
        g = jnp.dot(x, wgb[...], preferred_element_type=F32)
        u = jnp.dot(x, wub[...], preferred_element_type=F32)
        hid = (_silu(g) * u).astype(BF16)
        y = jnp.dot(hid, wdb[...], preferred_element_type=F32)
        ybuf[slot] = _pack_bf16_pairs(y * rw_ref[:, 0:1])
        scatter(nv, slot)

        @pl.when(i == n_used - 1)
        def _():
            @pl.when(i >= 1)
            def _():
                scatter_wait(nv_ref[jnp.maximum(i - 1, 0)], 1 - slot)
            scatter_wait(nv, slot)


def _moe_experts(h, tile_expert, tile_valid, n_used, row_tok, row_dst, row_w, w_gate, w_up, w_down, layer):
    N = h.shape[0]
    _, E, D, Hd = w_gate.shape
    tm = MOE_TILE
    n_tiles = tile_expert.shape[0]
    idx_blk = lambda f: pl.BlockSpec((1, 1, tm), f, memory_space=pltpu.SMEM)
    grid_spec = pltpu.PrefetchScalarGridSpec(
        num_scalar_prefetch=3,
        grid=(n_tiles,),
        in_specs=[
            idx_blk(lambda i, te, nv, nu: (i, 0, 0)),
            idx_blk(lambda i, te, nv, nu: (jnp.minimum(i + 1, n_tiles - 1), 0, 0)),
            idx_blk(lambda i, te, nv, nu: (i, 0, 0)),
            pl.BlockSpec((tm, LANES), lambda i, te, nv, nu: (i, 0)),
            pl.BlockSpec(memory_space=pl.ANY),
            pl.BlockSpec((1, 1, D, Hd), lambda i, te, nv, nu: (layer, te[i], 0, 0)),
            pl.BlockSpec((1, 1, D, Hd), lambda i, te, nv, nu: (layer, te[i], 0, 0)),
            pl.BlockSpec((1, 1, Hd, D), lambda i, te, nv, nu: (layer, te[i], 0, 0)),
        ],
        out_specs=pl.BlockSpec(memory_space=pl.ANY),
        scratch_shapes=[
            pltpu.VMEM((2, tm, D // 2), jnp.uint32), pltpu.VMEM((2, tm, D // 2), jnp.uint32),
            pltpu.VMEM((D, Hd), BF16), pltpu.VMEM((D, Hd), BF16), pltpu.VMEM((Hd, D), BF16),
            pltpu.SemaphoreType.DMA((2,)), pltpu.SemaphoreType.DMA((2,)),
        ],
    )
    return pl.pallas_call(
        _moe_body,
        grid_spec=grid_spec,
        out_shape=jax.ShapeDtypeStruct((2 * N + 2 * tm, D // 2), jnp.uint32),
        compiler_params=_cparams(("arbitrary",)),
        name="moe_experts",
    )(tile_expert, tile_valid, n_used, row_tok, row_tok, row_dst, row_w, h, w_gate, w_up, w_down)


def _moe_plan(ids, wts, N):
    tm = MOE_TILE
    E = N_EXPERTS
    A = 2 * N
    n_tiles = A // tm + E
    e_flat = ids.reshape(A)
    counts = jnp.sum((e_flat[:, None] == jnp.arange(E, dtype=jnp.int32)[None, :]).astype(jnp.int32), axis=0)
    fill = (-counts) % tm
    fj = jnp.arange(tm, dtype=jnp.int32)[None, :]
    fe = jnp.arange(E, dtype=jnp.int32)[:, None]
    fkey = jnp.where(fj < fill[:, None], fe, E).reshape(E * tm)
    n_fill = E * tm
    shift = 20
    assert A + n_fill < (1 << shift) and (E + 1) << shift < (1 << 31)
    idx = jnp.arange(A + n_fill, dtype=jnp.int32)
    keys = (jnp.concatenate([e_flat, fkey]) << shift) | idx
    wgt = jnp.concatenate([wts.reshape(A), jnp.zeros((n_fill,), F32)])
    key_s, w_s = lax.sort((keys, wgt), num_keys=1)
    e_s = (key_s >> shift).reshape(n_tiles, tm)
    a_s = (key_s & ((1 << shift) - 1)).reshape(n_tiles, tm)
    real = a_s < A
    n_used = jnp.sum(counts + fill) // tm
    t_idx = jnp.arange(n_tiles, dtype=jnp.int32)
    te = jnp.minimum(e_s[:, 0], E - 1)
    last_e = jnp.max(jnp.where(t_idx < n_used, te, 0))
    te = jnp.where(t_idx < n_used, te, last_e).astype(jnp.int32)
    tile_valid = jnp.sum(real.astype(jnp.int32), axis=1)
    tile_rows = jnp.where(t_idx < n_used, (tile_valid + 7) // 8 * 8, 0).astype(jnp.int32)
    spare = A + (t_idx[:, None] % 2) * tm + fj
    row_tok = jnp.where(real, a_s >> 1, 0).astype(jnp.int32)
    row_dst = jnp.where(real, (a_s & 1) * N + (a_s >> 1), spare).astype(jnp.int32)
    row_w = jnp.broadcast_to(w_s[:, None], (n_tiles * tm, LANES))
    return (te, tile_rows, n_used.reshape(1).astype(jnp.int32),
            row_tok.reshape(n_tiles, 1, tm), row_dst.reshape(n_tiles, 1, tm), row_w)


def _moe_out_body(y0_ref, y1_ref, x_ref, mod_ref, lng_ref, lnb_ref, o_ref, *, alpha):
    m = mod_ref[0, 0]
    lo0, hi0 = _unpack_bf16_pairs(y0_ref[...])
    lo1, hi1 = _unpack_bf16_pairs(y1_ref[...])
    f = jnp.concatenate([lo0 + lo1, hi0 + hi1], axis=1)
    o_ref[0] = _layer_norm_rows(alpha * x_ref[0] + m[5:6] * f, lng_ref[...], lnb_ref[...])


def _moe_out(y, xs, modt, ln_g, ln_b, n_ctx_tiles, alpha):
    B, R, D = xs.shape
    tm = TOKEN_TILE
    rt = R // tm
    if n_ctx_tiles < 0:
        seg = lambda j: 1
    else:
        seg = lambda j: jnp.where(j >= n_ctx_tiles, 1, 0)
    tok = pl.BlockSpec((1, tm, D), lambda b, j: (b, j, 0))
    vec = pl.BlockSpec((1, D), lambda b, j: (0, 0))
    return pl.pallas_call(
        functools.partial(_moe_out_body, alpha=alpha),
        grid=(B, R // tm),
        in_specs=[pl.BlockSpec((tm, D // 2), lambda b, j: (b * rt + j, 0)),
                  pl.BlockSpec((tm, D // 2), lambda b, j: (B * rt + b * rt + j, 0)),
                  tok, pl.BlockSpec((1, 1, 6, D), lambda b, j: (b, seg(j), 0, 0)), vec, vec],
        out_specs=tok,
        out_shape=jax.ShapeDtypeStruct((B, R, D), F32),
        compiler_params=_cparams(("arbitrary", "arbitrary")),
        name="moe_out_ln",
    )(y, y, xs, modt, ln_g.reshape(1, D), ln_b.reshape(1, D))


def _hier_moe(h, ids, wts, xs, modt, w_gate, w_up, w_down, layer, ln_g, ln_b, n_ctx_tiles, alpha):
    B, R, D = xs.shape
    N = B * R
    plan = _moe_plan(ids.reshape(N, LANES)[:, :2], wts.reshape(N, LANES)[:, :2], N)
    y = _moe_experts(h.reshape(N, D // 2), *plan, w_gate, w_up, w_down, layer)
    return _moe_out(y, xs, modt, ln_g, ln_b, n_ctx_tiles, alpha)


GLA_PROJ_COL_TILE = 1024


def _gla_proj_body(x_ref, mod_ref, w_ref, w1_ref, w2_ref, b_ref, o_ref, g_ref, *, q_cols, q_scale):
    m = mod_ref[0, 0]
    z = (x_ref[0] * (1.0 + m[1:2]) + m[0:1]).astype(BF16)
    ct = GLA_PROJ_COL_TILE
    for c0 in range(0, o_ref.shape[2], ct):
        acc = jnp.dot(z, w_ref[:, c0:c0 + ct], preferred_element_type=F32)
        if c0 < q_cols:
            acc = acc * q_scale
        o_ref[0, :, c0:c0 + ct] = acc.astype(BF16)
    t = jnp.dot(z, w1_ref[...], preferred_element_type=F32)
    pre = jnp.dot(t.astype(BF16), w2_ref[...], preferred_element_type=F32) + b_ref[...]
    g_ref[0] = (jnp.minimum(pre, 0.0) - jnp.log(1.0 + jnp.exp(-jnp.abs(pre)))) * (1.0 / GLA_TAU)


def _gla_proj(xs, modt, w_bf16, w1, w2, gb, n_ctx_tiles, q_cols, q_scale):
    B, S, D = xs.shape
    NO = w_bf16.shape[1]
    rank = w1.shape[2]
    kd = w2.shape[2]
    tm = TOKEN_TILE
    assert q_cols % GLA_PROJ_COL_TILE == 0 and NO % GLA_PROJ_COL_TILE == 0
    w1c = jnp.zeros((D, LANES), F32).at[:, :rank].set(w1[0]).at[:, rank:2 * rank].set(w1[1]).astype(BF16)
    w2c = (jnp.zeros((LANES, 2 * kd), F32).at[:rank, :kd].set(w2[0]).at[rank:2 * rank, kd:].set(w2[1])
           .astype(BF16))
    seg = lambda j: jnp.where(j >= n_ctx_tiles, 1, 0)
    const = lambda shape: pl.BlockSpec(shape, lambda b, j: (0, 0))
    return pl.pallas_call(
        functools.partial(_gla_proj_body, q_cols=q_cols, q_scale=q_scale),
        grid=(B, S // tm),
        in_specs=[
            pl.BlockSpec((1, tm, D), lambda b, j: (b, j, 0)),
            pl.BlockSpec((1, 1, 6, D), lambda b, j: (b, seg(j), 0, 0)),
            pl.BlockSpec((D, NO), lambda b, j: (0, 0), pipeline_mode=pl.Buffered(1)),
            const((D, LANES)), const((LANES, 2 * kd)), const((1, 2 * kd)),
        ],
        out_specs=[pl.BlockSpec((1, tm, NO), lambda b, j: (b, j, 0)),
                   pl.BlockSpec((1, tm, 2 * kd), lambda b, j: (b, j, 0))],
        out_shape=[jax.ShapeDtypeStruct((B, S, NO), BF16), jax.ShapeDtypeStruct((B, S, 2 * kd), F32)],
        compiler_params=_cparams(("arbitrary", "arbitrary")),
        name="gla_proj",
    )(xs, modt, w_bf16, w1c, w2c, gb.reshape(1, 2 * kd))


GLA_LEVELS = (32, 16, 8, 4, 2, 1)


def _gla_constants():
    L = GLA_CHUNK
    t = np.arange(L)[:, None]
    u = np.arange(L)[None, :]
    mats, masks = [], []
    for bwd in (False, True):
        blocks = []
        blocks.append((u >= t) if bwd else (u <= t))
        blocks.append((u < t) if bwd else (u > t))
        q_lv, k_lv, m_lv = [], [], []
        for m in GLA_LEVELS:
            base = (t // (2 * m)) * (2 * m)
            ubase = (u // (2 * m)) * (2 * m)
            t_hi = (t % (2 * m)) >= m
            if not bwd:
                r = base + m - 1
                q_lv.append(t_hi & (u > r) & (u <= t))
                k_lv.append(~t_hi & (u > t) & (u <= r))
                m_lv.append(t_hi & ((u % (2 * m)) < m) & (base == ubase))
            else:
                r = base + m
                q_lv.append(~t_hi & (u >= t) & (u < r))
                k_lv.append(t_hi & (u >= r) & (u < t))
                m_lv.append(~t_hi & ((u % (2 * m)) >= m) & (base == ubase))
        blocks += [ql | kl for ql, kl in zip(q_lv, k_lv)]
        blocks.append(np.ones((8, L), bool))
        a = np.concatenate(blocks, axis=0).astype(np.float32)
        mats.append(np.concatenate([a, a, a], axis=1))
        m_lv.append(t == u)
        masks.append(np.concatenate(m_lv, axis=0).astype(np.float32))
    return np.stack(mats), np.stack(masks)


LOG2_E = 1.4426950408889634


def _gla_chunk_step(q, k, v, g, a_ref, mk_ref, st_ref, hd):
    L = GLA_CHUNK
    nlv = len(GLA_LEVELS)
    g = g * LOG2_E
    g_hi = g.astype(BF16)
    r1 = g - g_hi.astype(F32)
    g_mid = r1.astype(BF16)
    g_lo = (r1 - g_mid.astype(F32)).astype(BF16)
    gs = jnp.concatenate([g_hi, g_mid, g_lo], axis=0)
    ex = jnp.exp2(jnp.dot(a_ref[...], gs, preferred_element_type=F32))
    qf = q.astype(F32)
    kf = k.astype(F32)
    q_inter = (qf * ex[0:L]).astype(BF16)
    k_state = (kf * ex[L:2 * L]).astype(BF16)
    dec = ex[(2 + nlv) * L:(2 + nlv) * L + 1]
    nt = (((1,), (1,)), ((), ()))
    att = lax.dot_general(q, k, nt, preferred_element_type=F32) * mk_ref[nlv * L:(nlv + 1) * L]
    for lv in range(nlv):
        f = ex[(2 + lv) * L:(3 + lv) * L]
        ql = (qf * f).astype(BF16)
        kl = (kf * f).astype(BF16)
        att = att + lax.dot_general(ql, kl, nt, preferred_element_type=F32) * mk_ref[lv * L:(lv + 1) * L]
    st = st_ref[hd]
    o = jnp.dot(att.astype(BF16), v, preferred_element_type=F32)
    o = o + lax.dot_general(q_inter, st.astype(BF16), nt, preferred_element_type=F32)
    upd = lax.dot_general(v, k_state, (((0,), (0,)), ((), ())), preferred_element_type=F32)
    st_ref[hd] = st * dec + upd
    return o


def _gla_scan_body(qf_ref, kf_ref, vf_ref, gf_ref, qb_ref, kb_ref, vb_ref, gb_ref,
                   af_ref, ab_ref, mf_ref, mb_ref, of_ref, ob_ref, stf_ref, stb_ref):
    @pl.when(pl.program_id(1) == 0)
    def _():
        stf_ref[...] = jnp.zeros(stf_ref.shape, stf_ref.dtype)
        stb_ref[...] = jnp.zeros(stb_ref.shape, stb_ref.dtype)

    _, dv, dk = stf_ref.shape
    for hd in range(GLA_HEADS):
        ks = slice(hd * dk, (hd + 1) * dk)
        vs = slice(hd * dv, (hd + 1) * dv)
        o = _gla_chunk_step(qf_ref[0, :, ks], kf_ref[0, :, ks], vf_ref[0, :, vs], gf_ref[0, :, ks],
                            af_ref, mf_ref, stf_ref, hd)
        of_ref[0, :, vs] = o.astype(of_ref.dtype)
        o = _gla_chunk_step(qb_ref[0, :, ks], kb_ref[0, :, ks], vb_ref[0, :, vs], gb_ref[0, :, ks],
                            ab_ref, mb_ref, stb_ref, hd)
        ob_ref[0, :, vs] = o.astype(ob_ref.dtype)


def _gla_scan(proj, lg, C, T):
    B, S, NO = proj.shape
    kd = lg.shape[2] // 2
    D = NO - 2 * kd
    D = D // 2
    dk = kd // GLA_HEADS
    dv = D // GLA_HEADS
    L = GLA_CHUNK
    n_ctx = C // L
    n_all = S // L
    amat, masks = _gla_constants()
    amat = jnp.asarray(amat, BF16)
    masks = jnp.asarray(masks, F32)

    def fwd(c):
        return c

    def bwd(c):
        return jnp.where(c < n_ctx, n_ctx - 1 - c, n_ctx + n_all - 1 - c)

    assert 2 * kd == D

    def specs(chunk, d):
        return [
            pl.BlockSpec((1, L, kd), lambda b, c: (b, chunk(c), 0)),
            pl.BlockSpec((1, L, kd), lambda b, c: (b, chunk(c), 1)),
            pl.BlockSpec((1, L, D), lambda b, c: (b, chunk(c), 1)),
            pl.BlockSpec((1, L, kd), lambda b, c: (b, chunk(c), d)),
        ]

    const = lambda arr: pl.BlockSpec(arr.shape[1:], lambda b, c: (0, 0))
    out = jax.ShapeDtypeStruct((B, S, D), BF16)
    state = pltpu.VMEM((GLA_HEADS, dv, dk), F32)
    return pl.pallas_call(
        _gla_scan_body,
        grid=(B, n_all),
        in_specs=specs(fwd, 0) + specs(bwd, 1) + [const(amat), const(amat), const(masks), const(masks)],
        out_specs=[pl.BlockSpec((1, L, D), lambda b, c: (b, fwd(c), 0)),
                   pl.BlockSpec((1, L, D), lambda b, c: (b, bwd(c), 0))],
        out_shape=[out, out],
        scratch_shapes=[state, state],
        compiler_params=_cparams(("arbitrary", "arbitrary")),
        name="gla_scan",
    )(proj, proj, proj, lg, proj, proj, proj, lg, amat[0], amat[1], masks[0], masks[1])


def kernel(x, c, ctx, c_ctx, mod_w, mod_b, ln_g, ln_b, attn_w_qkv, attn_w_o, attn_sink,
           gla_w_in, gla_gate_w1, gla_gate_w2, gla_gate_b, gla_norm_g, gla_w_o,
           moe_group_w, moe_group_b, moe_router_w, moe_router_b, moe_w_gate, moe_w_up, moe_w_down):
    B, T, D = x.shape
    C = ctx.shape[1]
    depth = mod_w.shape[0]
    assert depth == 2 and D == N_HEADS * HEAD_DIM
    assert C % TOKEN_TILE == 0 and T % TOKEN_TILE == 0 and T % GRID_W == 0
    alpha = (2 * depth) ** 0.25
    n_ctx_tiles = C // TOKEN_TILE

    m_rows = -(-(B + 1) // 8) * 8
    cond = jnp.zeros((m_rows, D), F32).at[:B].set(c).at[B].set(c_ctx)
    mods = _adaln_mods(cond, mod_w, mod_b)

    def mod_table(l):
        lat = mods[l, :B].reshape(B, 1, 6, D)
        cm = jnp.broadcast_to(mods[l, B].reshape(1, 1, 6, D), (B, 1, 6, D))
        return jnp.concatenate([cm, lat], axis=1)

    modt = mod_table(0)
    cos, sin = _rope_tables(C, T)
    qkv = _qkv_proj(ctx, x, modt, attn_w_qkv[0].astype(BF16), cos, sin, n_ctx_tiles)
    attn = _attention(qkv, attn_sink[0], C, T)
    router = _router_params(moe_group_w[0], moe_group_b[0], moe_router_w[0], moe_router_b[0])
    xs, h, ids, wts = _attn_out(attn, ctx, x, modt, attn_w_o[0].astype(BF16), ln_g[0, 0], ln_b[0, 0], router,
                                n_ctx_tiles, alpha)
    xs = _hier_moe(h, ids, wts, xs, modt, moe_w_gate, moe_w_up, moe_w_down, 0, ln_g[0, 1], ln_b[0, 1],
                   n_ctx_tiles, alpha)

    modt = mod_table(1)
    kd = gla_gate_w2.shape[3]
    dk = kd // GLA_HEADS
    proj, lg = _gla_proj(xs, modt, gla_w_in[0].astype(BF16), gla_gate_w1[0], gla_gate_w2[0], gla_gate_b[0],
                         n_ctx_tiles, kd, dk ** -0.5)
    o2 = _gla_scan(proj, lg, C, T)
    router = _router_params(moe_group_w[1], moe_group_b[1], moe_router_w[1], moe_router_b[1])
    xl, h, ids, wts = _gla_out(o2, proj, gla_norm_g[0], xs, modt, gla_w_o[0].astype(BF16), ln_g[1, 0],
                               ln_b[1, 0], router, n_ctx_tiles, T, alpha)
    return _hier_moe(h, ids, wts, xl, modt, moe_w_gate, moe_w_up, moe_w_down, 1, ln_g[1, 1], ln_b[1, 1],
                     -1, alpha)
```

```python
import functools

import numpy as np
import jax
import jax.numpy as jnp
from jax import lax
from jax.experimental import pallas as pl
from jax.experimental.pallas import tpu as pltpu

F32 = jnp.float32
BF16 = jnp.bfloat16

N_HEADS = 16
N_KV_HEADS = 4
HEAD_DIM = 128
GROUP = N_HEADS // N_KV_HEADS
WINDOW = 128
GRID_W = 64
ROPE_BASE = 10000.0
GLA_HEADS = 4
GLA_TAU = 16.0
GLA_CHUNK = 64
MOE_GROUPS = 4
MOE_EXPERTS_PER_GROUP = 8
N_EXPERTS = MOE_GROUPS * MOE_EXPERTS_PER_GROUP
LN_EPS = 1e-5
RMS_EPS = 1e-6

V7X_VMEM_LIMIT_BYTES = 56 * 1024 * 1024
LANES = 128
TOKEN_TILE = 256
MOE_TILE = 256
MODS_COL_TILE = 1024


def _cparams(sem):
    return pltpu.CompilerParams(dimension_semantics=sem, vmem_limit_bytes=V7X_VMEM_LIMIT_BYTES)


def _silu(v):
    return v / (1.0 + jnp.exp(-v))


def _pack_bf16_pairs(v):
    n = v.shape[1] // 2
    lo = pltpu.bitcast(v[:, :n].astype(BF16).astype(F32), jnp.uint32)
    hi = pltpu.bitcast(v[:, n:].astype(BF16).astype(F32), jnp.uint32)
    return hi | (lo >> 16)


def _unpack_bf16_pairs(w):
    lo = pltpu.bitcast(w << 16, F32)
    hi = pltpu.bitcast(w & jnp.uint32(0xFFFF0000), F32)
    return lo, hi


SUBLANES = 8


def _store_token_tiles(ref, v):
    rows, width = v.shape
    n = width // LANES
    for s in range(n):
        ref[pl.ds(s, rows, stride=n), :] = v[:, s * LANES:(s + 1) * LANES]


def _load_token_tiles(ref, rows):
    n = ref.shape[0] // rows
    return jnp.concatenate([ref[pl.ds(s, rows, stride=n), :] for s in range(n)], axis=1)


def _mods_body(cond_ref, w_ref, b_ref, o_ref):
    a = _silu(cond_ref[...]).astype(BF16)
    o_ref[0] = jnp.dot(a, w_ref[0].astype(BF16), preferred_element_type=F32) + b_ref[0]


def _adaln_mods(cond, mod_w, mod_b):
    L, D, N6 = mod_w.shape
    M = cond.shape[0]
    tn = MODS_COL_TILE
    return pl.pallas_call(
        _mods_body,
        grid=(L, N6 // tn),
        in_specs=[
            pl.BlockSpec((M, D), lambda l, j: (0, 0)),
            pl.BlockSpec((1, D, tn), lambda l, j: (l, 0, j)),
            pl.BlockSpec((1, 1, tn), lambda l, j: (l, 0, j)),
        ],
        out_specs=pl.BlockSpec((1, M, tn), lambda l, j: (l, 0, j)),
        out_shape=jax.ShapeDtypeStruct((L, M, N6), F32),
        compiler_params=_cparams(("arbitrary", "arbitrary")),
        name="adaln_mods",
    )(cond, mod_w, mod_b.reshape(L, 1, N6))


def _stream_rows(c_ref, x_ref, n_ctx_tiles):
    return jnp.where(pl.program_id(1) < n_ctx_tiles, c_ref[0], x_ref[0])


def _stream_specs(tm, D, n_ctx_tiles):
    return [pl.BlockSpec((1, tm, D), lambda b, j: (b, jnp.minimum(j, n_ctx_tiles - 1), 0)),
            pl.BlockSpec((1, tm, D), lambda b, j: (b, jnp.maximum(j - n_ctx_tiles, 0), 0))]


def _qkv_body(c_ref, x_ref, mod_ref, w_ref, cos_ref, sin_ref, o_ref, *, n_ctx_tiles, n_rot_heads, n_q_heads,
              scale):
    m = mod_ref[0, 0]
    h = (_stream_rows(c_ref, x_ref, n_ctx_tiles) * (1.0 + m[1:2]) + m[0:1]).astype(BF16)
    acc = jnp.dot(h, w_ref[...], preferred_element_type=F32)
    cos = cos_ref[...]
    sin = sin_ref[...]
    lane = lax.broadcasted_iota(jnp.int32, cos.shape, 1)
    first = (lane & 32) == 0
    for hd in range(n_rot_heads):
        y = acc[:, hd * HEAD_DIM:(hd + 1) * HEAD_DIM]
        partner = jnp.where(first, pltpu.roll(y, HEAD_DIM - 32, 1), pltpu.roll(y, 32, 1))
        r = y * cos + partner * sin
        if hd < n_q_heads:
            r = r * scale
        o_ref[0, :, hd * HEAD_DIM:(hd + 1) * HEAD_DIM] = r.astype(BF16)
    rest = n_rot_heads * HEAD_DIM
    o_ref[0, :, rest:] = acc[:, rest:].astype(BF16)


def _rope_tables(C, T):
    half = HEAD_DIM // 2
    pos = np.arange(T)
    inv_freq = ROPE_BASE ** (-np.arange(0, half, 2, dtype=np.float32) / half)
    ang_r = (pos // GRID_W).astype(np.float32)[:, None] * inv_freq
    ang_c = (pos % GRID_W).astype(np.float32)[:, None] * inv_freq
    ang_r = jnp.asarray(ang_r, F32)
    ang_c = jnp.asarray(ang_c, F32)
    cos = jnp.concatenate([jnp.cos(ang_r)] * 2 + [jnp.cos(ang_c)] * 2, axis=-1)
    sin = jnp.concatenate([-jnp.sin(ang_r), jnp.sin(ang_r), -jnp.sin(ang_c), jnp.sin(ang_c)], axis=-1)
    cos = jnp.concatenate([jnp.ones((C, HEAD_DIM), F32), cos], axis=0)
    sin = jnp.concatenate([jnp.zeros((C, HEAD_DIM), F32), sin], axis=0)
    return cos, sin


def _qkv_proj(ctx, x, modt, w_bf16, cos, sin, n_ctx_tiles):
    B, T, D = x.shape
    S = ctx.shape[1] + T
    NO = w_bf16.shape[1]
    tm = TOKEN_TILE
    body = functools.partial(_qkv_body, n_ctx_tiles=n_ctx_tiles, n_rot_heads=N_HEADS + N_KV_HEADS,
                             n_q_heads=N_HEADS, scale=HEAD_DIM ** -0.5)
    seg = lambda j: jnp.where(j >= n_ctx_tiles, 1, 0)
    return pl.pallas_call(
        body,
        grid=(B, S // tm),
        in_specs=_stream_specs(tm, D, n_ctx_tiles) + [
            pl.BlockSpec((1, 1, 6, D), lambda b, j: (b, seg(j), 0, 0)),
            pl.BlockSpec((D, NO), lambda b, j: (0, 0)),
            pl.BlockSpec((tm, HEAD_DIM), lambda b, j: (j, 0)),
            pl.BlockSpec((tm, HEAD_DIM), lambda b, j: (j, 0)),
        ],
        out_specs=pl.BlockSpec((1, tm, NO), lambda b, j: (b, j, 0)),
        out_shape=jax.ShapeDtypeStruct((B, S, NO), BF16),
        compiler_params=_cparams(("arbitrary", "arbitrary")),
        name="qkv_rope",
    )(ctx, x, modt, w_bf16, cos, sin)


ATTN_KV_PER_STEP = 2


def _attn_one_head(sink_ref, head, q, kc, vc, k_win, v_win, n, n_lat_blk, latent):
    blk = WINDOW
    rows = GROUP * blk
    row = lax.broadcasted_iota(jnp.int32, (rows, 1), 0)
    sink = jnp.full((rows, 1), sink_ref[head * GROUP + GROUP - 1], F32)
    for g in range(GROUP - 2, -1, -1):
        sink = jnp.where(row < (g + 1) * blk, sink_ref[head * GROUP + g], sink)
    nt = (((1,), (1,)), ((), ()))
    s_c = lax.dot_general(q, kc, nt, preferred_element_type=F32)
    scores = [(s_c, vc)]
    if latent:
        a = lax.broadcasted_iota(jnp.int32, (rows, blk), 0) % blk
        key = lax.broadcasted_iota(jnp.int32, (rows, blk), 1)
        s0 = lax.dot_general(q, k_win[0], nt, preferred_element_type=F32)
        s0 = jnp.where((key >= a) & (n > 0), s0, -jnp.inf)
        s1 = lax.dot_general(q, k_win[1], nt, preferred_element_type=F32)
        s2 = lax.dot_general(q, k_win[2], nt, preferred_element_type=F32)
        s2 = jnp.where((key <= a) & (n < n_lat_blk - 1), s2, -jnp.inf)
        scores += [(s0, v_win[0]), (s1, v_win[1]), (s2, v_win[2])]
    lane_tiles = [s[:, j * LANES:(j + 1) * LANES] for s, _ in scores for j in range(s.shape[1] // LANES)]
    m_el = lane_tiles[0]
    for t in lane_tiles[1:]:
        m_el = jnp.maximum(m_el, t)
    m = jnp.maximum(jnp.max(m_el, axis=1, keepdims=True), sink)
    acc = jnp.zeros((rows, HEAD_DIM), F32)
    l_el = jnp.zeros((rows, LANES), F32)
    for s, v in scores:
        p = jnp.exp(s - m)
        for j in range(s.shape[1] // LANES):
            l_el = l_el + p[:, j * LANES:(j + 1) * LANES]
        acc = acc + jnp.dot(p.astype(BF16), v, preferred_element_type=F32)
    denom = jnp.exp(sink - m) + jnp.sum(l_el, axis=1, keepdims=True)
    return acc / denom


def _attn_body(sink_ref, q_ref, kc_ref, vc_ref, k0_ref, k1_ref, k2_ref, v0_ref, v1_ref, v2_ref, o_ref,
               *, n_ctx_blk, n_lat_blk):
    kh0 = pl.program_id(1) * ATTN_KV_PER_STEP
    qb = pl.program_id(2)
    blk = WINDOW
    gw = GROUP * HEAD_DIM

    def run(latent):
        n = qb - n_ctx_blk
        for i in range(ATTN_KV_PER_STEP):
            hs = slice(i * HEAD_DIM, (i + 1) * HEAD_DIM)
            q = jnp.concatenate([q_ref[0, :, i * gw + g * HEAD_DIM:i * gw + (g + 1) * HEAD_DIM]
                                 for g in range(GROUP)], axis=0)
            k_win = [r[0, :, hs] for r in (k0_ref, k1_ref, k2_ref)] if latent else None
            v_win = [r[0, :, hs] for r in (v0_ref, v1_ref, v2_ref)] if latent else None
            o = _attn_one_head(sink_ref, kh0 + i, q, kc_ref[0, :, hs], vc_ref[0, :, hs], k_win, v_win,
                               n, n_lat_blk, latent)
            for g in range(GROUP):
                o_ref[0, :, i * gw + g * HEAD_DIM:i * gw + (g + 1) * HEAD_DIM] = (
                    o[g * blk:(g + 1) * blk].astype(o_ref.dtype))

    @pl.when(qb < n_ctx_blk)
    def _():
        run(False)

    @pl.when(qb >= n_ctx_blk)
    def _():
        run(True)


def _attention(qkv, sink, C, T):
    B, S, _ = qkv.shape
    blk = WINDOW
    n_ctx_blk = C // blk
    n_lat_blk = T // blk
    kvs = ATTN_KV_PER_STEP
    kvw = kvs * HEAD_DIM
    k_col = N_HEADS // kvs
    v_col = (N_HEADS + N_KV_HEADS) // kvs

    def win(j, col):
        def im(b, kh, qb):
            n = jnp.maximum(qb - n_ctx_blk, 0)
            return (b, n_ctx_blk + jnp.clip(n + j - 1, 0, n_lat_blk - 1), col + kh)
        return im

    kv_blk = (1, blk, kvw)
    q_blk = (1, blk, kvs * GROUP * HEAD_DIM)
    body = functools.partial(_attn_body, n_ctx_blk=n_ctx_blk, n_lat_blk=n_lat_blk)
    return pl.pallas_call(
        body,
        grid=(B, N_KV_HEADS // kvs, S // blk),
        in_specs=[
            pl.BlockSpec(memory_space=pltpu.SMEM),
            pl.BlockSpec(q_blk, lambda b, kh, qb: (b, qb, kh)),
            pl.BlockSpec((1, C, kvw), lambda b, kh, qb: (b, 0, k_col + kh)),
            pl.BlockSpec((1, C, kvw), lambda b, kh, qb: (b, 0, v_col + kh)),
            pl.BlockSpec(kv_blk, win(0, k_col)), pl.BlockSpec(kv_blk, win(1, k_col)),
            pl.BlockSpec(kv_blk, win(2, k_col)),
            pl.BlockSpec(kv_blk, win(0, v_col)), pl.BlockSpec(kv_blk, win(1, v_col)),
            pl.BlockSpec(kv_blk, win(2, v_col)),
        ],
        out_specs=pl.BlockSpec(q_blk, lambda b, kh, qb: (b, qb, kh)),
        out_shape=jax.ShapeDtypeStruct((B, S, N_HEADS * HEAD_DIM), BF16),
        compiler_params=_cparams(("arbitrary", "arbitrary", "arbitrary")),
        name="window_attention",
    )(sink, qkv, qkv, qkv, qkv, qkv, qkv, qkv, qkv, qkv)


def _layer_norm_rows(r, g, b):
    mu = jnp.mean(r, axis=-1, keepdims=True)
    rc = r - mu
    var = jnp.mean(rc * rc, axis=-1, keepdims=True)
    return rc * lax.rsqrt(var + LN_EPS) * g + b


def _route(logits):
    lane = lax.broadcasted_iota(jnp.int32, logits.shape, 1)
    neg = -jnp.inf

    def first_max(vals):
        mx = jnp.max(vals, axis=1, keepdims=True)
        idx = jnp.min(jnp.where(vals == mx, lane, LANES), axis=1, keepdims=True)
        return mx, idx

    gmask = lane < MOE_GROUPS
    gl = jnp.where(gmask, logits, neg)
    gmax, gidx = first_max(gl)
    gsum = jnp.sum(jnp.where(gmask, jnp.exp(gl - gmax), 0.0), axis=1, keepdims=True)
    g_w = 1.0 / gsum
    lo = MOE_GROUPS + gidx * MOE_EXPERTS_PER_GROUP
    el = jnp.where((lane >= lo) & (lane < lo + MOE_EXPERTS_PER_GROUP), logits, neg)
    v1, i1 = first_max(el)
    v2, i2 = first_max(jnp.where(lane == i1, neg, el))
    e2 = jnp.exp(v2 - v1)
    w1 = g_w / (1.0 + e2)
    w2 = g_w * e2 / (1.0 + e2)
    ids = jnp.where(lane == 0, i1 - MOE_GROUPS, jnp.where(lane == 1, i2 - MOE_GROUPS, 0))
    wts = jnp.where(lane == 0, w1, jnp.where(lane == 1, w2, 0.0))
    return ids, wts


def _mix_out_body(*refs, alpha, gla, n_ctx_tiles):
    wr_ref, br_ref, xo_ref, h_ref, id_ref, wt_ref = refs[-6:]
    refs = refs[:-6]
    if gla:
        of_ref, ob_ref, r_ref, ng_ref, x_ref, mod_ref, w_ref, lng_ref, lnb_ref = refs
        x = x_ref[0]
        o = of_ref[0].astype(F32) + ob_ref[0].astype(F32)
        dv = ng_ref.shape[1]
        parts = []
        for hd in range(GLA_HEADS):
            oh = o[:, hd * dv:(hd + 1) * dv]
            ms = jnp.mean(oh * oh, axis=-1, keepdims=True)
            parts.append(oh * lax.rsqrt(ms + RMS_EPS) * ng_ref[...])
        a = (jnp.concatenate(parts, axis=1) * _silu(r_ref[0].astype(F32))).astype(BF16)
    else:
        a_ref, c_ref, x_ref, mod_ref, w_ref, lng_ref, lnb_ref = refs
        a = a_ref[0]
        x = _stream_rows(c_ref, x_ref, n_ctx_tiles)
    m = mod_ref[0, 0]
    y = jnp.dot(a, w_ref[...], preferred_element_type=F32)
    xn = _layer_norm_rows(alpha * x + m[2:3] * y, lng_ref[...], lnb_ref[...])
    xo_ref[0] = xn
    h = xn * (1.0 + m[4:5]) + m[3:4]
    _store_token_tiles(h_ref.at[0], _pack_bf16_pairs(h))
    ids, wts = _route(jnp.dot(h.astype(BF16), wr_ref[...], preferred_element_type=F32) + br_ref[...])
    id_ref[0] = ids
    wt_ref[0] = wts


def _router_params(wg, bg, we, be):
    D = wg.shape[0]
    pad = LANES - MOE_GROUPS - N_EXPERTS
    wr = jnp.concatenate([wg, we, jnp.zeros((D, pad), F32)], axis=1).astype(BF16)
    br = jnp.concatenate([bg, be, jnp.zeros((pad,), F32)]).reshape(1, LANES)
    return wr, br


def _mix_out_tail(B, R, D, tm):
    tok = pl.BlockSpec((1, tm, D), lambda b, j: (b, j, 0))
    wpt = D // 2 // LANES
    packed = pl.BlockSpec((1, tm * wpt, LANES), lambda b, j: (b, j, 0))
    lane_tile = pl.BlockSpec((1, tm, LANES), lambda b, j: (b, j, 0))
    in_specs = [pl.BlockSpec((D, LANES), lambda b, j: (0, 0)), pl.BlockSpec((1, LANES), lambda b, j: (0, 0))]
    out_specs = [tok, packed, lane_tile, lane_tile]
    out_shape = [jax.ShapeDtypeStruct((B, R, D), F32), jax.ShapeDtypeStruct((B, R * wpt, LANES), jnp.uint32),
                 jax.ShapeDtypeStruct((B, R, LANES), jnp.int32), jax.ShapeDtypeStruct((B, R, LANES), F32)]
    return in_specs, out_specs, out_shape


def _attn_out(attn, ctx, x, modt, w_bf16, ln_g, ln_b, router, n_ctx_tiles, alpha):
    B, S, D = attn.shape
    tm = TOKEN_TILE
    seg = lambda j: jnp.where(j >= n_ctx_tiles, 1, 0)
    tok = pl.BlockSpec((1, tm, D), lambda b, j: (b, j, 0))
    vec = pl.BlockSpec((1, D), lambda b, j: (0, 0))
    r_in, out_specs, out_shape = _mix_out_tail(B, S, D, tm)
    return pl.pallas_call(
        functools.partial(_mix_out_body, alpha=alpha, gla=False, n_ctx_tiles=n_ctx_tiles),
        grid=(B, S // tm),
        in_specs=[tok] + _stream_specs(tm, D, n_ctx_tiles) + [
                  pl.BlockSpec((1, 1, 6, D), lambda b, j: (b, seg(j), 0, 0)),
                  pl.BlockSpec((D, D), lambda b, j: (0, 0)), vec, vec] + r_in,
        out_specs=out_specs,
        out_shape=out_shape,
        compiler_params=_cparams(("arbitrary", "arbitrary")),
        name="attn_out_ln",
    )(attn, ctx, x, modt, w_bf16, ln_g.reshape(1, D), ln_b.reshape(1, D), *router)


def _gla_out(o2, proj, norm_g, xs, modt, w_bf16, ln_g, ln_b, router, n_ctx_tiles, T, alpha):
    B, S, D = xs.shape
    tm = TOKEN_TILE
    dv = D // GLA_HEADS
    r_col = proj.shape[2] // D - 1
    off = n_ctx_tiles
    vec = pl.BlockSpec((1, D), lambda b, j: (0, 0))
    r_in, out_specs, out_shape = _mix_out_tail(B, T, D, tm)
    return pl.pallas_call(
        functools.partial(_mix_out_body, alpha=alpha, gla=True, n_ctx_tiles=n_ctx_tiles),
        grid=(B, T // tm),
        in_specs=[
            pl.BlockSpec((1, tm, D), lambda b, j: (b, j + off, 0)),
            pl.BlockSpec((1, tm, D), lambda b, j: (b, j + off, 0)),
            pl.BlockSpec((1, tm, D), lambda b, j: (b, j + off, r_col)),
            pl.BlockSpec((1, dv), lambda b, j: (0, 0)),
            pl.BlockSpec((1, tm, D), lambda b, j: (b, j + off, 0)),
            pl.BlockSpec((1, 1, 6, D), lambda b, j: (b, 1, 0, 0)),
            pl.BlockSpec((D, D), lambda b, j: (0, 0)), vec, vec,
        ] + r_in,
        out_specs=out_specs,
        out_shape=out_shape,
        compiler_params=_cparams(("arbitrary", "arbitrary")),
        name="gla_out_ln",
    )(o2[0], o2[1], proj, norm_g.reshape(1, dv), xs, modt, w_bf16, ln_g.reshape(1, D), ln_b.reshape(1, D),
      *router)


def _moe_body(te_ref, nv_ref, nu_ref, tok_ref, tokn_ref, dst_ref, rw_ref, h_hbm, wg_ref, wu_ref, wd_ref,
              y_hbm, xbuf, ybuf, wgb, wub, wdb, gsem, ssem):
    i = pl.program_id(0)
    n_used = nu_ref[0]
    slot = i % 2

    unroll = 8
    tm = rw_ref.shape[0]
    wpt = xbuf.shape[1] // tm

    def token_rows(ref, first_word_row):
        return ref.at[pl.ds(pl.multiple_of(first_word_row, SUBLANES), wpt), :]

    def gather(idx_ref, n_rows, s):
        def issue(r8, c):
            for u in range(unroll):
                r = r8 * unroll + u
                pltpu.make_async_copy(token_rows(h_hbm, idx_ref[0, 0, r]),
                                      token_rows(xbuf.at[s], r * wpt), gsem.at[s]).start()
            return c
        lax.fori_loop(0, n_rows // unroll, issue, 0)

    def gather_wait(n_rows, s):
        n = pl.multiple_of(n_rows * wpt, SUBLANES)
        pltpu.make_async_copy(h_hbm.at[pl.ds(0, n), :], xbuf.at[s, pl.ds(0, n), :], gsem.at[s]).wait()

    def scatter(n_rows, s):
        def issue(r8, c):
            for u in range(unroll):
                r = r8 * unroll + u
                pltpu.make_async_copy(token_rows(ybuf.at[s], r * wpt),
                                      token_rows(y_hbm, dst_ref[0, 0, r]), ssem.at[s]).start()
            return c
        lax.fori_loop(0, n_rows // unroll, issue, 0)

    def scatter_wait(n_rows, s):
        n = pl.multiple_of(n_rows * wpt, SUBLANES)
        pltpu.make_async_copy(ybuf.at[s, pl.ds(0, n), :], y_hbm.at[pl.ds(0, n), :], ssem.at[s]).wait()

    @pl.when(i == 0)
    def _():
        xbuf[...] = jnp.zeros(xbuf.shape, xbuf.dtype)
        spare = y_hbm.shape[0] - 2 * tm * wpt
        for s in range(2):
            cp = pltpu.make_async_copy(xbuf.at[s], y_hbm.at[pl.ds(spare + s * tm * wpt, tm * wpt), :],
                                       ssem.at[s])
            cp.start()
            cp.wait()
        gather(tok_ref, nv_ref[0], 0)

    @pl.when(i + 1 < n_used)
    def _():
        gather(tokn_ref, nv_ref[i + 1], 1 - slot)

    @pl.when(i < n_used)
    def _():
        nv = nv_ref[i]
        gather_wait(nv, slot)

        @pl.when((i == 0) | (te_ref[i] != te_ref[jnp.maximum(i - 1, 0)]))
        def _():
            wgb[...] = wg_ref[0, 0].astype(BF16)
            wub[...] = wu_ref[0, 0].astype(BF16)
            wdb[...] = wd_ref[0, 0].astype(BF16)

        @pl.when(i >= 2)
        def _():
            scatter_wait(nv_ref[jnp.maximum(i - 2, 0)], slot)

        x = jnp.concatenate(_unpack_bf16_pairs(_load_token_tiles(xbuf.at[slot], tm)), axis=1).astype(BF16)
        g = jnp.dot(x, wgb[...], preferred_element_type=F32)
        u = jnp.dot(x, wub[...], preferred_element_type=F32)
        hid = (_silu(g) * u).astype(BF16)
        y = jnp.dot(hid, wdb[...], preferred_element_type=F32)
        _store_token_tiles(ybuf.at[slot], _pack_bf16_pairs(y * rw_ref[:, 0:1]))
        scatter(nv, slot)

        @pl.when(i == n_used - 1)
        def _():
            @pl.when(i >= 1)
            def _():
                scatter_wait(nv_ref[jnp.maximum(i - 1, 0)], 1 - slot)
            scatter_wait(nv, slot)


def _moe_experts(h, tile_expert, tile_valid, n_used, row_tok, row_dst, row_w, w_gate, w_up, w_down, layer):
    _, E, D, Hd = w_gate.shape
    wpt = D // 2 // LANES
    assert wpt % SUBLANES == 0
    N = h.shape[0] // wpt
    tm = MOE_TILE
    n_tiles = tile_expert.shape[0]
    idx_blk = lambda f: pl.BlockSpec((1, 1, tm), f, memory_space=pltpu.SMEM)
    grid_spec = pltpu.PrefetchScalarGridSpec(
        num_scalar_prefetch=3,
        grid=(n_tiles,),
        in_specs=[
            idx_blk(lambda i, te, nv, nu: (i, 0, 0)),
            idx_blk(lambda i, te, nv, nu: (jnp.minimum(i + 1, n_tiles - 1), 0, 0)),
            idx_blk(lambda i, te, nv, nu: (i, 0, 0)),
            pl.BlockSpec((tm, LANES), lambda i, te, nv, nu: (i, 0)),
            pl.BlockSpec(memory_space=pl.ANY),
            pl.BlockSpec((1, 1, D, Hd), lambda i, te, nv, nu: (layer, te[i], 0, 0)),
            pl.BlockSpec((1, 1, D, Hd), lambda i, te, nv, nu: (layer, te[i], 0, 0)),
            pl.BlockSpec((1, 1, Hd, D), lambda i, te, nv, nu: (layer, te[i], 0, 0)),
        ],
        out_specs=pl.BlockSpec(memory_space=pl.ANY),
        scratch_shapes=[
            pltpu.VMEM((2, tm * wpt, LANES), jnp.uint32), pltpu.VMEM((2, tm * wpt, LANES), jnp.uint32),
            pltpu.VMEM((D, Hd), BF16), pltpu.VMEM((D, Hd), BF16), pltpu.VMEM((Hd, D), BF16),
            pltpu.SemaphoreType.DMA((2,)), pltpu.SemaphoreType.DMA((2,)),
        ],
    )
    return pl.pallas_call(
        _moe_body,
        grid_spec=grid_spec,
        out_shape=jax.ShapeDtypeStruct(((2 * N + 2 * tm) * wpt, LANES), jnp.uint32),
        compiler_params=_cparams(("arbitrary",)),
        name="moe_experts",
    )(tile_expert, tile_valid, n_used, row_tok, row_tok, row_dst, row_w, h, w_gate, w_up, w_down)


def _moe_plan(ids, wts, N, wpt):
    tm = MOE_TILE
    E = N_EXPERTS
    A = 2 * N
    n_tiles = A // tm + E
    e_flat = ids.reshape(A)
    counts = jnp.sum((e_flat[:, None] == jnp.arange(E, dtype=jnp.int32)[None, :]).astype(jnp.int32), axis=0)
    fill = (-counts) % tm
    fj = jnp.arange(tm, dtype=jnp.int32)[None, :]
    fe = jnp.arange(E, dtype=jnp.int32)[:, None]
    fkey = jnp.where(fj < fill[:, None], fe, E).reshape(E * tm)
    n_fill = E * tm
    shift = 20
    assert A + n_fill < (1 << shift) and (E + 1) << shift < (1 << 31)
    idx = jnp.arange(A + n_fill, dtype=jnp.int32)
    keys = (jnp.concatenate([e_flat, fkey]) << shift) | idx
    wgt = jnp.concatenate([wts.reshape(A), jnp.zeros((n_fill,), F32)])
    key_s, w_s = lax.sort((keys, wgt), num_keys=1)
    e_s = (key_s >> shift).reshape(n_tiles, tm)
    a_s = (key_s & ((1 << shift) - 1)).reshape(n_tiles, tm)
    real = a_s < A
    n_used = jnp.sum(counts + fill) // tm
    t_idx = jnp.arange(n_tiles, dtype=jnp.int32)
    te = jnp.minimum(e_s[:, 0], E - 1)
    last_e = jnp.max(jnp.where(t_idx < n_used, te, 0))
    te = jnp.where(t_idx < n_used, te, last_e).astype(jnp.int32)
    tile_valid = jnp.sum(real.astype(jnp.int32), axis=1)
    tile_rows = jnp.where(t_idx < n_used, (tile_valid + 7) // 8 * 8, 0).astype(jnp.int32)
    spare = A + (t_idx[:, None] % 2) * tm + fj
    row_tok = (jnp.where(real, a_s >> 1, 0) * wpt).astype(jnp.int32)
    row_dst = (jnp.where(real, (a_s & 1) * N + (a_s >> 1), spare) * wpt).astype(jnp.int32)
    row_w = jnp.broadcast_to(w_s[:, None], (n_tiles * tm, LANES))
    return (te, tile_rows, n_used.reshape(1).astype(jnp.int32),
            row_tok.reshape(n_tiles, 1, tm), row_dst.reshape(n_tiles, 1, tm), row_w)


def _moe_out_body(y0_ref, y1_ref, x_ref, mod_ref, lng_ref, lnb_ref, o_ref, *, alpha):
    m = mod_ref[0, 0]
    rows = x_ref.shape[1]
    lo0, hi0 = _unpack_bf16_pairs(_load_token_tiles(y0_ref, rows))
    lo1, hi1 = _unpack_bf16_pairs(_load_token_tiles(y1_ref, rows))
    f = jnp.concatenate([lo0 + lo1, hi0 + hi1], axis=1)
    o_ref[0] = _layer_norm_rows(alpha * x_ref[0] + m[5:6] * f, lng_ref[...], lnb_ref[...])


def _moe_out(y, xs, modt, ln_g, ln_b, n_ctx_tiles, alpha):
    B, R, D = xs.shape
    tm = TOKEN_TILE
    rt = R // tm
    wpt = D // 2 // LANES
    if n_ctx_tiles < 0:
        seg = lambda j: 1
    else:
        seg = lambda j: jnp.where(j >= n_ctx_tiles, 1, 0)
    tok = pl.BlockSpec((1, tm, D), lambda b, j: (b, j, 0))
    vec = pl.BlockSpec((1, D), lambda b, j: (0, 0))
    return pl.pallas_call(
        functools.partial(_moe_out_body, alpha=alpha),
        grid=(B, R // tm),
        in_specs=[pl.BlockSpec((tm * wpt, LANES), lambda b, j: (b * rt + j, 0)),
                  pl.BlockSpec((tm * wpt, LANES), lambda b, j: (B * rt + b * rt + j, 0)),
                  tok, pl.BlockSpec((1, 1, 6, D), lambda b, j: (b, seg(j), 0, 0)), vec, vec],
        out_specs=tok,
        out_shape=jax.ShapeDtypeStruct((B, R, D), F32),
        compiler_params=_cparams(("arbitrary", "arbitrary")),
        name="moe_out_ln",
    )(y, y, xs, modt, ln_g.reshape(1, D), ln_b.reshape(1, D))


def _hier_moe(h, ids, wts, xs, modt, w_gate, w_up, w_down, layer, ln_g, ln_b, n_ctx_tiles, alpha):
    B, R, D = xs.shape
    N = B * R
    wpt = D // 2 // LANES
    plan = _moe_plan(ids.reshape(N, LANES)[:, :2], wts.reshape(N, LANES)[:, :2], N, wpt)
    y = _moe_experts(h.reshape(N * wpt, LANES), *plan, w_gate, w_up, w_down, layer)
    return _moe_out(y, xs, modt, ln_g, ln_b, n_ctx_tiles, alpha)


GLA_PROJ_COL_TILE = 1024


def _gla_proj_body(x_ref, mod_ref, w_ref, w1_ref, w2_ref, b_ref, o_ref, g_ref, *, q_cols, q_scale):
    m = mod_ref[0, 0]
    z = (x_ref[0] * (1.0 + m[1:2]) + m[0:1]).astype(BF16)
    ct = GLA_PROJ_COL_TILE
    for c0 in range(0, o_ref.shape[2], ct):
        acc = jnp.dot(z, w_ref[:, c0:c0 + ct], preferred_element_type=F32)
        if c0 < q_cols:
            acc = acc * q_scale
        o_ref[0, :, c0:c0 + ct] = acc.astype(BF16)
    t = jnp.dot(z, w1_ref[...], preferred_element_type=F32)
    pre = jnp.dot(t.astype(BF16), w2_ref[...], preferred_element_type=F32) + b_ref[...]
    g_ref[0] = (jnp.minimum(pre, 0.0) - jnp.log(1.0 + jnp.exp(-jnp.abs(pre)))) * (1.0 / GLA_TAU)


def _gla_proj(xs, modt, w_bf16, w1, w2, gb, n_ctx_tiles, q_cols, q_scale):
    B, S, D = xs.shape
    NO = w_bf16.shape[1]
    rank = w1.shape[2]
    kd = w2.shape[2]
    tm = TOKEN_TILE
    assert q_cols % GLA_PROJ_COL_TILE == 0 and NO % GLA_PROJ_COL_TILE == 0
    w1c = jnp.zeros((D, LANES), F32).at[:, :rank].set(w1[0]).at[:, rank:2 * rank].set(w1[1]).astype(BF16)
    w2c = (jnp.zeros((LANES, 2 * kd), F32).at[:rank, :kd].set(w2[0]).at[rank:2 * rank, kd:].set(w2[1])
           .astype(BF16))
    seg = lambda j: jnp.where(j >= n_ctx_tiles, 1, 0)
    const = lambda shape: pl.BlockSpec(shape, lambda b, j: (0, 0))
    return pl.pallas_call(
        functools.partial(_gla_proj_body, q_cols=q_cols, q_scale=q_scale),
        grid=(B, S // tm),
        in_specs=[
            pl.BlockSpec((1, tm, D), lambda b, j: (b, j, 0)),
            pl.BlockSpec((1, 1, 6, D), lambda b, j: (b, seg(j), 0, 0)),
            pl.BlockSpec((D, NO), lambda b, j: (0, 0), pipeline_mode=pl.Buffered(1)),
            const((D, LANES)), const((LANES, 2 * kd)), const((1, 2 * kd)),
        ],
        out_specs=[pl.BlockSpec((1, tm, NO), lambda b, j: (b, j, 0)),
                   pl.BlockSpec((1, tm, 2 * kd), lambda b, j: (b, j, 0))],
        out_shape=[jax.ShapeDtypeStruct((B, S, NO), BF16), jax.ShapeDtypeStruct((B, S, 2 * kd), F32)],
        compiler_params=_cparams(("arbitrary", "arbitrary")),
        name="gla_proj",
    )(xs, modt, w_bf16, w1c, w2c, gb.reshape(1, 2 * kd))


GLA_LEVELS = (32, 16, 8, 4, 2, 1)


def _gla_constants():
    L = GLA_CHUNK
    t = np.arange(L)[:, None]
    u = np.arange(L)[None, :]
    mats, masks = [], []
    for bwd in (False, True):
        blocks = []
        blocks.append((u >= t) if bwd else (u <= t))
        blocks.append((u < t) if bwd else (u > t))
        q_lv, k_lv, m_lv = [], [], []
        for m in GLA_LEVELS:
            base = (t // (2 * m)) * (2 * m)
            ubase = (u // (2 * m)) * (2 * m)
            t_hi = (t % (2 * m)) >= m
            if not bwd:
                r = base + m - 1
                q_lv.append(t_hi & (u > r) & (u <= t))
                k_lv.append(~t_hi & (u > t) & (u <= r))
                m_lv.append(t_hi & ((u % (2 * m)) < m) & (base == ubase))
            else:
                r = base + m
                q_lv.append(~t_hi & (u >= t) & (u < r))
                k_lv.append(t_hi & (u >= r) & (u < t))
                m_lv.append(~t_hi & ((u % (2 * m)) >= m) & (base == ubase))
        blocks += [ql | kl for ql, kl in zip(q_lv, k_lv)]
        blocks.append(np.ones((8, L), bool))
        a = np.concatenate(blocks, axis=0).astype(np.float32)
        mats.append(np.concatenate([a, a, a], axis=1))
        m_lv.append(t == u)
        masks.append(np.concatenate(m_lv, axis=0).astype(np.float32))
    return np.stack(mats), np.stack(masks)


LOG2_E = 1.4426950408889634


def _gla_chunk_step(q, k, v, g, a_ref, mk_ref, st_ref, hd):
    L = GLA_CHUNK
    nlv = len(GLA_LEVELS)
    g = g * LOG2_E
    g_hi = g.astype(BF16)
    r1 = g - g_hi.astype(F32)
    g_mid = r1.astype(BF16)
    g_lo = (r1 - g_mid.astype(F32)).astype(BF16)
    gs = jnp.concatenate([g_hi, g_mid, g_lo], axis=0)
    ex = jnp.exp2(jnp.dot(a_ref[...], gs, preferred_element_type=F32))
    qf = q.astype(F32)
    kf = k.astype(F32)
    q_inter = (qf * ex[0:L]).astype(BF16)
    k_state = (kf * ex[L:2 * L]).astype(BF16)
    dec = ex[(2 + nlv) * L:(2 + nlv) * L + 1]
    nt = (((1,), (1,)), ((), ()))
    att = lax.dot_general(q, k, nt, preferred_element_type=F32) * mk_ref[nlv * L:(nlv + 1) * L]
    for lv in range(nlv):
        f = ex[(2 + lv) * L:(3 + lv) * L]
        ql = (qf * f).astype(BF16)
        kl = (kf * f).astype(BF16)
        att = att + lax.dot_general(ql, kl, nt, preferred_element_type=F32) * mk_ref[lv * L:(lv + 1) * L]
    st = st_ref[hd]
    o = jnp.dot(att.astype(BF16), v, preferred_element_type=F32)
    o = o + lax.dot_general(q_inter, st.astype(BF16), nt, preferred_element_type=F32)
    upd = lax.dot_general(v, k_state, (((0,), (0,)), ((), ())), preferred_element_type=F32)
    st_ref[hd] = st * dec + upd
    return o


def _gla_scan_body(qf_ref, kf_ref, vf_ref, gf_ref, qb_ref, kb_ref, vb_ref, gb_ref,
                   af_ref, ab_ref, mf_ref, mb_ref, of_ref, ob_ref, stf_ref, stb_ref):
    @pl.when(pl.program_id(1) == 0)
    def _():
        stf_ref[...] = jnp.zeros(stf_ref.shape, stf_ref.dtype)
        stb_ref[...] = jnp.zeros(stb_ref.shape, stb_ref.dtype)

    _, dv, dk = stf_ref.shape
    for hd in range(GLA_HEADS):
        ks = slice(hd * dk, (hd + 1) * dk)
        vs = slice(hd * dv, (hd + 1) * dv)
        o = _gla_chunk_step(qf_ref[0, :, ks], kf_ref[0, :, ks], vf_ref[0, :, vs], gf_ref[0, :, ks],
                            af_ref, mf_ref, stf_ref, hd)
        of_ref[0, :, vs] = o.astype(of_ref.dtype)
        o = _gla_chunk_step(qb_ref[0, :, ks], kb_ref[0, :, ks], vb_ref[0, :, vs], gb_ref[0, :, ks],
                            ab_ref, mb_ref, stb_ref, hd)
        ob_ref[0, :, vs] = o.astype(ob_ref.dtype)


def _gla_scan(proj, lg, C, T):
    B, S, NO = proj.shape
    kd = lg.shape[2] // 2
    D = NO - 2 * kd
    D = D // 2
    dk = kd // GLA_HEADS
    dv = D // GLA_HEADS
    L = GLA_CHUNK
    n_ctx = C // L
    n_all = S // L
    amat, masks = _gla_constants()
    amat = jnp.asarray(amat, BF16)
    masks = jnp.asarray(masks, F32)

    def fwd(c):
        return c

    def bwd(c):
        return jnp.where(c < n_ctx, n_ctx - 1 - c, n_ctx + n_all - 1 - c)

    assert 2 * kd == D

    def specs(chunk, d):
        return [
            pl.BlockSpec((1, L, kd), lambda b, c: (b, chunk(c), 0)),
            pl.BlockSpec((1, L, kd), lambda b, c: (b, chunk(c), 1)),
            pl.BlockSpec((1, L, D), lambda b, c: (b, chunk(c), 1)),
            pl.BlockSpec((1, L, kd), lambda b, c: (b, chunk(c), d)),
        ]

    const = lambda arr: pl.BlockSpec(arr.shape[1:], lambda b, c: (0, 0))
    out = jax.ShapeDtypeStruct((B, S, D), BF16)
    state = pltpu.VMEM((GLA_HEADS, dv, dk), F32)
    return pl.pallas_call(
        _gla_scan_body,
        grid=(B, n_all),
        in_specs=specs(fwd, 0) + specs(bwd, 1) + [const(amat), const(amat), const(masks), const(masks)],
        out_specs=[pl.BlockSpec((1, L, D), lambda b, c: (b, fwd(c), 0)),
                   pl.BlockSpec((1, L, D), lambda b, c: (b, bwd(c), 0))],
        out_shape=[out, out],
        scratch_shapes=[state, state],
        compiler_params=_cparams(("arbitrary", "arbitrary")),
        name="gla_scan",
    )(proj, proj, proj, lg, proj, proj, proj, lg, amat[0], amat[1], masks[0], masks[1])


def kernel(x, c, ctx, c_ctx, mod_w, mod_b, ln_g, ln_b, attn_w_qkv, attn_w_o, attn_sink,
           gla_w_in, gla_gate_w1, gla_gate_w2, gla_gate_b, gla_norm_g, gla_w_o,
           moe_group_w, moe_group_b, moe_router_w, moe_router_b, moe_w_gate, moe_w_up, moe_w_down):
    B, T, D = x.shape
    C = ctx.shape[1]
    depth = mod_w.shape[0]
    assert depth == 2 and D == N_HEADS * HEAD_DIM
    assert C % TOKEN_TILE == 0 and T % TOKEN_TILE == 0 and T % GRID_W == 0
    alpha = (2 * depth) ** 0.25
    n_ctx_tiles = C // TOKEN_TILE

    m_rows = -(-(B + 1) // 8) * 8
    cond = jnp.zeros((m_rows, D), F32).at[:B].set(c).at[B].set(c_ctx)
    mods = _adaln_mods(cond, mod_w, mod_b)

    def mod_table(l):
        lat = mods[l, :B].reshape(B, 1, 6, D)
        cm = jnp.broadcast_to(mods[l, B].reshape(1, 1, 6, D), (B, 1, 6, D))
        return jnp.concatenate([cm, lat], axis=1)

    modt = mod_table(0)
    cos, sin = _rope_tables(C, T)
    qkv = _qkv_proj(ctx, x, modt, attn_w_qkv[0].astype(BF16), cos, sin, n_ctx_tiles)
    attn = _attention(qkv, attn_sink[0], C, T)
    router = _router_params(moe_group_w[0], moe_group_b[0], moe_router_w[0], moe_router_b[0])
    xs, h, ids, wts = _attn_out(attn, ctx, x, modt, attn_w_o[0].astype(BF16), ln_g[0, 0], ln_b[0, 0], router,
                                n_ctx_tiles, alpha)
    xs = _hier_moe(h, ids, wts, xs, modt, moe_w_gate, moe_w_up, moe_w_down, 0, ln_g[0, 1], ln_b[0, 1],
                   n_ctx_tiles, alpha)

    modt = mod_table(1)
    kd = gla_gate_w2.shape[3]
    dk = kd // GLA_HEADS
    proj, lg = _gla_proj(xs, modt, gla_w_in[0].astype(BF16), gla_gate_w1[0], gla_gate_w2[0], gla_gate_b[0],
                         n_ctx_tiles, kd, dk ** -0.5)
    o2 = _gla_scan(proj, lg, C, T)
    router = _router_params(moe_group_w[1], moe_group_b[1], moe_router_w[1], moe_router_b[1])
    xl, h, ids, wts = _gla_out(o2, proj, gla_norm_g[0], xs, modt, gla_w_o[0].astype(BF16), ln_g[1, 0],
                               ln_b[1, 0], router, n_ctx_tiles, T, alpha)
    return _hier_moe(h, ids, wts, xl, modt, moe_w_gate, moe_w_up, moe_w_down, 1, ln_g[1, 1], ln_b[1, 1],
                     -1, alpha)
```

```python
import functools

import numpy as np
import jax
import jax.numpy as jnp
from jax import lax
from jax.experimental import pallas as pl
from jax.experimental.pallas import tpu as pltpu

F32 = jnp.float32
BF16 = jnp.bfloat16

N_HEADS = 16
N_KV_HEADS = 4
HEAD_DIM = 128
GROUP = N_HEADS // N_KV_HEADS
WINDOW = 128
GRID_W = 64
ROPE_BASE = 10000.0
GLA_HEADS = 4
GLA_TAU = 16.0
GLA_CHUNK = 64
MOE_GROUPS = 4
MOE_EXPERTS_PER_GROUP = 8
N_EXPERTS = MOE_GROUPS * MOE_EXPERTS_PER_GROUP
LN_EPS = 1e-5
RMS_EPS = 1e-6

V7X_VMEM_LIMIT_BYTES = 56 * 1024 * 1024
LANES = 128
TOKEN_TILE = 256
MOE_TILE = 256
MODS_COL_TILE = 1024


def _cparams(sem):
    return pltpu.CompilerParams(dimension_semantics=sem, vmem_limit_bytes=V7X_VMEM_LIMIT_BYTES)


def _silu(v):
    return v / (1.0 + jnp.exp(-v))


def _pack_bf16_pairs(v):
    n = v.shape[1] // 2
    lo = pltpu.bitcast(v[:, :n].astype(BF16).astype(F32), jnp.uint32)
    hi = pltpu.bitcast(v[:, n:].astype(BF16).astype(F32), jnp.uint32)
    return hi | (lo >> 16)


def _unpack_bf16_pairs(w):
    lo = pltpu.bitcast(w << 16, F32)
    hi = pltpu.bitcast(w & jnp.uint32(0xFFFF0000), F32)
    return lo, hi


SUBLANES = 8


def _store_token_tiles(ref, v):
    rows, width = v.shape
    n = width // LANES
    for s in range(n):
        ref[pl.ds(s, rows, stride=n), :] = v[:, s * LANES:(s + 1) * LANES]


def _load_token_tiles(ref, rows):
    n = ref.shape[0] // rows
    return jnp.concatenate([ref[pl.ds(s, rows, stride=n), :] for s in range(n)], axis=1)


def _mods_body(cond_ref, w_ref, b_ref, o_ref):
    a = _silu(cond_ref[...]).astype(BF16)
    o_ref[0] = jnp.dot(a, w_ref[0].astype(BF16), preferred_element_type=F32) + b_ref[0]


def _adaln_mods(cond, mod_w, mod_b):
    L, D, N6 = mod_w.shape
    M = cond.shape[0]
    tn = MODS_COL_TILE
    return pl.pallas_call(
        _mods_body,
        grid=(L, N6 // tn),
        in_specs=[
            pl.BlockSpec((M, D), lambda l, j: (0, 0)),
            pl.BlockSpec((1, D, tn), lambda l, j: (l, 0, j)),
            pl.BlockSpec((1, 1, tn), lambda l, j: (l, 0, j)),
        ],
        out_specs=pl.BlockSpec((1, M, tn), lambda l, j: (l, 0, j)),
        out_shape=jax.ShapeDtypeStruct((L, M, N6), F32),
        compiler_params=_cparams(("arbitrary", "arbitrary")),
        name="adaln_mods",
    )(cond, mod_w, mod_b.reshape(L, 1, N6))


def _stream_rows(c_ref, x_ref, n_ctx_tiles):
    return jnp.where(pl.program_id(1) < n_ctx_tiles, c_ref[0], x_ref[0])


def _stream_specs(tm, D, n_ctx_tiles):
    return [pl.BlockSpec((1, tm, D), lambda b, j: (b, jnp.minimum(j, n_ctx_tiles - 1), 0)),
            pl.BlockSpec((1, tm, D), lambda b, j: (b, jnp.maximum(j - n_ctx_tiles, 0), 0))]


def _qkv_body(c_ref, x_ref, mod_ref, w_ref, cos_ref, sin_ref, o_ref, *, n_ctx_tiles, n_rot_heads, n_q_heads,
              scale):
    m = mod_ref[0, 0]
    h = (_stream_rows(c_ref, x_ref, n_ctx_tiles) * (1.0 + m[1:2]) + m[0:1]).astype(BF16)
    acc = jnp.dot(h, w_ref[...], preferred_element_type=F32)
    cos = cos_ref[...]
    sin = sin_ref[...]
    lane = lax.broadcasted_iota(jnp.int32, cos.shape, 1)
    first = (lane & 32) == 0
    for hd in range(n_rot_heads):
        y = acc[:, hd * HEAD_DIM:(hd + 1) * HEAD_DIM]
        partner = jnp.where(first, pltpu.roll(y, HEAD_DIM - 32, 1), pltpu.roll(y, 32, 1))
        r = y * cos + partner * sin
        if hd < n_q_heads:
            r = r * scale
        o_ref[0, :, hd * HEAD_DIM:(hd + 1) * HEAD_DIM] = r.astype(BF16)
    rest = n_rot_heads * HEAD_DIM
    o_ref[0, :, rest:] = acc[:, rest:].astype(BF16)


def _rope_tables(C, T):
    half = HEAD_DIM // 2
    pos = np.arange(T)
    inv_freq = ROPE_BASE ** (-np.arange(0, half, 2, dtype=np.float32) / half)
    ang_r = (pos // GRID_W).astype(np.float32)[:, None] * inv_freq
    ang_c = (pos % GRID_W).astype(np.float32)[:, None] * inv_freq
    ang_r = jnp.asarray(ang_r, F32)
    ang_c = jnp.asarray(ang_c, F32)
    cos = jnp.concatenate([jnp.cos(ang_r)] * 2 + [jnp.cos(ang_c)] * 2, axis=-1)
    sin = jnp.concatenate([-jnp.sin(ang_r), jnp.sin(ang_r), -jnp.sin(ang_c), jnp.sin(ang_c)], axis=-1)
    cos = jnp.concatenate([jnp.ones((C, HEAD_DIM), F32), cos], axis=0)
    sin = jnp.concatenate([jnp.zeros((C, HEAD_DIM), F32), sin], axis=0)
    return cos, sin


def _qkv_proj(ctx, x, modt, w_bf16, cos, sin, n_ctx_tiles):
    B, T, D = x.shape
    S = ctx.shape[1] + T
    NO = w_bf16.shape[1]
    tm = TOKEN_TILE
    body = functools.partial(_qkv_body, n_ctx_tiles=n_ctx_tiles, n_rot_heads=N_HEADS + N_KV_HEADS,
                             n_q_heads=N_HEADS, scale=HEAD_DIM ** -0.5)
    seg = lambda j: jnp.where(j >= n_ctx_tiles, 1, 0)
    return pl.pallas_call(
        body,
        grid=(B, S // tm),
        in_specs=_stream_specs(tm, D, n_ctx_tiles) + [
            pl.BlockSpec((1, 1, 6, D), lambda b, j: (b, seg(j), 0, 0)),
            pl.BlockSpec((D, NO), lambda b, j: (0, 0)),
            pl.BlockSpec((tm, HEAD_DIM), lambda b, j: (j, 0)),
            pl.BlockSpec((tm, HEAD_DIM), lambda b, j: (j, 0)),
        ],
        out_specs=pl.BlockSpec((1, tm, NO), lambda b, j: (b, j, 0)),
        out_shape=jax.ShapeDtypeStruct((B, S, NO), BF16),
        compiler_params=_cparams(("arbitrary", "arbitrary")),
        name="qkv_rope",
    )(ctx, x, modt, w_bf16, cos, sin)


ATTN_KV_PER_STEP = 4


def _attn_one_head(sink_ref, head, q, kc, vc, k_win, v_win, n, n_lat_blk, latent):
    blk = WINDOW
    rows = GROUP * blk
    row = lax.broadcasted_iota(jnp.int32, (rows, 1), 0)
    sink = jnp.full((rows, 1), sink_ref[head * GROUP + GROUP - 1], F32)
    for g in range(GROUP - 2, -1, -1):
        sink = jnp.where(row < (g + 1) * blk, sink_ref[head * GROUP + g], sink)
    nt = (((1,), (1,)), ((), ()))
    s_c = lax.dot_general(q, kc, nt, preferred_element_type=F32)
    scores = [(s_c, vc)]
    if latent:
        a = lax.broadcasted_iota(jnp.int32, (rows, blk), 0) % blk
        key = lax.broadcasted_iota(jnp.int32, (rows, blk), 1)
        s0 = lax.dot_general(q, k_win[0], nt, preferred_element_type=F32)
        s0 = jnp.where((key >= a) & (n > 0), s0, -jnp.inf)
        s1 = lax.dot_general(q, k_win[1], nt, preferred_element_type=F32)
        s2 = lax.dot_general(q, k_win[2], nt, preferred_element_type=F32)
        s2 = jnp.where((key <= a) & (n < n_lat_blk - 1), s2, -jnp.inf)
        scores += [(s0, v_win[0]), (s1, v_win[1]), (s2, v_win[2])]
    lane_tiles = [s[:, j * LANES:(j + 1) * LANES] for s, _ in scores for j in range(s.shape[1] // LANES)]
    m_el = lane_tiles[0]
    for t in lane_tiles[1:]:
        m_el = jnp.maximum(m_el, t)
    m = jnp.maximum(jnp.max(m_el, axis=1, keepdims=True), sink)
    acc = jnp.zeros((rows, HEAD_DIM), F32)
    l_el = jnp.zeros((rows, LANES), F32)
    for s, v in scores:
        p = jnp.exp(s - m)
        for j in range(s.shape[1] // LANES):
            l_el = l_el + p[:, j * LANES:(j + 1) * LANES]
        acc = acc + jnp.dot(p.astype(BF16), v, preferred_element_type=F32)
    denom = jnp.exp(sink - m) + jnp.sum(l_el, axis=1, keepdims=True)
    return acc / denom


def _attn_body(sink_ref, q_ref, kc_ref, vc_ref, k0_ref, k1_ref, k2_ref, v0_ref, v1_ref, v2_ref, o_ref,
               *, n_ctx_blk, n_lat_blk):
    kh0 = pl.program_id(1) * ATTN_KV_PER_STEP
    qb = pl.program_id(2)
    blk = WINDOW
    gw = GROUP * HEAD_DIM

    def run(latent):
        n = qb - n_ctx_blk
        for i in range(ATTN_KV_PER_STEP):
            hs = slice(i * HEAD_DIM, (i + 1) * HEAD_DIM)
            q = jnp.concatenate([q_ref[0, :, i * gw + g * HEAD_DIM:i * gw + (g + 1) * HEAD_DIM]
                                 for g in range(GROUP)], axis=0)
            k_win = [r[0, :, hs] for r in (k0_ref, k1_ref, k2_ref)] if latent else None
            v_win = [r[0, :, hs] for r in (v0_ref, v1_ref, v2_ref)] if latent else None
            o = _attn_one_head(sink_ref, kh0 + i, q, kc_ref[0, :, hs], vc_ref[0, :, hs], k_win, v_win,
                               n, n_lat_blk, latent)
            for g in range(GROUP):
                o_ref[0, :, i * gw + g * HEAD_DIM:i * gw + (g + 1) * HEAD_DIM] = (
                    o[g * blk:(g + 1) * blk].astype(o_ref.dtype))

    @pl.when(qb < n_ctx_blk)
    def _():
        run(False)

    @pl.when(qb >= n_ctx_blk)
    def _():
        run(True)


def _attention(qkv, sink, C, T):
    B, S, _ = qkv.shape
    blk = WINDOW
    n_ctx_blk = C // blk
    n_lat_blk = T // blk
    kvs = ATTN_KV_PER_STEP
    kvw = kvs * HEAD_DIM
    k_col = N_HEADS // kvs
    v_col = (N_HEADS + N_KV_HEADS) // kvs

    def win(j, col):
        def im(b, kh, qb):
            n = jnp.maximum(qb - n_ctx_blk, 0)
            return (b, n_ctx_blk + jnp.clip(n + j - 1, 0, n_lat_blk - 1), col + kh)
        return im

    kv_blk = (1, blk, kvw)
    q_blk = (1, blk, kvs * GROUP * HEAD_DIM)
    body = functools.partial(_attn_body, n_ctx_blk=n_ctx_blk, n_lat_blk=n_lat_blk)
    return pl.pallas_call(
        body,
        grid=(B, N_KV_HEADS // kvs, S // blk),
        in_specs=[
            pl.BlockSpec(memory_space=pltpu.SMEM),
            pl.BlockSpec(q_blk, lambda b, kh, qb: (b, qb, kh)),
            pl.BlockSpec((1, C, kvw), lambda b, kh, qb: (b, 0, k_col + kh)),
            pl.BlockSpec((1, C, kvw), lambda b, kh, qb: (b, 0, v_col + kh)),
            pl.BlockSpec(kv_blk, win(0, k_col)), pl.BlockSpec(kv_blk, win(1, k_col)),
            pl.BlockSpec(kv_blk, win(2, k_col)),
            pl.BlockSpec(kv_blk, win(0, v_col)), pl.BlockSpec(kv_blk, win(1, v_col)),
            pl.BlockSpec(kv_blk, win(2, v_col)),
        ],
        out_specs=pl.BlockSpec(q_blk, lambda b, kh, qb: (b, qb, kh)),
        out_shape=jax.ShapeDtypeStruct((B, S, N_HEADS * HEAD_DIM), BF16),
        compiler_params=_cparams(("arbitrary", "arbitrary", "arbitrary")),
        name="window_attention",
    )(sink, qkv, qkv, qkv, qkv, qkv, qkv, qkv, qkv, qkv)


def _layer_norm_rows(r, g, b):
    mu = jnp.mean(r, axis=-1, keepdims=True)
    rc = r - mu
    var = jnp.mean(rc * rc, axis=-1, keepdims=True)
    return rc * lax.rsqrt(var + LN_EPS) * g + b


def _route(logits):
    lane = lax.broadcasted_iota(jnp.int32, logits.shape, 1)
    neg = -jnp.inf

    def first_max(vals):
        mx = jnp.max(vals, axis=1, keepdims=True)
        idx = jnp.min(jnp.where(vals == mx, lane, LANES), axis=1, keepdims=True)
        return mx, idx

    gmask = lane < MOE_GROUPS
    gl = jnp.where(gmask, logits, neg)
    gmax, gidx = first_max(gl)
    gsum = jnp.sum(jnp.where(gmask, jnp.exp(gl - gmax), 0.0), axis=1, keepdims=True)
    g_w = 1.0 / gsum
    lo = MOE_GROUPS + gidx * MOE_EXPERTS_PER_GROUP
    el = jnp.where((lane >= lo) & (lane < lo + MOE_EXPERTS_PER_GROUP), logits, neg)
    v1, i1 = first_max(el)
    v2, i2 = first_max(jnp.where(lane == i1, neg, el))
    e2 = jnp.exp(v2 - v1)
    w1 = g_w / (1.0 + e2)
    w2 = g_w * e2 / (1.0 + e2)
    ids = jnp.where(lane == 0, i1 - MOE_GROUPS, jnp.where(lane == 1, i2 - MOE_GROUPS, 0))
    wts = jnp.where(lane == 0, w1, jnp.where(lane == 1, w2, 0.0))
    return ids, wts


MIX_SUB_TILES = 2


def _mix_out_body(*refs, alpha, gla, n_ctx_tiles):
    wr_ref, br_ref, xo_ref, h_ref, id_ref, wt_ref = refs[-6:]
    refs = refs[:-6]
    tm = xo_ref.shape[1]
    sub = tm // MIX_SUB_TILES
    wpt = h_ref.shape[1] // tm
    for t in range(MIX_SUB_TILES):
        rs = slice(t * sub, (t + 1) * sub)
        if gla:
            of_ref, ob_ref, r_ref, ng_ref, x_ref, mod_ref, w_ref, lng_ref, lnb_ref = refs
            x = x_ref[0, rs]
            o = of_ref[0, rs].astype(F32) + ob_ref[0, rs].astype(F32)
            dv = ng_ref.shape[1]
            parts = []
            for hd in range(GLA_HEADS):
                oh = o[:, hd * dv:(hd + 1) * dv]
                ms = jnp.mean(oh * oh, axis=-1, keepdims=True)
                parts.append(oh * lax.rsqrt(ms + RMS_EPS) * ng_ref[...])
            a = (jnp.concatenate(parts, axis=1) * _silu(r_ref[0, rs].astype(F32))).astype(BF16)
        else:
            a_ref, c_ref, x_ref, mod_ref, w_ref, lng_ref, lnb_ref = refs
            a = a_ref[0, rs]
            x = jnp.where(pl.program_id(1) < n_ctx_tiles, c_ref[0, rs], x_ref[0, rs])
        m = mod_ref[0, 0]
        y = jnp.dot(a, w_ref[...], preferred_element_type=F32)
        xn = _layer_norm_rows(alpha * x + m[2:3] * y, lng_ref[...], lnb_ref[...])
        xo_ref[0, rs] = xn
        h = xn * (1.0 + m[4:5]) + m[3:4]
        _store_token_tiles(h_ref.at[0, pl.ds(t * sub * wpt, sub * wpt)], _pack_bf16_pairs(h))
        ids, wts = _route(jnp.dot(h.astype(BF16), wr_ref[...], preferred_element_type=F32) + br_ref[...])
        id_ref[0, rs] = ids
        wt_ref[0, rs] = wts


def _router_params(wg, bg, we, be):
    D = wg.shape[0]
    pad = LANES - MOE_GROUPS - N_EXPERTS
    wr = jnp.concatenate([wg, we, jnp.zeros((D, pad), F32)], axis=1).astype(BF16)
    br = jnp.concatenate([bg, be, jnp.zeros((pad,), F32)]).reshape(1, LANES)
    return wr, br


def _mix_out_tail(B, R, D, tm):
    tok = pl.BlockSpec((1, tm, D), lambda b, j: (b, j, 0))
    wpt = D // 2 // LANES
    packed = pl.BlockSpec((1, tm * wpt, LANES), lambda b, j: (b, j, 0))
    lane_tile = pl.BlockSpec((1, tm, LANES), lambda b, j: (b, j, 0))
    in_specs = [pl.BlockSpec((D, LANES), lambda b, j: (0, 0)), pl.BlockSpec((1, LANES), lambda b, j: (0, 0))]
    out_specs = [tok, packed, lane_tile, lane_tile]
    out_shape = [jax.ShapeDtypeStruct((B, R, D), F32), jax.ShapeDtypeStruct((B, R * wpt, LANES), jnp.uint32),
                 jax.ShapeDtypeStruct((B, R, LANES), jnp.int32), jax.ShapeDtypeStruct((B, R, LANES), F32)]
    return in_specs, out_specs, out_shape


def _attn_out(attn, ctx, x, modt, w_bf16, ln_g, ln_b, router, n_ctx_tiles, alpha):
    B, S, D = attn.shape
    tm = TOKEN_TILE
    seg = lambda j: jnp.where(j >= n_ctx_tiles, 1, 0)
    tok = pl.BlockSpec((1, tm, D), lambda b, j: (b, j, 0))
    vec = pl.BlockSpec((1, D), lambda b, j: (0, 0))
    r_in, out_specs, out_shape = _mix_out_tail(B, S, D, tm)
    return pl.pallas_call(
        functools.partial(_mix_out_body, alpha=alpha, gla=False, n_ctx_tiles=n_ctx_tiles),
        grid=(B, S // tm),
        in_specs=[tok] + _stream_specs(tm, D, n_ctx_tiles) + [
                  pl.BlockSpec((1, 1, 6, D), lambda b, j: (b, seg(j), 0, 0)),
                  pl.BlockSpec((D, D), lambda b, j: (0, 0)), vec, vec] + r_in,
        out_specs=out_specs,
        out_shape=out_shape,
        compiler_params=_cparams(("arbitrary", "arbitrary")),
        name="attn_out_ln",
    )(attn, ctx, x, modt, w_bf16, ln_g.reshape(1, D), ln_b.reshape(1, D), *router)


def _gla_out(o2, proj, norm_g, xs, modt, w_bf16, ln_g, ln_b, router, n_ctx_tiles, T, alpha):
    B, S, D = xs.shape
    tm = TOKEN_TILE
    dv = D // GLA_HEADS
    r_col = proj.shape[2] // D - 1
    off = n_ctx_tiles
    vec = pl.BlockSpec((1, D), lambda b, j: (0, 0))
    r_in, out_specs, out_shape = _mix_out_tail(B, T, D, tm)
    return pl.pallas_call(
        functools.partial(_mix_out_body, alpha=alpha, gla=True, n_ctx_tiles=n_ctx_tiles),
        grid=(B, T // tm),
        in_specs=[
            pl.BlockSpec((1, tm, D), lambda b, j: (b, j + off, 0)),
            pl.BlockSpec((1, tm, D), lambda b, j: (b, j + off, 0)),
            pl.BlockSpec((1, tm, D), lambda b, j: (b, j + off, r_col)),
            pl.BlockSpec((1, dv), lambda b, j: (0, 0)),
            pl.BlockSpec((1, tm, D), lambda b, j: (b, j + off, 0)),
            pl.BlockSpec((1, 1, 6, D), lambda b, j: (b, 1, 0, 0)),
            pl.BlockSpec((D, D), lambda b, j: (0, 0)), vec, vec,
        ] + r_in,
        out_specs=out_specs,
        out_shape=out_shape,
        compiler_params=_cparams(("arbitrary", "arbitrary")),
        name="gla_out_ln",
    )(o2[0], o2[1], proj, norm_g.reshape(1, dv), xs, modt, w_bf16, ln_g.reshape(1, D), ln_b.reshape(1, D),
      *router)


def _moe_body(te_ref, nv_ref, nu_ref, tok_ref, tokn_ref, dst_ref, rw_ref, h_hbm, wg_ref, wu_ref, wd_ref,
              y_hbm, xbuf, ybuf, wgb, wub, wdb, gsem, ssem):
    i = pl.program_id(0)
    n_used = nu_ref[0]

    unroll = 8
    tm = rw_ref.shape[0]
    wpt = xbuf.shape[1] // tm

    def token_rows(ref, first_word_row):
        return ref.at[pl.ds(pl.multiple_of(first_word_row, SUBLANES), wpt), :]

    def gather(idx_ref, n_rows, s):
        def issue(r8, c):
            for u in range(unroll):
                r = r8 * unroll + u
                pltpu.make_async_copy(token_rows(h_hbm, idx_ref[0, 0, r]),
                                      token_rows(xbuf.at[s], r * wpt), gsem.at[s]).start()
            return c
        lax.fori_loop(0, n_rows // unroll, issue, 0)

    def gather_wait(n_rows, s):
        n = pl.multiple_of(n_rows * wpt, SUBLANES)
        pltpu.make_async_copy(h_hbm.at[pl.ds(0, n), :], xbuf.at[s, pl.ds(0, n), :], gsem.at[s]).wait()

    def scatter(n_rows, s):
        def issue(r8, c):
            for u in range(unroll):
                r = r8 * unroll + u
                pltpu.make_async_copy(token_rows(ybuf.at[s], r * wpt),
                                      token_rows(y_hbm, dst_ref[0, 0, r]), ssem.at[s]).start()
            return c
        lax.fori_loop(0, n_rows // unroll, issue, 0)

    def scatter_wait(n_rows, s):
        n = pl.multiple_of(n_rows * wpt, SUBLANES)
        pltpu.make_async_copy(ybuf.at[s, pl.ds(0, n), :], y_hbm.at[pl.ds(0, n), :], ssem.at[s]).wait()

    @pl.when(i == 0)
    def _():
        xbuf[...] = jnp.zeros(xbuf.shape, xbuf.dtype)
        spare = y_hbm.shape[0] - 2 * tm * wpt
        for s in range(2):
            cp = pltpu.make_async_copy(xbuf.at[s], y_hbm.at[pl.ds(spare + s * tm * wpt, tm * wpt), :],
                                       ssem.at[s])
            cp.start()
            cp.wait()
        gather(tok_ref, nv_ref[0], 0)

    def step(slot):
        @pl.when(i + 1 < n_used)
        def _():
            gather(tokn_ref, nv_ref[i + 1], 1 - slot)

        @pl.when(i < n_used)
        def _():
            compute(slot)

    def compute(slot):
        nv = nv_ref[i]
        gather_wait(nv, slot)

        @pl.when((i == 0) | (te_ref[i] != te_ref[jnp.maximum(i - 1, 0)]))
        def _():
            wgb[...] = wg_ref[0, 0].astype(BF16)
            wub[...] = wu_ref[0, 0].astype(BF16)
            wdb[...] = wd_ref[0, 0].astype(BF16)

        @pl.when(i >= 2)
        def _():
            scatter_wait(nv_ref[jnp.maximum(i - 2, 0)], slot)

        x = jnp.concatenate(_unpack_bf16_pairs(_load_token_tiles(xbuf.at[slot], tm)), axis=1).astype(BF16)
        g = jnp.dot(x, wgb[...], preferred_element_type=F32)
        u = jnp.dot(x, wub[...], preferred_element_type=F32)
        hid = (_silu(g) * u).astype(BF16)
        y = jnp.dot(hid, wdb[...], preferred_element_type=F32)
        _store_token_tiles(ybuf.at[slot], _pack_bf16_pairs(y * rw_ref[:, 0:1]))
        scatter(nv, slot)

        @pl.when(i == n_used - 1)
        def _():
            @pl.when(i >= 1)
            def _():
                scatter_wait(nv_ref[jnp.maximum(i - 1, 0)], 1 - slot)
            scatter_wait(nv, slot)

    for slot in range(2):
        pl.when(i % 2 == slot)(functools.partial(step, slot))


def _moe_experts(h, tile_expert, tile_valid, n_used, row_tok, row_dst, row_w, w_gate, w_up, w_down, layer):
    _, E, D, Hd = w_gate.shape
    wpt = D // 2 // LANES
    assert wpt % SUBLANES == 0
    N = h.shape[0] // wpt
    tm = MOE_TILE
    n_tiles = tile_expert.shape[0]
    idx_blk = lambda f: pl.BlockSpec((1, 1, tm), f, memory_space=pltpu.SMEM)
    grid_spec = pltpu.PrefetchScalarGridSpec(
        num_scalar_prefetch=3,
        grid=(n_tiles,),
        in_specs=[
            idx_blk(lambda i, te, nv, nu: (i, 0, 0)),
            idx_blk(lambda i, te, nv, nu: (jnp.minimum(i + 1, n_tiles - 1), 0, 0)),
            idx_blk(lambda i, te, nv, nu: (i, 0, 0)),
            pl.BlockSpec((tm, LANES), lambda i, te, nv, nu: (i, 0)),
            pl.BlockSpec(memory_space=pl.ANY),
            pl.BlockSpec((1, 1, D, Hd), lambda i, te, nv, nu: (layer, te[i], 0, 0)),
            pl.BlockSpec((1, 1, D, Hd), lambda i, te, nv, nu: (layer, te[i], 0, 0)),
            pl.BlockSpec((1, 1, Hd, D), lambda i, te, nv, nu: (layer, te[i], 0, 0)),
        ],
        out_specs=pl.BlockSpec(memory_space=pl.ANY),
        scratch_shapes=[
            pltpu.VMEM((2, tm * wpt, LANES), jnp.uint32), pltpu.VMEM((2, tm * wpt, LANES), jnp.uint32),
            pltpu.VMEM((D, Hd), BF16), pltpu.VMEM((D, Hd), BF16), pltpu.VMEM((Hd, D), BF16),
            pltpu.SemaphoreType.DMA((2,)), pltpu.SemaphoreType.DMA((2,)),
        ],
    )
    return pl.pallas_call(
        _moe_body,
        grid_spec=grid_spec,
        out_shape=jax.ShapeDtypeStruct(((2 * N + 2 * tm) * wpt, LANES), jnp.uint32),
        compiler_params=_cparams(("arbitrary",)),
        name="moe_experts",
    )(tile_expert, tile_valid, n_used, row_tok, row_tok, row_dst, row_w, h, w_gate, w_up, w_down)


def _moe_plan(ids, wts, N, wpt):
    tm = MOE_TILE
    E = N_EXPERTS
    A = 2 * N
    n_tiles = A // tm + E
    e_flat = ids.reshape(A)
    counts = jnp.sum((e_flat[:, None] == jnp.arange(E, dtype=jnp.int32)[None, :]).astype(jnp.int32), axis=0)
    fill = (-counts) % tm
    fj = jnp.arange(tm, dtype=jnp.int32)[None, :]
    fe = jnp.arange(E, dtype=jnp.int32)[:, None]
    fkey = jnp.where(fj < fill[:, None], fe, E).reshape(E * tm)
    n_fill = E * tm
    shift = 20
    assert A + n_fill < (1 << shift) and (E + 1) << shift < (1 << 31)
    idx = jnp.arange(A + n_fill, dtype=jnp.int32)
    keys = (jnp.concatenate([e_flat, fkey]) << shift) | idx
    wgt = jnp.concatenate([wts.reshape(A), jnp.zeros((n_fill,), F32)])
    key_s, w_s = lax.sort((keys, wgt), num_keys=1)
    e_s = (key_s >> shift).reshape(n_tiles, tm)
    a_s = (key_s & ((1 << shift) - 1)).reshape(n_tiles, tm)
    real = a_s < A
    n_used = jnp.sum(counts + fill) // tm
    t_idx = jnp.arange(n_tiles, dtype=jnp.int32)
    te = jnp.minimum(e_s[:, 0], E - 1)
    last_e = jnp.max(jnp.where(t_idx < n_used, te, 0))
    te = jnp.where(t_idx < n_used, te, last_e).astype(jnp.int32)
    tile_valid = jnp.sum(real.astype(jnp.int32), axis=1)
    tile_rows = jnp.where(t_idx < n_used, (tile_valid + 7) // 8 * 8, 0).astype(jnp.int32)
    spare = A + (t_idx[:, None] % 2) * tm + fj
    row_tok = (jnp.where(real, a_s >> 1, 0) * wpt).astype(jnp.int32)
    row_dst = (jnp.where(real, (a_s & 1) * N + (a_s >> 1), spare) * wpt).astype(jnp.int32)
    row_w = jnp.broadcast_to(w_s[:, None], (n_tiles * tm, LANES))
    return (te, tile_rows, n_used.reshape(1).astype(jnp.int32),
            row_tok.reshape(n_tiles, 1, tm), row_dst.reshape(n_tiles, 1, tm), row_w)


def _moe_out_body(y0_ref, y1_ref, x_ref, mod_ref, lng_ref, lnb_ref, o_ref, *, alpha):
    m = mod_ref[0, 0]
    rows = x_ref.shape[1]
    lo0, hi0 = _unpack_bf16_pairs(_load_token_tiles(y0_ref, rows))
    lo1, hi1 = _unpack_bf16_pairs(_load_token_tiles(y1_ref, rows))
    f = jnp.concatenate([lo0 + lo1, hi0 + hi1], axis=1)
    o_ref[0] = _layer_norm_rows(alpha * x_ref[0] + m[5:6] * f, lng_ref[...], lnb_ref[...])


def _moe_out(y, xs, modt, ln_g, ln_b, n_ctx_tiles, alpha):
    B, R, D = xs.shape
    tm = TOKEN_TILE
    rt = R // tm
    wpt = D // 2 // LANES
    if n_ctx_tiles < 0:
        seg = lambda j: 1
    else:
        seg = lambda j: jnp.where(j >= n_ctx_tiles, 1, 0)
    tok = pl.BlockSpec((1, tm, D), lambda b, j: (b, j, 0))
    vec = pl.BlockSpec((1, D), lambda b, j: (0, 0))
    return pl.pallas_call(
        functools.partial(_moe_out_body, alpha=alpha),
        grid=(B, R // tm),
        in_specs=[pl.BlockSpec((tm * wpt, LANES), lambda b, j: (b * rt + j, 0)),
                  pl.BlockSpec((tm * wpt, LANES), lambda b, j: (B * rt + b * rt + j, 0)),
                  tok, pl.BlockSpec((1, 1, 6, D), lambda b, j: (b, seg(j), 0, 0)), vec, vec],
        out_specs=tok,
        out_shape=jax.ShapeDtypeStruct((B, R, D), F32),
        compiler_params=_cparams(("arbitrary", "arbitrary")),
        name="moe_out_ln",
    )(y, y, xs, modt, ln_g.reshape(1, D), ln_b.reshape(1, D))


def _hier_moe(h, ids, wts, xs, modt, w_gate, w_up, w_down, layer, ln_g, ln_b, n_ctx_tiles, alpha):
    B, R, D = xs.shape
    N = B * R
    wpt = D // 2 // LANES
    plan = _moe_plan(ids.reshape(N, LANES)[:, :2], wts.reshape(N, LANES)[:, :2], N, wpt)
    y = _moe_experts(h.reshape(N * wpt, LANES), *plan, w_gate, w_up, w_down, layer)
    return _moe_out(y, xs, modt, ln_g, ln_b, n_ctx_tiles, alpha)


GLA_PROJ_COL_TILE = 1024


def _gla_proj_body(x_ref, mod_ref, w_ref, w1_ref, w2_ref, b_ref, o_ref, g_ref, *, q_cols, q_scale):
    m = mod_ref[0, 0]
    z = (x_ref[0] * (1.0 + m[1:2]) + m[0:1]).astype(BF16)
    ct = GLA_PROJ_COL_TILE
    for c0 in range(0, o_ref.shape[2], ct):
        acc = jnp.dot(z, w_ref[:, c0:c0 + ct], preferred_element_type=F32)
        if c0 < q_cols:
            acc = acc * q_scale
        o_ref[0, :, c0:c0 + ct] = acc.astype(BF16)
    t = jnp.dot(z, w1_ref[...], preferred_element_type=F32)
    pre = jnp.dot(t.astype(BF16), w2_ref[...], preferred_element_type=F32) + b_ref[...]
    g_ref[0] = (jnp.minimum(pre, 0.0) - jnp.log(1.0 + jnp.exp(-jnp.abs(pre)))) * (1.0 / GLA_TAU)


def _gla_proj(xs, modt, w_bf16, w1, w2, gb, n_ctx_tiles, q_cols, q_scale):
    B, S, D = xs.shape
    NO = w_bf16.shape[1]
    rank = w1.shape[2]
    kd = w2.shape[2]
    tm = TOKEN_TILE
    assert q_cols % GLA_PROJ_COL_TILE == 0 and NO % GLA_PROJ_COL_TILE == 0
    w1c = jnp.zeros((D, LANES), F32).at[:, :rank].set(w1[0]).at[:, rank:2 * rank].set(w1[1]).astype(BF16)
    w2c = (jnp.zeros((LANES, 2 * kd), F32).at[:rank, :kd].set(w2[0]).at[rank:2 * rank, kd:].set(w2[1])
           .astype(BF16))
    seg = lambda j: jnp.where(j >= n_ctx_tiles, 1, 0)
    const = lambda shape: pl.BlockSpec(shape, lambda b, j: (0, 0))
    return pl.pallas_call(
        functools.partial(_gla_proj_body, q_cols=q_cols, q_scale=q_scale),
        grid=(B, S // tm),
        in_specs=[
            pl.BlockSpec((1, tm, D), lambda b, j: (b, j, 0)),
            pl.BlockSpec((1, 1, 6, D), lambda b, j: (b, seg(j), 0, 0)),
            pl.BlockSpec((D, NO), lambda b, j: (0, 0), pipeline_mode=pl.Buffered(1)),
            const((D, LANES)), const((LANES, 2 * kd)), const((1, 2 * kd)),
        ],
        out_specs=[pl.BlockSpec((1, tm, NO), lambda b, j: (b, j, 0)),
                   pl.BlockSpec((1, tm, 2 * kd), lambda b, j: (b, j, 0))],
        out_shape=[jax.ShapeDtypeStruct((B, S, NO), BF16), jax.ShapeDtypeStruct((B, S, 2 * kd), F32)],
        compiler_params=_cparams(("arbitrary", "arbitrary")),
        name="gla_proj",
    )(xs, modt, w_bf16, w1c, w2c, gb.reshape(1, 2 * kd))


GLA_LEVELS = (32, 16, 8, 4, 2, 1)


def _gla_constants():
    L = GLA_CHUNK
    t = np.arange(L)[:, None]
    u = np.arange(L)[None, :]
    mats, masks = [], []
    for bwd in (False, True):
        blocks = []
        blocks.append((u >= t) if bwd else (u <= t))
        blocks.append((u < t) if bwd else (u > t))
        q_lv, k_lv, m_lv = [], [], []
        for m in GLA_LEVELS:
            base = (t // (2 * m)) * (2 * m)
            ubase = (u // (2 * m)) * (2 * m)
            t_hi = (t % (2 * m)) >= m
            if not bwd:
                r = base + m - 1
                q_lv.append(t_hi & (u > r) & (u <= t))
                k_lv.append(~t_hi & (u > t) & (u <= r))
                m_lv.append(t_hi & ((u % (2 * m)) < m) & (base == ubase))
            else:
                r = base + m
                q_lv.append(~t_hi & (u >= t) & (u < r))
                k_lv.append(t_hi & (u >= r) & (u < t))
                m_lv.append(~t_hi & ((u % (2 * m)) >= m) & (base == ubase))
        blocks += [ql | kl for ql, kl in zip(q_lv, k_lv)]
        blocks.append(np.ones((8, L), bool))
        a = np.concatenate(blocks, axis=0).astype(np.float32)
        mats.append(np.concatenate([a, a, a], axis=1))
        m_lv.append(t == u)
        masks.append(np.concatenate(m_lv, axis=0).astype(np.float32))
    return np.stack(mats), np.stack(masks)


LOG2_E = 1.4426950408889634


def _gla_chunk_step(q, k, v, g, a_ref, mk_ref, st_ref, hd):
    L = GLA_CHUNK
    nlv = len(GLA_LEVELS)
    g = g * LOG2_E
    g_hi = g.astype(BF16)
    r1 = g - g_hi.astype(F32)
    g_mid = r1.astype(BF16)
    g_lo = (r1 - g_mid.astype(F32)).astype(BF16)
    gs = jnp.concatenate([g_hi, g_mid, g_lo], axis=0)
    ex = jnp.exp2(jnp.dot(a_ref[...], gs, preferred_element_type=F32))
    qf = q.astype(F32)
    kf = k.astype(F32)
    q_inter = (qf * ex[0:L]).astype(BF16)
    k_state = (kf * ex[L:2 * L]).astype(BF16)
    dec = ex[(2 + nlv) * L:(2 + nlv) * L + 1]
    nt = (((1,), (1,)), ((), ()))
    att = lax.dot_general(q, k, nt, preferred_element_type=F32) * mk_ref[nlv * L:(nlv + 1) * L]
    for lv in range(nlv):
        f = ex[(2 + lv) * L:(3 + lv) * L]
        ql = (qf * f).astype(BF16)
        kl = (kf * f).astype(BF16)
        att = att + lax.dot_general(ql, kl, nt, preferred_element_type=F32) * mk_ref[lv * L:(lv + 1) * L]
    st = st_ref[hd]
    o = jnp.dot(att.astype(BF16), v, preferred_element_type=F32)
    o = o + lax.dot_general(q_inter, st.astype(BF16), nt, preferred_element_type=F32)
    upd = lax.dot_general(v, k_state, (((0,), (0,)), ((), ())), preferred_element_type=F32)
    st_ref[hd] = st * dec + upd
    return o


def _gla_scan_body(qf_ref, kf_ref, vf_ref, gf_ref, qb_ref, kb_ref, vb_ref, gb_ref,
                   af_ref, ab_ref, mf_ref, mb_ref, of_ref, ob_ref, stf_ref, stb_ref):
    @pl.when(pl.program_id(1) == 0)
    def _():
        stf_ref[...] = jnp.zeros(stf_ref.shape, stf_ref.dtype)
        stb_ref[...] = jnp.zeros(stb_ref.shape, stb_ref.dtype)

    _, dv, dk = stf_ref.shape
    for hd in range(GLA_HEADS):
        ks = slice(hd * dk, (hd + 1) * dk)
        vs = slice(hd * dv, (hd + 1) * dv)
        o = _gla_chunk_step(qf_ref[0, :, ks], kf_ref[0, :, ks], vf_ref[0, :, vs], gf_ref[0, :, ks],
                            af_ref, mf_ref, stf_ref, hd)
        of_ref[0, :, vs] = o.astype(of_ref.dtype)
        o = _gla_chunk_step(qb_ref[0, :, ks], kb_ref[0, :, ks], vb_ref[0, :, vs], gb_ref[0, :, ks],
                            ab_ref, mb_ref, stb_ref, hd)
        ob_ref[0, :, vs] = o.astype(ob_ref.dtype)


def _gla_scan(proj, lg, C, T):
    B, S, NO = proj.shape
    kd = lg.shape[2] // 2
    D = NO - 2 * kd
    D = D // 2
    dk = kd // GLA_HEADS
    dv = D // GLA_HEADS
    L = GLA_CHUNK
    n_ctx = C // L
    n_all = S // L
    amat, masks = _gla_constants()
    amat = jnp.asarray(amat, BF16)
    masks = jnp.asarray(masks, F32)

    def fwd(c):
        return c

    def bwd(c):
        return jnp.where(c < n_ctx, n_ctx - 1 - c, n_ctx + n_all - 1 - c)

    assert 2 * kd == D

    def specs(chunk, d):
        return [
            pl.BlockSpec((1, L, kd), lambda b, c: (b, chunk(c), 0)),
            pl.BlockSpec((1, L, kd), lambda b, c: (b, chunk(c), 1)),
            pl.BlockSpec((1, L, D), lambda b, c: (b, chunk(c), 1)),
            pl.BlockSpec((1, L, kd), lambda b, c: (b, chunk(c), d)),
        ]

    const = lambda arr: pl.BlockSpec(arr.shape[1:], lambda b, c: (0, 0))
    out = jax.ShapeDtypeStruct((B, S, D), BF16)
    state = pltpu.VMEM((GLA_HEADS, dv, dk), F32)
    return pl.pallas_call(
        _gla_scan_body,
        grid=(B, n_all),
        in_specs=specs(fwd, 0) + specs(bwd, 1) + [const(amat), const(amat), const(masks), const(masks)],
        out_specs=[pl.BlockSpec((1, L, D), lambda b, c: (b, fwd(c), 0)),
                   pl.BlockSpec((1, L, D), lambda b, c: (b, bwd(c), 0))],
        out_shape=[out, out],
        scratch_shapes=[state, state],
        compiler_params=_cparams(("arbitrary", "arbitrary")),
        name="gla_scan",
    )(proj, proj, proj, lg, proj, proj, proj, lg, amat[0], amat[1], masks[0], masks[1])


def kernel(x, c, ctx, c_ctx, mod_w, mod_b, ln_g, ln_b, attn_w_qkv, attn_w_o, attn_sink,
           gla_w_in, gla_gate_w1, gla_gate_w2, gla_gate_b, gla_norm_g, gla_w_o,
           moe_group_w, moe_group_b, moe_router_w, moe_router_b, moe_w_gate, moe_w_up, moe_w_down):
    B, T, D = x.shape
    C = ctx.shape[1]
    depth = mod_w.shape[0]
    assert depth == 2 and D == N_HEADS * HEAD_DIM
    assert C % TOKEN_TILE == 0 and T % TOKEN_TILE == 0 and T % GRID_W == 0
    alpha = (2 * depth) ** 0.25
    n_ctx_tiles = C // TOKEN_TILE

    m_rows = -(-(B + 1) // 8) * 8
    cond = jnp.zeros((m_rows, D), F32).at[:B].set(c).at[B].set(c_ctx)
    mods = _adaln_mods(cond, mod_w, mod_b)

    def mod_table(l):
        lat = mods[l, :B].reshape(B, 1, 6, D)
        cm = jnp.broadcast_to(mods[l, B].reshape(1, 1, 6, D), (B, 1, 6, D))
        return jnp.concatenate([cm, lat], axis=1)

    modt = mod_table(0)
    cos, sin = _rope_tables(C, T)
    qkv = _qkv_proj(ctx, x, modt, attn_w_qkv[0].astype(BF16), cos, sin, n_ctx_tiles)
    attn = _attention(qkv, attn_sink[0], C, T)
    router = _router_params(moe_group_w[0], moe_group_b[0], moe_router_w[0], moe_router_b[0])
    xs, h, ids, wts = _attn_out(attn, ctx, x, modt, attn_w_o[0].astype(BF16), ln_g[0, 0], ln_b[0, 0], router,
                                n_ctx_tiles, alpha)
    xs = _hier_moe(h, ids, wts, xs, modt, moe_w_gate, moe_w_up, moe_w_down, 0, ln_g[0, 1], ln_b[0, 1],
                   n_ctx_tiles, alpha)

    modt = mod_table(1)
    kd = gla_gate_w2.shape[3]
    dk = kd // GLA_HEADS
    proj, lg = _gla_proj(xs, modt, gla_w_in[0].astype(BF16), gla_gate_w1[0], gla_gate_w2[0], gla_gate_b[0],
                         n_ctx_tiles, kd, dk ** -0.5)
    o2 = _gla_scan(proj, lg, C, T)
    router = _router_params(moe_group_w[1], moe_group_b[1], moe_router_w[1], moe_router_b[1])
    xl, h, ids, wts = _gla_out(o2, proj, gla_norm_g[0], xs, modt, gla_w_o[0].astype(BF16), ln_g[1, 0],
                               ln_b[1, 0], router, n_ctx_tiles, T, alpha)
    return _hier_moe(h, ids, wts, xl, modt, moe_w_gate, moe_w_up, moe_w_down, 1, ln_g[1, 1], ln_b[1, 1],
                     -1, alpha)
```

```python
import functools

import numpy as np
import jax
import jax.numpy as jnp
from jax import lax
from jax.experimental import pallas as pl
from jax.experimental.pallas import tpu as pltpu

F32 = jnp.float32
BF16 = jnp.bfloat16

N_HEADS = 16
N_KV_HEADS = 4
HEAD_DIM = 128
GROUP = N_HEADS // N_KV_HEADS
WINDOW = 128
GRID_W = 64
ROPE_BASE = 10000.0
GLA_HEADS = 4
GLA_TAU = 16.0
GLA_CHUNK = 64
MOE_GROUPS = 4
MOE_EXPERTS_PER_GROUP = 8
N_EXPERTS = MOE_GROUPS * MOE_EXPERTS_PER_GROUP
LN_EPS = 1e-5
RMS_EPS = 1e-6

V7X_VMEM_LIMIT_BYTES = 56 * 1024 * 1024
LANES = 128
TOKEN_TILE = 256
MOE_TILE = 256
MODS_COL_TILE = 1024


def _cparams(sem):
    return pltpu.CompilerParams(dimension_semantics=sem, vmem_limit_bytes=V7X_VMEM_LIMIT_BYTES)


def _silu(v):
    return v / (1.0 + jnp.exp(-v))


def _pack_bf16_pairs(v):
    n = v.shape[1] // 2
    lo = pltpu.bitcast(v[:, :n].astype(BF16).astype(F32), jnp.uint32)
    hi = pltpu.bitcast(v[:, n:].astype(BF16).astype(F32), jnp.uint32)
    return hi | (lo >> 16)


def _unpack_bf16_pairs(w):
    lo = pltpu.bitcast(w << 16, F32)
    hi = pltpu.bitcast(w & jnp.uint32(0xFFFF0000), F32)
    return lo, hi


SUBLANES = 8


def _store_token_tiles(ref, v):
    rows, width = v.shape
    n = width // LANES
    for s in range(n):
        ref[pl.ds(s, rows, stride=n), :] = v[:, s * LANES:(s + 1) * LANES]


def _load_token_tiles(ref, rows):
    n = ref.shape[0] // rows
    return jnp.concatenate([ref[pl.ds(s, rows, stride=n), :] for s in range(n)], axis=1)


def _mods_body(cond_ref, w_ref, b_ref, o_ref):
    a = _silu(cond_ref[...]).astype(BF16)
    o_ref[0] = jnp.dot(a, w_ref[0].astype(BF16), preferred_element_type=F32) + b_ref[0]


def _adaln_mods(cond, mod_w, mod_b):
    L, D, N6 = mod_w.shape
    M = cond.shape[0]
    tn = MODS_COL_TILE
    return pl.pallas_call(
        _mods_body,
        grid=(L, N6 // tn),
        in_specs=[
            pl.BlockSpec((M, D), lambda l, j: (0, 0)),
            pl.BlockSpec((1, D, tn), lambda l, j: (l, 0, j)),
            pl.BlockSpec((1, 1, tn), lambda l, j: (l, 0, j)),
        ],
        out_specs=pl.BlockSpec((1, M, tn), lambda l, j: (l, 0, j)),
        out_shape=jax.ShapeDtypeStruct((L, M, N6), F32),
        compiler_params=_cparams(("arbitrary", "arbitrary")),
        name="adaln_mods",
    )(cond, mod_w, mod_b.reshape(L, 1, N6))


def _stream_rows(c_ref, x_ref, n_ctx_tiles):
    return jnp.where(pl.program_id(1) < n_ctx_tiles, c_ref[0], x_ref[0])


def _stream_specs(tm, D, n_ctx_tiles):
    return [pl.BlockSpec((1, tm, D), lambda b, j: (b, jnp.minimum(j, n_ctx_tiles - 1), 0)),
            pl.BlockSpec((1, tm, D), lambda b, j: (b, jnp.maximum(j - n_ctx_tiles, 0), 0))]


def _qkv_body(c_ref, x_ref, mod_ref, w_ref, cos_ref, sin_ref, o_ref, *, n_ctx_tiles, n_rot_heads, n_q_heads,
              scale):
    m = mod_ref[0, 0]
    h = (_stream_rows(c_ref, x_ref, n_ctx_tiles) * (1.0 + m[1:2]) + m[0:1]).astype(BF16)
    acc = jnp.dot(h, w_ref[...], preferred_element_type=F32)
    cos = cos_ref[...]
    sin = sin_ref[...]
    lane = lax.broadcasted_iota(jnp.int32, cos.shape, 1)
    first = (lane & 32) == 0
    for hd in range(n_rot_heads):
        y = acc[:, hd * HEAD_DIM:(hd + 1) * HEAD_DIM]
        partner = jnp.where(first, pltpu.roll(y, HEAD_DIM - 32, 1), pltpu.roll(y, 32, 1))
        r = y * cos + partner * sin
        if hd < n_q_heads:
            r = r * scale
        o_ref[0, :, hd * HEAD_DIM:(hd + 1) * HEAD_DIM] = r.astype(BF16)
    rest = n_rot_heads * HEAD_DIM
    o_ref[0, :, rest:] = acc[:, rest:].astype(BF16)


def _rope_tables(C, T):
    half = HEAD_DIM // 2
    pos = np.arange(T)
    inv_freq = ROPE_BASE ** (-np.arange(0, half, 2, dtype=np.float32) / half)
    ang_r = (pos // GRID_W).astype(np.float32)[:, None] * inv_freq
    ang_c = (pos % GRID_W).astype(np.float32)[:, None] * inv_freq
    ang_r = jnp.asarray(ang_r, F32)
    ang_c = jnp.asarray(ang_c, F32)
    cos = jnp.concatenate([jnp.cos(ang_r)] * 2 + [jnp.cos(ang_c)] * 2, axis=-1)
    sin = jnp.concatenate([-jnp.sin(ang_r), jnp.sin(ang_r), -jnp.sin(ang_c), jnp.sin(ang_c)], axis=-1)
    cos = jnp.concatenate([jnp.ones((C, HEAD_DIM), F32), cos], axis=0)
    sin = jnp.concatenate([jnp.zeros((C, HEAD_DIM), F32), sin], axis=0)
    return cos, sin


def _qkv_proj(ctx, x, modt, w_bf16, cos, sin, n_ctx_tiles):
    B, T, D = x.shape
    S = ctx.shape[1] + T
    NO = w_bf16.shape[1]
    tm = TOKEN_TILE
    body = functools.partial(_qkv_body, n_ctx_tiles=n_ctx_tiles, n_rot_heads=N_HEADS + N_KV_HEADS,
                             n_q_heads=N_HEADS, scale=HEAD_DIM ** -0.5)
    seg = lambda j: jnp.where(j >= n_ctx_tiles, 1, 0)
    return pl.pallas_call(
        body,
        grid=(B, S // tm),
        in_specs=_stream_specs(tm, D, n_ctx_tiles) + [
            pl.BlockSpec((1, 1, 6, D), lambda b, j: (b, seg(j), 0, 0)),
            pl.BlockSpec((D, NO), lambda b, j: (0, 0)),
            pl.BlockSpec((tm, HEAD_DIM), lambda b, j: (j, 0)),
            pl.BlockSpec((tm, HEAD_DIM), lambda b, j: (j, 0)),
        ],
        out_specs=pl.BlockSpec((1, tm, NO), lambda b, j: (b, j, 0)),
        out_shape=jax.ShapeDtypeStruct((B, S, NO), BF16),
        compiler_params=_cparams(("arbitrary", "arbitrary")),
        name="qkv_rope",
    )(ctx, x, modt, w_bf16, cos, sin)


ATTN_KV_PER_STEP = 4


def _attn_one_head(sink_ref, head, q, kc, vc, k_win, v_win, n, n_lat_blk, latent):
    blk = WINDOW
    rows = GROUP * blk
    row = lax.broadcasted_iota(jnp.int32, (rows, 1), 0)
    sink = jnp.full((rows, 1), sink_ref[head * GROUP + GROUP - 1], F32)
    for g in range(GROUP - 2, -1, -1):
        sink = jnp.where(row < (g + 1) * blk, sink_ref[head * GROUP + g], sink)
    nt = (((1,), (1,)), ((), ()))
    s_c = lax.dot_general(q, kc, nt, preferred_element_type=F32)
    scores = [(s_c, vc)]
    if latent:
        a = lax.broadcasted_iota(jnp.int32, (rows, blk), 0) % blk
        key = lax.broadcasted_iota(jnp.int32, (rows, blk), 1)
        s0 = lax.dot_general(q, k_win[0], nt, preferred_element_type=F32)
        s0 = jnp.where((key >= a) & (n > 0), s0, -jnp.inf)
        s1 = lax.dot_general(q, k_win[1], nt, preferred_element_type=F32)
        s2 = lax.dot_general(q, k_win[2], nt, preferred_element_type=F32)
        s2 = jnp.where((key <= a) & (n < n_lat_blk - 1), s2, -jnp.inf)
        scores += [(s0, v_win[0]), (s1, v_win[1]), (s2, v_win[2])]
    lane_tiles = [s[:, j * LANES:(j + 1) * LANES] for s, _ in scores for j in range(s.shape[1] // LANES)]
    m_el = lane_tiles[0]
    for t in lane_tiles[1:]:
        m_el = jnp.maximum(m_el, t)
    m = jnp.maximum(jnp.max(m_el, axis=1, keepdims=True), sink)
    acc = jnp.zeros((rows, HEAD_DIM), F32)
    l_el = jnp.zeros((rows, LANES), F32)
    for s, v in scores:
        p = jnp.exp(s - m)
        for j in range(s.shape[1] // LANES):
            l_el = l_el + p[:, j * LANES:(j + 1) * LANES]
        acc = acc + jnp.dot(p.astype(BF16), v, preferred_element_type=F32)
    denom = jnp.exp(sink - m) + jnp.sum(l_el, axis=1, keepdims=True)
    return acc / denom


def _attn_body(sink_ref, q_ref, kc_ref, vc_ref, k0_ref, k1_ref, k2_ref, v0_ref, v1_ref, v2_ref, o_ref,
               *, n_ctx_blk, n_lat_blk):
    kh0 = pl.program_id(1) * ATTN_KV_PER_STEP
    qb = pl.program_id(2)
    blk = WINDOW
    gw = GROUP * HEAD_DIM

    def run(latent):
        n = qb - n_ctx_blk
        for i in range(ATTN_KV_PER_STEP):
            hs = slice(i * HEAD_DIM, (i + 1) * HEAD_DIM)
            q = jnp.concatenate([q_ref[0, :, i * gw + g * HEAD_DIM:i * gw + (g + 1) * HEAD_DIM]
                                 for g in range(GROUP)], axis=0)
            k_win = [r[0, :, hs] for r in (k0_ref, k1_ref, k2_ref)] if latent else None
            v_win = [r[0, :, hs] for r in (v0_ref, v1_ref, v2_ref)] if latent else None
            o = _attn_one_head(sink_ref, kh0 + i, q, kc_ref[0, :, hs], vc_ref[0, :, hs], k_win, v_win,
                               n, n_lat_blk, latent)
            for g in range(GROUP):
                o_ref[0, :, i * gw + g * HEAD_DIM:i * gw + (g + 1) * HEAD_DIM] = (
                    o[g * blk:(g + 1) * blk].astype(o_ref.dtype))

    @pl.when(qb < n_ctx_blk)
    def _():
        run(False)

    @pl.when(qb >= n_ctx_blk)
    def _():
        run(True)


def _attention(qkv, sink, C, T):
    B, S, _ = qkv.shape
    blk = WINDOW
    n_ctx_blk = C // blk
    n_lat_blk = T // blk
    kvs = ATTN_KV_PER_STEP
    kvw = kvs * HEAD_DIM
    k_col = N_HEADS // kvs
    v_col = (N_HEADS + N_KV_HEADS) // kvs

    def win(j, col):
        def im(b, kh, qb):
            n = jnp.maximum(qb - n_ctx_blk, 0)
            return (b, n_ctx_blk + jnp.clip(n + j - 1, 0, n_lat_blk - 1), col + kh)
        return im

    kv_blk = (1, blk, kvw)
    q_blk = (1, blk, kvs * GROUP * HEAD_DIM)
    body = functools.partial(_attn_body, n_ctx_blk=n_ctx_blk, n_lat_blk=n_lat_blk)
    return pl.pallas_call(
        body,
        grid=(B, N_KV_HEADS // kvs, S // blk),
        in_specs=[
            pl.BlockSpec(memory_space=pltpu.SMEM),
            pl.BlockSpec(q_blk, lambda b, kh, qb: (b, qb, kh)),
            pl.BlockSpec((1, C, kvw), lambda b, kh, qb: (b, 0, k_col + kh)),
            pl.BlockSpec((1, C, kvw), lambda b, kh, qb: (b, 0, v_col + kh)),
            pl.BlockSpec(kv_blk, win(0, k_col)), pl.BlockSpec(kv_blk, win(1, k_col)),
            pl.BlockSpec(kv_blk, win(2, k_col)),
            pl.BlockSpec(kv_blk, win(0, v_col)), pl.BlockSpec(kv_blk, win(1, v_col)),
            pl.BlockSpec(kv_blk, win(2, v_col)),
        ],
        out_specs=pl.BlockSpec(q_blk, lambda b, kh, qb: (b, qb, kh)),
        out_shape=jax.ShapeDtypeStruct((B, S, N_HEADS * HEAD_DIM), BF16),
        compiler_params=_cparams(("arbitrary", "arbitrary", "arbitrary")),
        name="window_attention",
    )(sink, qkv, qkv, qkv, qkv, qkv, qkv, qkv, qkv, qkv)


def _layer_norm_rows(r, g, b):
    mu = jnp.mean(r, axis=-1, keepdims=True)
    rc = r - mu
    var = jnp.mean(rc * rc, axis=-1, keepdims=True)
    return rc * lax.rsqrt(var + LN_EPS) * g + b


def _route(logits):
    lane = lax.broadcasted_iota(jnp.int32, logits.shape, 1)
    neg = -jnp.inf

    def first_max(vals):
        mx = jnp.max(vals, axis=1, keepdims=True)
        idx = jnp.min(jnp.where(vals == mx, lane, LANES), axis=1, keepdims=True)
        return mx, idx

    gmask = lane < MOE_GROUPS
    gl = jnp.where(gmask, logits, neg)
    gmax, gidx = first_max(gl)
    gsum = jnp.sum(jnp.where(gmask, jnp.exp(gl - gmax), 0.0), axis=1, keepdims=True)
    g_w = 1.0 / gsum
    lo = MOE_GROUPS + gidx * MOE_EXPERTS_PER_GROUP
    el = jnp.where((lane >= lo) & (lane < lo + MOE_EXPERTS_PER_GROUP), logits, neg)
    v1, i1 = first_max(el)
    v2, i2 = first_max(jnp.where(lane == i1, neg, el))
    e2 = jnp.exp(v2 - v1)
    w1 = g_w / (1.0 + e2)
    w2 = g_w * e2 / (1.0 + e2)
    ids = jnp.where(lane == 0, i1 - MOE_GROUPS, jnp.where(lane == 1, i2 - MOE_GROUPS, 0))
    wts = jnp.where(lane == 0, w1, jnp.where(lane == 1, w2, 0.0))
    return ids, wts


MIX_SUB_TILES = 1


def _mix_out_body(*refs, alpha, gla, n_ctx_tiles):
    wr_ref, br_ref, xo_ref, h_ref, id_ref, wt_ref = refs[-6:]
    refs = refs[:-6]
    tm = xo_ref.shape[1]
    sub = tm // MIX_SUB_TILES
    wpt = h_ref.shape[1] // tm
    for t in range(MIX_SUB_TILES):
        rs = slice(t * sub, (t + 1) * sub)
        if gla:
            of_ref, ob_ref, r_ref, ng_ref, x_ref, mod_ref, w_ref, lng_ref, lnb_ref = refs
            x = x_ref[0, rs]
            o = of_ref[0, rs].astype(F32) + ob_ref[0, rs].astype(F32)
            dv = ng_ref.shape[1]
            parts = []
            for hd in range(GLA_HEADS):
                oh = o[:, hd * dv:(hd + 1) * dv]
                ms = jnp.mean(oh * oh, axis=-1, keepdims=True)
                parts.append(oh * lax.rsqrt(ms + RMS_EPS) * ng_ref[...])
            a = (jnp.concatenate(parts, axis=1) * _silu(r_ref[0, rs].astype(F32))).astype(BF16)
        else:
            a_ref, c_ref, x_ref, mod_ref, w_ref, lng_ref, lnb_ref = refs
            a = a_ref[0, rs]
            x = jnp.where(pl.program_id(1) < n_ctx_tiles, c_ref[0, rs], x_ref[0, rs])
        m = mod_ref[0, 0]
        y = jnp.dot(a, w_ref[...], preferred_element_type=F32)
        xn = _layer_norm_rows(alpha * x + m[2:3] * y, lng_ref[...], lnb_ref[...])
        xo_ref[0, rs] = xn
        h = xn * (1.0 + m[4:5]) + m[3:4]
        _store_token_tiles(h_ref.at[0, pl.ds(t * sub * wpt, sub * wpt)], _pack_bf16_pairs(h))
        ids, wts = _route(jnp.dot(h.astype(BF16), wr_ref[...], preferred_element_type=F32) + br_ref[...])
        id_ref[0, rs] = ids
        wt_ref[0, rs] = wts


def _router_params(wg, bg, we, be):
    D = wg.shape[0]
    pad = LANES - MOE_GROUPS - N_EXPERTS
    wr = jnp.concatenate([wg, we, jnp.zeros((D, pad), F32)], axis=1).astype(BF16)
    br = jnp.concatenate([bg, be, jnp.zeros((pad,), F32)]).reshape(1, LANES)
    return wr, br


def _mix_out_tail(B, R, D, tm):
    tok = pl.BlockSpec((1, tm, D), lambda b, j: (b, j, 0))
    wpt = D // 2 // LANES
    packed = pl.BlockSpec((1, tm * wpt, LANES), lambda b, j: (b, j, 0))
    lane_tile = pl.BlockSpec((1, tm, LANES), lambda b, j: (b, j, 0))
    in_specs = [pl.BlockSpec((D, LANES), lambda b, j: (0, 0)), pl.BlockSpec((1, LANES), lambda b, j: (0, 0))]
    out_specs = [tok, packed, lane_tile, lane_tile]
    out_shape = [jax.ShapeDtypeStruct((B, R, D), F32), jax.ShapeDtypeStruct((B, R * wpt, LANES), jnp.uint32),
                 jax.ShapeDtypeStruct((B, R, LANES), jnp.int32), jax.ShapeDtypeStruct((B, R, LANES), F32)]
    return in_specs, out_specs, out_shape


def _attn_out(attn, ctx, x, modt, w_bf16, ln_g, ln_b, router, n_ctx_tiles, alpha):
    B, S, D = attn.shape
    tm = TOKEN_TILE
    seg = lambda j: jnp.where(j >= n_ctx_tiles, 1, 0)
    tok = pl.BlockSpec((1, tm, D), lambda b, j: (b, j, 0))
    vec = pl.BlockSpec((1, D), lambda b, j: (0, 0))
    r_in, out_specs, out_shape = _mix_out_tail(B, S, D, tm)
    return pl.pallas_call(
        functools.partial(_mix_out_body, alpha=alpha, gla=False, n_ctx_tiles=n_ctx_tiles),
        grid=(B, S // tm),
        in_specs=[tok] + _stream_specs(tm, D, n_ctx_tiles) + [
                  pl.BlockSpec((1, 1, 6, D), lambda b, j: (b, seg(j), 0, 0)),
                  pl.BlockSpec((D, D), lambda b, j: (0, 0)), vec, vec] + r_in,
        out_specs=out_specs,
        out_shape=out_shape,
        compiler_params=_cparams(("arbitrary", "arbitrary")),
        name="attn_out_ln",
    )(attn, ctx, x, modt, w_bf16, ln_g.reshape(1, D), ln_b.reshape(1, D), *router)


def _gla_out(o2, proj, norm_g, xs, modt, w_bf16, ln_g, ln_b, router, n_ctx_tiles, T, alpha):
    B, S, D = xs.shape
    tm = TOKEN_TILE
    dv = D // GLA_HEADS
    r_col = proj.shape[2] // D - 1
    off = n_ctx_tiles
    vec = pl.BlockSpec((1, D), lambda b, j: (0, 0))
    r_in, out_specs, out_shape = _mix_out_tail(B, T, D, tm)
    return pl.pallas_call(
        functools.partial(_mix_out_body, alpha=alpha, gla=True, n_ctx_tiles=n_ctx_tiles),
        grid=(B, T // tm),
        in_specs=[
            pl.BlockSpec((1, tm, D), lambda b, j: (b, j + off, 0)),
            pl.BlockSpec((1, tm, D), lambda b, j: (b, j + off, 0)),
            pl.BlockSpec((1, tm, D), lambda b, j: (b, j + off, r_col)),
            pl.BlockSpec((1, dv), lambda b, j: (0, 0)),
            pl.BlockSpec((1, tm, D), lambda b, j: (b, j + off, 0)),
            pl.BlockSpec((1, 1, 6, D), lambda b, j: (b, 1, 0, 0)),
            pl.BlockSpec((D, D), lambda b, j: (0, 0)), vec, vec,
        ] + r_in,
        out_specs=out_specs,
        out_shape=out_shape,
        compiler_params=_cparams(("arbitrary", "arbitrary")),
        name="gla_out_ln",
    )(o2[0], o2[1], proj, norm_g.reshape(1, dv), xs, modt, w_bf16, ln_g.reshape(1, D), ln_b.reshape(1, D),
      *router)


def _moe_body(te_ref, nv_ref, nu_ref, tok_ref, tokn_ref, dst_ref, rw_ref, h_hbm, wg_ref, wu_ref, wd_ref,
              y_hbm, xbuf, ybuf, wgb, wub, wdb, gsem, ssem):
    i = pl.program_id(0)
    n_used = nu_ref[0]

    unroll = 8
    tm = rw_ref.shape[0]
    wpt = xbuf.shape[1] // tm

    def token_rows(ref, first_word_row):
        return ref.at[pl.ds(pl.multiple_of(first_word_row, SUBLANES), wpt), :]

    def gather(idx_ref, n_rows, s):
        def issue(r8, c):
            for u in range(unroll):
                r = r8 * unroll + u
                pltpu.make_async_copy(token_rows(h_hbm, idx_ref[0, 0, r]),
                                      token_rows(xbuf.at[s], r * wpt), gsem.at[s]).start()
            return c
        lax.fori_loop(0, n_rows // unroll, issue, 0)

    def gather_wait(n_rows, s):
        n = pl.multiple_of(n_rows * wpt, SUBLANES)
        pltpu.make_async_copy(h_hbm.at[pl.ds(0, n), :], xbuf.at[s, pl.ds(0, n), :], gsem.at[s]).wait()

    def scatter(n_rows, s):
        def issue(r8, c):
            for u in range(unroll):
                r = r8 * unroll + u
                pltpu.make_async_copy(token_rows(ybuf.at[s], r * wpt),
                                      token_rows(y_hbm, dst_ref[0, 0, r]), ssem.at[s]).start()
            return c
        lax.fori_loop(0, n_rows // unroll, issue, 0)

    def scatter_wait(n_rows, s):
        n = pl.multiple_of(n_rows * wpt, SUBLANES)
        pltpu.make_async_copy(ybuf.at[s, pl.ds(0, n), :], y_hbm.at[pl.ds(0, n), :], ssem.at[s]).wait()

    @pl.when(i == 0)
    def _():
        xbuf[...] = jnp.zeros(xbuf.shape, xbuf.dtype)
        spare = y_hbm.shape[0] - 2 * tm * wpt
        for s in range(2):
            cp = pltpu.make_async_copy(xbuf.at[s], y_hbm.at[pl.ds(spare + s * tm * wpt, tm * wpt), :],
                                       ssem.at[s])
            cp.start()
            cp.wait()
        gather(tok_ref, nv_ref[0], 0)

    def step(slot):
        @pl.when(i + 1 < n_used)
        def _():
            gather(tokn_ref, nv_ref[i + 1], 1 - slot)

        @pl.when(i < n_used)
        def _():
            compute(slot)

    def compute(slot):
        nv = nv_ref[i]
        gather_wait(nv, slot)

        @pl.when((i == 0) | (te_ref[i] != te_ref[jnp.maximum(i - 1, 0)]))
        def _():
            wgb[...] = wg_ref[0, 0].astype(BF16)
            wub[...] = wu_ref[0, 0].astype(BF16)
            wdb[...] = wd_ref[0, 0].astype(BF16)

        @pl.when(i >= 2)
        def _():
            scatter_wait(nv_ref[jnp.maximum(i - 2, 0)], slot)

        x = jnp.concatenate(_unpack_bf16_pairs(_load_token_tiles(xbuf.at[slot], tm)), axis=1).astype(BF16)
        g = jnp.dot(x, wgb[...], preferred_element_type=F32)
        u = jnp.dot(x, wub[...], preferred_element_type=F32)
        hid = (_silu(g) * u).astype(BF16)
        y = jnp.dot(hid, wdb[...], preferred_element_type=F32)
        _store_token_tiles(ybuf.at[slot], _pack_bf16_pairs(y * rw_ref[:, 0:1]))
        scatter(nv, slot)

        @pl.when(i == n_used - 1)
        def _():
            @pl.when(i >= 1)
            def _():
                scatter_wait(nv_ref[jnp.maximum(i - 1, 0)], 1 - slot)
            scatter_wait(nv, slot)

    for slot in range(2):
        pl.when(i % 2 == slot)(functools.partial(step, slot))


def _moe_experts(h, tile_expert, tile_valid, n_used, row_tok, row_dst, row_w, w_gate, w_up, w_down, layer):
    _, E, D, Hd = w_gate.shape
    wpt = D // 2 // LANES
    assert wpt % SUBLANES == 0
    N = h.shape[0] // wpt
    tm = MOE_TILE
    n_tiles = tile_expert.shape[0]
    idx_blk = lambda f: pl.BlockSpec((1, 1, tm), f, memory_space=pltpu.SMEM)
    grid_spec = pltpu.PrefetchScalarGridSpec(
        num_scalar_prefetch=3,
        grid=(n_tiles,),
        in_specs=[
            idx_blk(lambda i, te, nv, nu: (i, 0, 0)),
            idx_blk(lambda i, te, nv, nu: (jnp.minimum(i + 1, n_tiles - 1), 0, 0)),
            idx_blk(lambda i, te, nv, nu: (i, 0, 0)),
            pl.BlockSpec((tm, LANES), lambda i, te, nv, nu: (i, 0)),
            pl.BlockSpec(memory_space=pl.ANY),
            pl.BlockSpec((1, 1, D, Hd), lambda i, te, nv, nu: (layer, te[i], 0, 0)),
            pl.BlockSpec((1, 1, D, Hd), lambda i, te, nv, nu: (layer, te[i], 0, 0)),
            pl.BlockSpec((1, 1, Hd, D), lambda i, te, nv, nu: (layer, te[i], 0, 0)),
        ],
        out_specs=pl.BlockSpec(memory_space=pl.ANY),
        scratch_shapes=[
            pltpu.VMEM((2, tm * wpt, LANES), jnp.uint32), pltpu.VMEM((2, tm * wpt, LANES), jnp.uint32),
            pltpu.VMEM((D, Hd), BF16), pltpu.VMEM((D, Hd), BF16), pltpu.VMEM((Hd, D), BF16),
            pltpu.SemaphoreType.DMA((2,)), pltpu.SemaphoreType.DMA((2,)),
        ],
    )
    return pl.pallas_call(
        _moe_body,
        grid_spec=grid_spec,
        out_shape=jax.ShapeDtypeStruct(((2 * N + 2 * tm) * wpt, LANES), jnp.uint32),
        compiler_params=_cparams(("arbitrary",)),
        name="moe_experts",
    )(tile_expert, tile_valid, n_used, row_tok, row_tok, row_dst, row_w, h, w_gate, w_up, w_down)


def _moe_plan(ids, wts, N, wpt):
    tm = MOE_TILE
    E = N_EXPERTS
    A = 2 * N
    n_tiles = A // tm + E
    e_flat = ids.reshape(A)
    counts = jnp.sum((e_flat[:, None] == jnp.arange(E, dtype=jnp.int32)[None, :]).astype(jnp.int32), axis=0)
    fill = (-counts) % tm
    fj = jnp.arange(tm, dtype=jnp.int32)[None, :]
    fe = jnp.arange(E, dtype=jnp.int32)[:, None]
    fkey = jnp.where(fj < fill[:, None], fe, E).reshape(E * tm)
    n_fill = E * tm
    shift = 20
    assert A + n_fill < (1 << shift) and (E + 1) << shift < (1 << 31)
    idx = jnp.arange(A + n_fill, dtype=jnp.int32)
    keys = (jnp.concatenate([e_flat, fkey]) << shift) | idx
    wgt = jnp.concatenate([wts.reshape(A), jnp.zeros((n_fill,), F32)])
    key_s, w_s = lax.sort((keys, wgt), num_keys=1)
    e_s = (key_s >> shift).reshape(n_tiles, tm)
    a_s = (key_s & ((1 << shift) - 1)).reshape(n_tiles, tm)
    real = a_s < A
    n_used = jnp.sum(counts + fill) // tm
    t_idx = jnp.arange(n_tiles, dtype=jnp.int32)
    te = jnp.minimum(e_s[:, 0], E - 1)
    last_e = jnp.max(jnp.where(t_idx < n_used, te, 0))
    te = jnp.where(t_idx < n_used, te, last_e).astype(jnp.int32)
    tile_valid = jnp.sum(real.astype(jnp.int32), axis=1)
    tile_rows = jnp.where(t_idx < n_used, (tile_valid + 7) // 8 * 8, 0).astype(jnp.int32)
    spare = A + (t_idx[:, None] % 2) * tm + fj
    row_tok = (jnp.where(real, a_s >> 1, 0) * wpt).astype(jnp.int32)
    row_dst = (jnp.where(real, (a_s & 1) * N + (a_s >> 1), spare) * wpt).astype(jnp.int32)
    row_w = jnp.broadcast_to(w_s[:, None], (n_tiles * tm, LANES))
    return (te, tile_rows, n_used.reshape(1).astype(jnp.int32),
            row_tok.reshape(n_tiles, 1, tm), row_dst.reshape(n_tiles, 1, tm), row_w)


def _moe_combine_ln(y0_ref, y1_ref, x_ref, mod_ref, lng_ref, lnb_ref, alpha):
    m = mod_ref[0, 0]
    rows = x_ref.shape[1]
    lo0, hi0 = _unpack_bf16_pairs(_load_token_tiles(y0_ref, rows))
    lo1, hi1 = _unpack_bf16_pairs(_load_token_tiles(y1_ref, rows))
    f = jnp.concatenate([lo0 + lo1, hi0 + hi1], axis=1)
    return _layer_norm_rows(alpha * x_ref[0] + m[5:6] * f, lng_ref[...], lnb_ref[...])


def _moe_out_body(y0_ref, y1_ref, x_ref, mod_ref, lng_ref, lnb_ref, o_ref, *, alpha):
    o_ref[0] = _moe_combine_ln(y0_ref, y1_ref, x_ref, mod_ref, lng_ref, lnb_ref, alpha)


def _moe_out(y, xs, modt, ln_g, ln_b, n_ctx_tiles, alpha):
    B, R, D = xs.shape
    tm = TOKEN_TILE
    rt = R // tm
    wpt = D // 2 // LANES
    if n_ctx_tiles < 0:
        seg = lambda j: 1
    else:
        seg = lambda j: jnp.where(j >= n_ctx_tiles, 1, 0)
    tok = pl.BlockSpec((1, tm, D), lambda b, j: (b, j, 0))
    vec = pl.BlockSpec((1, D), lambda b, j: (0, 0))
    return pl.pallas_call(
        functools.partial(_moe_out_body, alpha=alpha),
        grid=(B, R // tm),
        in_specs=[pl.BlockSpec((tm * wpt, LANES), lambda b, j: (b * rt + j, 0)),
                  pl.BlockSpec((tm * wpt, LANES), lambda b, j: (B * rt + b * rt + j, 0)),
                  tok, pl.BlockSpec((1, 1, 6, D), lambda b, j: (b, seg(j), 0, 0)), vec, vec],
        out_specs=tok,
        out_shape=jax.ShapeDtypeStruct((B, R, D), F32),
        compiler_params=_cparams(("arbitrary", "arbitrary")),
        name="moe_out_ln",
    )(y, y, xs, modt, ln_g.reshape(1, D), ln_b.reshape(1, D))


def _hier_moe_experts(h, ids, wts, B, R, D, w_gate, w_up, w_down, layer):
    N = B * R
    wpt = D // 2 // LANES
    plan = _moe_plan(ids.reshape(N, LANES)[:, :2], wts.reshape(N, LANES)[:, :2], N, wpt)
    return _moe_experts(h.reshape(N * wpt, LANES), *plan, w_gate, w_up, w_down, layer)


GLA_PROJ_COL_TILE = 1024


def _gla_proj_body(y0_ref, y1_ref, x_ref, mod0_ref, lng_ref, lnb_ref, mod_ref, w_ref, w1_ref, w2_ref, b_ref,
                   xo_ref, o_ref, g_ref, *, alpha, q_cols, q_scale):
    xs = _moe_combine_ln(y0_ref, y1_ref, x_ref, mod0_ref, lng_ref, lnb_ref, alpha)
    xo_ref[0] = xs
    m = mod_ref[0, 0]
    z = (xs * (1.0 + m[1:2]) + m[0:1]).astype(BF16)
    ct = GLA_PROJ_COL_TILE
    for c0 in range(0, o_ref.shape[2], ct):
        acc = jnp.dot(z, w_ref[:, c0:c0 + ct], preferred_element_type=F32)
        if c0 < q_cols:
            acc = acc * q_scale
        o_ref[0, :, c0:c0 + ct] = acc.astype(BF16)
    t = jnp.dot(z, w1_ref[...], preferred_element_type=F32)
    pre = jnp.dot(t.astype(BF16), w2_ref[...], preferred_element_type=F32) + b_ref[...]
    g_ref[0] = (jnp.minimum(pre, 0.0) - jnp.log(1.0 + jnp.exp(-jnp.abs(pre)))) * (1.0 / GLA_TAU)


def _gla_proj(y, xs, modt0, ln_g, ln_b, modt, w_bf16, w1, w2, gb, n_ctx_tiles, alpha, q_cols, q_scale):
    B, S, D = xs.shape
    NO = w_bf16.shape[1]
    rank = w1.shape[2]
    kd = w2.shape[2]
    tm = TOKEN_TILE
    rt = S // tm
    wpt = D // 2 // LANES
    assert q_cols % GLA_PROJ_COL_TILE == 0 and NO % GLA_PROJ_COL_TILE == 0
    w1c = jnp.zeros((D, LANES), F32).at[:, :rank].set(w1[0]).at[:, rank:2 * rank].set(w1[1]).astype(BF16)
    w2c = (jnp.zeros((LANES, 2 * kd), F32).at[:rank, :kd].set(w2[0]).at[rank:2 * rank, kd:].set(w2[1])
           .astype(BF16))
    seg = lambda j: jnp.where(j >= n_ctx_tiles, 1, 0)
    const = lambda shape: pl.BlockSpec(shape, lambda b, j: (0, 0))
    tok = pl.BlockSpec((1, tm, D), lambda b, j: (b, j, 0))
    mod = pl.BlockSpec((1, 1, 6, D), lambda b, j: (b, seg(j), 0, 0))
    return pl.pallas_call(
        functools.partial(_gla_proj_body, alpha=alpha, q_cols=q_cols, q_scale=q_scale),
        grid=(B, S // tm),
        in_specs=[
            pl.BlockSpec((tm * wpt, LANES), lambda b, j: (b * rt + j, 0)),
            pl.BlockSpec((tm * wpt, LANES), lambda b, j: (B * rt + b * rt + j, 0)),
            tok, mod, const((1, D)), const((1, D)), mod,
            pl.BlockSpec((D, NO), lambda b, j: (0, 0), pipeline_mode=pl.Buffered(1)),
            const((D, LANES)), const((LANES, 2 * kd)), const((1, 2 * kd)),
        ],
        out_specs=[tok, pl.BlockSpec((1, tm, NO), lambda b, j: (b, j, 0)),
                   pl.BlockSpec((1, tm, 2 * kd), lambda b, j: (b, j, 0))],
        out_shape=[jax.ShapeDtypeStruct((B, S, D), F32), jax.ShapeDtypeStruct((B, S, NO), BF16),
                   jax.ShapeDtypeStruct((B, S, 2 * kd), F32)],
        compiler_params=_cparams(("arbitrary", "arbitrary")),
        name="gla_proj",
    )(y, y, xs, modt0, ln_g.reshape(1, D), ln_b.reshape(1, D), modt, w_bf16, w1c, w2c, gb.reshape(1, 2 * kd))


GLA_LEVELS = (32, 16, 8, 4, 2, 1)


def _gla_constants():
    L = GLA_CHUNK
    t = np.arange(L)[:, None]
    u = np.arange(L)[None, :]
    mats, masks = [], []
    for bwd in (False, True):
        blocks = []
        blocks.append((u >= t) if bwd else (u <= t))
        blocks.append((u < t) if bwd else (u > t))
        q_lv, k_lv, m_lv = [], [], []
        for m in GLA_LEVELS:
            base = (t // (2 * m)) * (2 * m)
            ubase = (u // (2 * m)) * (2 * m)
            t_hi = (t % (2 * m)) >= m
            if not bwd:
                r = base + m - 1
                q_lv.append(t_hi & (u > r) & (u <= t))
                k_lv.append(~t_hi & (u > t) & (u <= r))
                m_lv.append(t_hi & ((u % (2 * m)) < m) & (base == ubase))
            else:
                r = base + m
                q_lv.append(~t_hi & (u >= t) & (u < r))
                k_lv.append(t_hi & (u >= r) & (u < t))
                m_lv.append(~t_hi & ((u % (2 * m)) >= m) & (base == ubase))
        blocks += [ql | kl for ql, kl in zip(q_lv, k_lv)]
        blocks.append(np.ones((8, L), bool))
        a = np.concatenate(blocks, axis=0).astype(np.float32)
        mats.append(np.concatenate([a, a, a], axis=1))
        m_lv.append(t == u)
        masks.append(np.concatenate(m_lv, axis=0).astype(np.float32))
    return np.stack(mats), np.stack(masks)


LOG2_E = 1.4426950408889634


def _gla_chunk_step(q, k, v, g, a_ref, mk_ref, st_ref, hd):
    L = GLA_CHUNK
    nlv = len(GLA_LEVELS)
    g = g * LOG2_E
    g_hi = g.astype(BF16)
    r1 = g - g_hi.astype(F32)
    g_mid = r1.astype(BF16)
    g_lo = (r1 - g_mid.astype(F32)).astype(BF16)
    gs = jnp.concatenate([g_hi, g_mid, g_lo], axis=0)
    ex = jnp.exp2(jnp.dot(a_ref[...], gs, preferred_element_type=F32))
    qf = q.astype(F32)
    kf = k.astype(F32)
    q_inter = (qf * ex[0:L]).astype(BF16)
    k_state = (kf * ex[L:2 * L]).astype(BF16)
    dec = ex[(2 + nlv) * L:(2 + nlv) * L + 1]
    nt = (((1,), (1,)), ((), ()))
    att = lax.dot_general(q, k, nt, preferred_element_type=F32) * mk_ref[nlv * L:(nlv + 1) * L]
    for lv in range(nlv):
        f = ex[(2 + lv) * L:(3 + lv) * L]
        ql = (qf * f).astype(BF16)
        kl = (kf * f).astype(BF16)
        att = att + lax.dot_general(ql, kl, nt, preferred_element_type=F32) * mk_ref[lv * L:(lv + 1) * L]
    st = st_ref[hd]
    o = jnp.dot(att.astype(BF16), v, preferred_element_type=F32)
    o = o + lax.dot_general(q_inter, st.astype(BF16), nt, preferred_element_type=F32)
    upd = lax.dot_general(v, k_state, (((0,), (0,)), ((), ())), preferred_element_type=F32)
    st_ref[hd] = st * dec + upd
    return o


def _gla_scan_body(qf_ref, kf_ref, vf_ref, gf_ref, qb_ref, kb_ref, vb_ref, gb_ref,
                   af_ref, ab_ref, mf_ref, mb_ref, of_ref, ob_ref, stf_ref, stb_ref):
    @pl.when(pl.program_id(1) == 0)
    def _():
        stf_ref[...] = jnp.zeros(stf_ref.shape, stf_ref.dtype)
        stb_ref[...] = jnp.zeros(stb_ref.shape, stb_ref.dtype)

    _, dv, dk = stf_ref.shape
    for hd in range(GLA_HEADS):
        ks = slice(hd * dk, (hd + 1) * dk)
        vs = slice(hd * dv, (hd + 1) * dv)
        o = _gla_chunk_step(qf_ref[0, :, ks], kf_ref[0, :, ks], vf_ref[0, :, vs], gf_ref[0, :, ks],
                            af_ref, mf_ref, stf_ref, hd)
        of_ref[0, :, vs] = o.astype(of_ref.dtype)
        o = _gla_chunk_step(qb_ref[0, :, ks], kb_ref[0, :, ks], vb_ref[0, :, vs], gb_ref[0, :, ks],
                            ab_ref, mb_ref, stb_ref, hd)
        ob_ref[0, :, vs] = o.astype(ob_ref.dtype)


def _gla_scan(proj, lg, C, T):
    B, S, NO = proj.shape
    kd = lg.shape[2] // 2
    D = NO - 2 * kd
    D = D // 2
    dk = kd // GLA_HEADS
    dv = D // GLA_HEADS
    L = GLA_CHUNK
    n_ctx = C // L
    n_all = S // L
    amat, masks = _gla_constants()
    amat = jnp.asarray(amat, BF16)
    masks = jnp.asarray(masks, F32)

    def fwd(c):
        return c

    def bwd(c):
        return jnp.where(c < n_ctx, n_ctx - 1 - c, n_ctx + n_all - 1 - c)

    assert 2 * kd == D

    def specs(chunk, d):
        return [
            pl.BlockSpec((1, L, kd), lambda b, c: (b, chunk(c), 0)),
            pl.BlockSpec((1, L, kd), lambda b, c: (b, chunk(c), 1)),
            pl.BlockSpec((1, L, D), lambda b, c: (b, chunk(c), 1)),
            pl.BlockSpec((1, L, kd), lambda b, c: (b, chunk(c), d)),
        ]

    const = lambda arr: pl.BlockSpec(arr.shape[1:], lambda b, c: (0, 0))
    out = jax.ShapeDtypeStruct((B, S, D), BF16)
    state = pltpu.VMEM((GLA_HEADS, dv, dk), F32)
    return pl.pallas_call(
        _gla_scan_body,
        grid=(B, n_all),
        in_specs=specs(fwd, 0) + specs(bwd, 1) + [const(amat), const(amat), const(masks), const(masks)],
        out_specs=[pl.BlockSpec((1, L, D), lambda b, c: (b, fwd(c), 0)),
                   pl.BlockSpec((1, L, D), lambda b, c: (b, bwd(c), 0))],
        out_shape=[out, out],
        scratch_shapes=[state, state],
        compiler_params=_cparams(("arbitrary", "arbitrary")),
        name="gla_scan",
    )(proj, proj, proj, lg, proj, proj, proj, lg, amat[0], amat[1], masks[0], masks[1])


def kernel(x, c, ctx, c_ctx, mod_w, mod_b, ln_g, ln_b, attn_w_qkv, attn_w_o, attn_sink,
           gla_w_in, gla_gate_w1, gla_gate_w2, gla_gate_b, gla_norm_g, gla_w_o,
           moe_group_w, moe_group_b, moe_router_w, moe_router_b, moe_w_gate, moe_w_up, moe_w_down):
    B, T, D = x.shape
    C = ctx.shape[1]
    depth = mod_w.shape[0]
    assert depth == 2 and D == N_HEADS * HEAD_DIM
    assert C % TOKEN_TILE == 0 and T % TOKEN_TILE == 0 and T % GRID_W == 0
    alpha = (2 * depth) ** 0.25
    n_ctx_tiles = C // TOKEN_TILE

    m_rows = -(-(B + 1) // 8) * 8
    cond = jnp.zeros((m_rows, D), F32).at[:B].set(c).at[B].set(c_ctx)
    mods = _adaln_mods(cond, mod_w, mod_b)

    def mod_table(l):
        lat = mods[l, :B].reshape(B, 1, 6, D)
        cm = jnp.broadcast_to(mods[l, B].reshape(1, 1, 6, D), (B, 1, 6, D))
        return jnp.concatenate([cm, lat], axis=1)

    modt = mod_table(0)
    cos, sin = _rope_tables(C, T)
    qkv = _qkv_proj(ctx, x, modt, attn_w_qkv[0].astype(BF16), cos, sin, n_ctx_tiles)
    attn = _attention(qkv, attn_sink[0], C, T)
    router = _router_params(moe_group_w[0], moe_group_b[0], moe_router_w[0], moe_router_b[0])
    xs, h, ids, wts = _attn_out(attn, ctx, x, modt, attn_w_o[0].astype(BF16), ln_g[0, 0], ln_b[0, 0], router,
                                n_ctx_tiles, alpha)
    y = _hier_moe_experts(h, ids, wts, B, C + T, D, moe_w_gate, moe_w_up, moe_w_down, 0)

    modt1 = mod_table(1)
    kd = gla_gate_w2.shape[3]
    dk = kd // GLA_HEADS
    xs, proj, lg = _gla_proj(y, xs, modt, ln_g[0, 1], ln_b[0, 1], modt1, gla_w_in[0].astype(BF16),
                             gla_gate_w1[0], gla_gate_w2[0], gla_gate_b[0], n_ctx_tiles, alpha, kd, dk ** -0.5)
    o2 = _gla_scan(proj, lg, C, T)
    router = _router_params(moe_group_w[1], moe_group_b[1], moe_router_w[1], moe_router_b[1])
    xl, h, ids, wts = _gla_out(o2, proj, gla_norm_g[0], xs, modt1, gla_w_o[0].astype(BF16), ln_g[1, 0],
                               ln_b[1, 0], router, n_ctx_tiles, T, alpha)
    y = _hier_moe_experts(h, ids, wts, B, T, D, moe_w_gate, moe_w_up, moe_w_down, 1)
    return _moe_out(y, xl, modt1, ln_g[1, 1], ln_b[1, 1], -1, alpha)
```

```python
import functools

import numpy as np
import jax
import jax.numpy as jnp
from jax import lax
from jax.experimental import pallas as pl
from jax.experimental.pallas import tpu as pltpu

F32 = jnp.float32
BF16 = jnp.bfloat16

N_HEADS = 16
N_KV_HEADS = 4
HEAD_DIM = 128
GROUP = N_HEADS // N_KV_HEADS
WINDOW = 128
GRID_W = 64
ROPE_BASE = 10000.0
GLA_HEADS = 4
GLA_TAU = 16.0
GLA_CHUNK = 64
MOE_GROUPS = 4
MOE_EXPERTS_PER_GROUP = 8
N_EXPERTS = MOE_GROUPS * MOE_EXPERTS_PER_GROUP
LN_EPS = 1e-5
RMS_EPS = 1e-6

V7X_VMEM_LIMIT_BYTES = 56 * 1024 * 1024
LANES = 128
TOKEN_TILE = 256
MOE_TILE = 256
MODS_COL_TILE = 1024


def _cparams(sem):
    return pltpu.CompilerParams(dimension_semantics=sem, vmem_limit_bytes=V7X_VMEM_LIMIT_BYTES)


def _silu(v):
    return v / (1.0 + jnp.exp(-v))


def _pack_bf16_pairs(v):
    n = v.shape[1] // 2
    lo = pltpu.bitcast(v[:, :n].astype(BF16).astype(F32), jnp.uint32)
    hi = pltpu.bitcast(v[:, n:].astype(BF16).astype(F32), jnp.uint32)
    return hi | (lo >> 16)


def _unpack_bf16_pairs(w):
    lo = pltpu.bitcast(w << 16, F32)
    hi = pltpu.bitcast(w & jnp.uint32(0xFFFF0000), F32)
    return lo, hi


SUBLANES = 8


def _store_token_tiles(ref, v):
    rows, width = v.shape
    n = width // LANES
    for s in range(n):
        ref[pl.ds(s, rows, stride=n), :] = v[:, s * LANES:(s + 1) * LANES]


def _load_token_tiles(ref, rows):
    n = ref.shape[0] // rows
    return jnp.concatenate([ref[pl.ds(s, rows, stride=n), :] for s in range(n)], axis=1)


def _mods_body(cond_ref, w_ref, b_ref, o_ref):
    a = _silu(cond_ref[...]).astype(BF16)
    o_ref[0] = jnp.dot(a, w_ref[0].astype(BF16), preferred_element_type=F32) + b_ref[0]


def _adaln_mods(cond, mod_w, mod_b):
    L, D, N6 = mod_w.shape
    M = cond.shape[0]
    tn = MODS_COL_TILE
    return pl.pallas_call(
        _mods_body,
        grid=(L, N6 // tn),
        in_specs=[
            pl.BlockSpec((M, D), lambda l, j: (0, 0)),
            pl.BlockSpec((1, D, tn), lambda l, j: (l, 0, j)),
            pl.BlockSpec((1, 1, tn), lambda l, j: (l, 0, j)),
        ],
        out_specs=pl.BlockSpec((1, M, tn), lambda l, j: (l, 0, j)),
        out_shape=jax.ShapeDtypeStruct((L, M, N6), F32),
        compiler_params=_cparams(("arbitrary", "arbitrary")),
        name="adaln_mods",
    )(cond, mod_w, mod_b.reshape(L, 1, N6))


def _stream_rows(c_ref, x_ref, n_ctx_tiles):
    return jnp.where(pl.program_id(1) < n_ctx_tiles, c_ref[0], x_ref[0])


def _stream_specs(tm, D, n_ctx_tiles):
    return [pl.BlockSpec((1, tm, D), lambda b, j: (b, jnp.minimum(j, n_ctx_tiles - 1), 0)),
            pl.BlockSpec((1, tm, D), lambda b, j: (b, jnp.maximum(j - n_ctx_tiles, 0), 0))]


def _qkv_body(c_ref, x_ref, mod_ref, w_ref, cos_ref, sin_ref, o_ref, *, n_ctx_tiles, n_rot_heads, n_q_heads,
              scale):
    m = mod_ref[0, 0]
    h = (_stream_rows(c_ref, x_ref, n_ctx_tiles) * (1.0 + m[1:2]) + m[0:1]).astype(BF16)
    acc = jnp.dot(h, w_ref[...], preferred_element_type=F32)
    cos = cos_ref[...]
    sin = sin_ref[...]
    lane = lax.broadcasted_iota(jnp.int32, cos.shape, 1)
    first = (lane & 32) == 0
    for hd in range(n_rot_heads):
        y = acc[:, hd * HEAD_DIM:(hd + 1) * HEAD_DIM]
        partner = jnp.where(first, pltpu.roll(y, HEAD_DIM - 32, 1), pltpu.roll(y, 32, 1))
        r = y * cos + partner * sin
        if hd < n_q_heads:
            r = r * scale
        o_ref[0, :, hd * HEAD_DIM:(hd + 1) * HEAD_DIM] = r.astype(BF16)
    rest = n_rot_heads * HEAD_DIM
    o_ref[0, :, rest:] = acc[:, rest:].astype(BF16)


def _rope_tables(C, T):
    half = HEAD_DIM // 2
    pos = np.arange(T)
    inv_freq = ROPE_BASE ** (-np.arange(0, half, 2, dtype=np.float32) / half)
    ang_r = (pos // GRID_W).astype(np.float32)[:, None] * inv_freq
    ang_c = (pos % GRID_W).astype(np.float32)[:, None] * inv_freq
    ang_r = jnp.asarray(ang_r, F32)
    ang_c = jnp.asarray(ang_c, F32)
    cos = jnp.concatenate([jnp.cos(ang_r)] * 2 + [jnp.cos(ang_c)] * 2, axis=-1)
    sin = jnp.concatenate([-jnp.sin(ang_r), jnp.sin(ang_r), -jnp.sin(ang_c), jnp.sin(ang_c)], axis=-1)
    cos = jnp.concatenate([jnp.ones((C, HEAD_DIM), F32), cos], axis=0)
    sin = jnp.concatenate([jnp.zeros((C, HEAD_DIM), F32), sin], axis=0)
    return cos, sin


def _qkv_proj(ctx, x, modt, w_bf16, cos, sin, n_ctx_tiles):
    B, T, D = x.shape
    S = ctx.shape[1] + T
    NO = w_bf16.shape[1]
    tm = TOKEN_TILE
    body = functools.partial(_qkv_body, n_ctx_tiles=n_ctx_tiles, n_rot_heads=N_HEADS + N_KV_HEADS,
                             n_q_heads=N_HEADS, scale=HEAD_DIM ** -0.5)
    seg = lambda j: jnp.where(j >= n_ctx_tiles, 1, 0)
    return pl.pallas_call(
        body,
        grid=(B, S // tm),
        in_specs=_stream_specs(tm, D, n_ctx_tiles) + [
            pl.BlockSpec((1, 1, 6, D), lambda b, j: (b, seg(j), 0, 0)),
            pl.BlockSpec((D, NO), lambda b, j: (0, 0)),
            pl.BlockSpec((tm, HEAD_DIM), lambda b, j: (j, 0)),
            pl.BlockSpec((tm, HEAD_DIM), lambda b, j: (j, 0)),
        ],
        out_specs=pl.BlockSpec((1, tm, NO), lambda b, j: (b, j, 0)),
        out_shape=jax.ShapeDtypeStruct((B, S, NO), BF16),
        compiler_params=_cparams(("arbitrary", "arbitrary")),
        name="qkv_rope",
    )(ctx, x, modt, w_bf16, cos, sin)


ATTN_KV_PER_STEP = 4


def _attn_one_head(sink_ref, head, q, kc, vc, k_win, v_win, n, n_lat_blk, latent):
    blk = WINDOW
    rows = GROUP * blk
    row = lax.broadcasted_iota(jnp.int32, (rows, 1), 0)
    sink = jnp.full((rows, 1), sink_ref[head * GROUP + GROUP - 1], F32)
    for g in range(GROUP - 2, -1, -1):
        sink = jnp.where(row < (g + 1) * blk, sink_ref[head * GROUP + g], sink)
    nt = (((1,), (1,)), ((), ()))
    s_c = lax.dot_general(q, kc, nt, preferred_element_type=F32)
    scores = [(s_c, vc)]
    if latent:
        a = lax.broadcasted_iota(jnp.int32, (rows, blk), 0) % blk
        key = lax.broadcasted_iota(jnp.int32, (rows, blk), 1)
        s0 = lax.dot_general(q, k_win[0], nt, preferred_element_type=F32)
        s0 = jnp.where((key >= a) & (n > 0), s0, -jnp.inf)
        s1 = lax.dot_general(q, k_win[1], nt, preferred_element_type=F32)
        s2 = lax.dot_general(q, k_win[2], nt, preferred_element_type=F32)
        s2 = jnp.where((key <= a) & (n < n_lat_blk - 1), s2, -jnp.inf)
        scores += [(s0, v_win[0]), (s1, v_win[1]), (s2, v_win[2])]
    lane_tiles = [s[:, j * LANES:(j + 1) * LANES] for s, _ in scores for j in range(s.shape[1] // LANES)]
    m_el = lane_tiles[0]
    for t in lane_tiles[1:]:
        m_el = jnp.maximum(m_el, t)
    m = jnp.maximum(jnp.max(m_el, axis=1, keepdims=True), sink)
    acc = jnp.zeros((rows, HEAD_DIM), F32)
    l_el = jnp.zeros((rows, LANES), F32)
    for s, v in scores:
        p = jnp.exp(s - m)
        for j in range(s.shape[1] // LANES):
            l_el = l_el + p[:, j * LANES:(j + 1) * LANES]
        acc = acc + jnp.dot(p.astype(BF16), v, preferred_element_type=F32)
    denom = jnp.exp(sink - m) + jnp.sum(l_el, axis=1, keepdims=True)
    return acc / denom


def _attn_body(sink_ref, q_ref, kc_ref, vc_ref, k0_ref, k1_ref, k2_ref, v0_ref, v1_ref, v2_ref, o_ref,
               *, n_ctx_blk, n_lat_blk):
    kh0 = pl.program_id(1) * ATTN_KV_PER_STEP
    qb = pl.program_id(2)
    blk = WINDOW
    gw = GROUP * HEAD_DIM

    def run(latent):
        n = qb - n_ctx_blk
        for i in range(ATTN_KV_PER_STEP):
            hs = slice(i * HEAD_DIM, (i + 1) * HEAD_DIM)
            q = jnp.concatenate([q_ref[0, :, i * gw + g * HEAD_DIM:i * gw + (g + 1) * HEAD_DIM]
                                 for g in range(GROUP)], axis=0)
            k_win = [r[0, :, hs] for r in (k0_ref, k1_ref, k2_ref)] if latent else None
            v_win = [r[0, :, hs] for r in (v0_ref, v1_ref, v2_ref)] if latent else None
            o = _attn_one_head(sink_ref, kh0 + i, q, kc_ref[0, :, hs], vc_ref[0, :, hs], k_win, v_win,
                               n, n_lat_blk, latent)
            for g in range(GROUP):
                o_ref[0, :, i * gw + g * HEAD_DIM:i * gw + (g + 1) * HEAD_DIM] = (
                    o[g * blk:(g + 1) * blk].astype(o_ref.dtype))

    @pl.when(qb < n_ctx_blk)
    def _():
        run(False)

    @pl.when(qb >= n_ctx_blk)
    def _():
        run(True)


def _attention(qkv, sink, C, T):
    B, S, _ = qkv.shape
    blk = WINDOW
    n_ctx_blk = C // blk
    n_lat_blk = T // blk
    kvs = ATTN_KV_PER_STEP
    kvw = kvs * HEAD_DIM
    k_col = N_HEADS // kvs
    v_col = (N_HEADS + N_KV_HEADS) // kvs

    def win(j, col):
        def im(b, kh, qb):
            n = jnp.maximum(qb - n_ctx_blk, 0)
            return (b, n_ctx_blk + jnp.clip(n + j - 1, 0, n_lat_blk - 1), col + kh)
        return im

    kv_blk = (1, blk, kvw)
    q_blk = (1, blk, kvs * GROUP * HEAD_DIM)
    body = functools.partial(_attn_body, n_ctx_blk=n_ctx_blk, n_lat_blk=n_lat_blk)
    return pl.pallas_call(
        body,
        grid=(B, N_KV_HEADS // kvs, S // blk),
        in_specs=[
            pl.BlockSpec(memory_space=pltpu.SMEM),
            pl.BlockSpec(q_blk, lambda b, kh, qb: (b, qb, kh)),
            pl.BlockSpec((1, C, kvw), lambda b, kh, qb: (b, 0, k_col + kh)),
            pl.BlockSpec((1, C, kvw), lambda b, kh, qb: (b, 0, v_col + kh)),
            pl.BlockSpec(kv_blk, win(0, k_col)), pl.BlockSpec(kv_blk, win(1, k_col)),
            pl.BlockSpec(kv_blk, win(2, k_col)),
            pl.BlockSpec(kv_blk, win(0, v_col)), pl.BlockSpec(kv_blk, win(1, v_col)),
            pl.BlockSpec(kv_blk, win(2, v_col)),
        ],
        out_specs=pl.BlockSpec(q_blk, lambda b, kh, qb: (b, qb, kh)),
        out_shape=jax.ShapeDtypeStruct((B, S, N_HEADS * HEAD_DIM), BF16),
        compiler_params=_cparams(("arbitrary", "arbitrary", "arbitrary")),
        name="window_attention",
    )(sink, qkv, qkv, qkv, qkv, qkv, qkv, qkv, qkv, qkv)


def _layer_norm_rows(r, g, b):
    mu = jnp.mean(r, axis=-1, keepdims=True)
    rc = r - mu
    var = jnp.mean(rc * rc, axis=-1, keepdims=True)
    return rc * lax.rsqrt(var + LN_EPS) * g + b


def _route(logits):
    lane = lax.broadcasted_iota(jnp.int32, logits.shape, 1)
    neg = -jnp.inf

    def first_max(vals):
        mx = jnp.max(vals, axis=1, keepdims=True)
        idx = jnp.min(jnp.where(vals == mx, lane, LANES), axis=1, keepdims=True)
        return mx, idx

    gmask = lane < MOE_GROUPS
    gl = jnp.where(gmask, logits, neg)
    gmax, gidx = first_max(gl)
    gsum = jnp.sum(jnp.where(gmask, jnp.exp(gl - gmax), 0.0), axis=1, keepdims=True)
    g_w = 1.0 / gsum
    lo = MOE_GROUPS + gidx * MOE_EXPERTS_PER_GROUP
    el = jnp.where((lane >= lo) & (lane < lo + MOE_EXPERTS_PER_GROUP), logits, neg)
    v1, i1 = first_max(el)
    v2, i2 = first_max(jnp.where(lane == i1, neg, el))
    e2 = jnp.exp(v2 - v1)
    w1 = g_w / (1.0 + e2)
    w2 = g_w * e2 / (1.0 + e2)
    ids = jnp.where(lane == 0, i1 - MOE_GROUPS, jnp.where(lane == 1, i2 - MOE_GROUPS, 0))
    wts = jnp.where(lane == 0, w1, jnp.where(lane == 1, w2, 0.0))
    return ids, wts


MIX_SUB_TILES = 1


def _mix_out_body(*refs, alpha, gla, n_ctx_tiles):
    wr_ref, br_ref, xo_ref, h_ref, id_ref, wt_ref = refs[-6:]
    refs = refs[:-6]
    tm = xo_ref.shape[1]
    sub = tm // MIX_SUB_TILES
    wpt = h_ref.shape[1] // tm
    for t in range(MIX_SUB_TILES):
        rs = slice(t * sub, (t + 1) * sub)
        if gla:
            of_ref, ob_ref, r_ref, ng_ref, x_ref, mod_ref, w_ref, lng_ref, lnb_ref = refs
            x = x_ref[0, rs]
            o = of_ref[0, rs].astype(F32) + ob_ref[0, rs].astype(F32)
            dv = ng_ref.shape[1]
            parts = []
            for hd in range(GLA_HEADS):
                oh = o[:, hd * dv:(hd + 1) * dv]
                ms = jnp.mean(oh * oh, axis=-1, keepdims=True)
                parts.append(oh * lax.rsqrt(ms + RMS_EPS) * ng_ref[...])
            a = (jnp.concatenate(parts, axis=1) * _silu(r_ref[0, rs].astype(F32))).astype(BF16)
        else:
            a_ref, c_ref, x_ref, mod_ref, w_ref, lng_ref, lnb_ref = refs
            a = a_ref[0, rs]
            x = jnp.where(pl.program_id(1) < n_ctx_tiles, c_ref[0, rs], x_ref[0, rs])
        m = mod_ref[0, 0]
        y = jnp.dot(a, w_ref[...], preferred_element_type=F32)
        xn = _layer_norm_rows(alpha * x + m[2:3] * y, lng_ref[...], lnb_ref[...])
        xo_ref[0, rs] = xn
        h = xn * (1.0 + m[4:5]) + m[3:4]
        _store_token_tiles(h_ref.at[0, pl.ds(t * sub * wpt, sub * wpt)], _pack_bf16_pairs(h))
        ids, wts = _route(jnp.dot(h.astype(BF16), wr_ref[...], preferred_element_type=F32) + br_ref[...])
        id_ref[0, rs] = ids
        wt_ref[0, rs] = wts


def _router_params(wg, bg, we, be):
    D = wg.shape[0]
    pad = LANES - MOE_GROUPS - N_EXPERTS
    wr = jnp.concatenate([wg, we, jnp.zeros((D, pad), F32)], axis=1).astype(BF16)
    br = jnp.concatenate([bg, be, jnp.zeros((pad,), F32)]).reshape(1, LANES)
    return wr, br


def _mix_out_tail(B, R, D, tm):
    tok = pl.BlockSpec((1, tm, D), lambda b, j: (b, j, 0))
    wpt = D // 2 // LANES
    packed = pl.BlockSpec((1, tm * wpt, LANES), lambda b, j: (b, j, 0))
    lane_tile = pl.BlockSpec((1, tm, LANES), lambda b, j: (b, j, 0))
    in_specs = [pl.BlockSpec((D, LANES), lambda b, j: (0, 0)), pl.BlockSpec((1, LANES), lambda b, j: (0, 0))]
    out_specs = [tok, packed, lane_tile, lane_tile]
    out_shape = [jax.ShapeDtypeStruct((B, R, D), F32), jax.ShapeDtypeStruct((B, R * wpt, LANES), jnp.uint32),
                 jax.ShapeDtypeStruct((B, R, LANES), jnp.int32), jax.ShapeDtypeStruct((B, R, LANES), F32)]
    return in_specs, out_specs, out_shape


def _attn_out(attn, ctx, x, modt, w_bf16, ln_g, ln_b, router, n_ctx_tiles, alpha):
    B, S, D = attn.shape
    tm = TOKEN_TILE
    seg = lambda j: jnp.where(j >= n_ctx_tiles, 1, 0)
    tok = pl.BlockSpec((1, tm, D), lambda b, j: (b, j, 0))
    vec = pl.BlockSpec((1, D), lambda b, j: (0, 0))
    r_in, out_specs, out_shape = _mix_out_tail(B, S, D, tm)
    return pl.pallas_call(
        functools.partial(_mix_out_body, alpha=alpha, gla=False, n_ctx_tiles=n_ctx_tiles),
        grid=(B, S // tm),
        in_specs=[tok] + _stream_specs(tm, D, n_ctx_tiles) + [
                  pl.BlockSpec((1, 1, 6, D), lambda b, j: (b, seg(j), 0, 0)),
                  pl.BlockSpec((D, D), lambda b, j: (0, 0)), vec, vec] + r_in,
        out_specs=out_specs,
        out_shape=out_shape,
        compiler_params=_cparams(("arbitrary", "arbitrary")),
        name="attn_out_ln",
    )(attn, ctx, x, modt, w_bf16, ln_g.reshape(1, D), ln_b.reshape(1, D), *router)


def _gla_out(o2, proj, norm_g, xs, modt, w_bf16, ln_g, ln_b, router, n_ctx_tiles, T, alpha):
    B, S, D = xs.shape
    tm = TOKEN_TILE
    dv = D // GLA_HEADS
    r_col = proj.shape[2] // D - 1
    off = n_ctx_tiles
    vec = pl.BlockSpec((1, D), lambda b, j: (0, 0))
    r_in, out_specs, out_shape = _mix_out_tail(B, T, D, tm)
    return pl.pallas_call(
        functools.partial(_mix_out_body, alpha=alpha, gla=True, n_ctx_tiles=n_ctx_tiles),
        grid=(B, T // tm),
        in_specs=[
            pl.BlockSpec((1, tm, D), lambda b, j: (b, j + off, 0)),
            pl.BlockSpec((1, tm, D), lambda b, j: (b, j + off, 0)),
            pl.BlockSpec((1, tm, D), lambda b, j: (b, j + off, r_col)),
            pl.BlockSpec((1, dv), lambda b, j: (0, 0)),
            pl.BlockSpec((1, tm, D), lambda b, j: (b, j + off, 0)),
            pl.BlockSpec((1, 1, 6, D), lambda b, j: (b, 1, 0, 0)),
            pl.BlockSpec((D, D), lambda b, j: (0, 0)), vec, vec,
        ] + r_in,
        out_specs=out_specs,
        out_shape=out_shape,
        compiler_params=_cparams(("arbitrary", "arbitrary")),
        name="gla_out_ln",
    )(o2[0], o2[1], proj, norm_g.reshape(1, dv), xs, modt, w_bf16, ln_g.reshape(1, D), ln_b.reshape(1, D),
      *router)


def _moe_body(te_ref, nv_ref, nu_ref, tok_ref, tokn_ref, dst_ref, rw_ref, h_hbm, wg_ref, wu_ref, wd_ref,
              y_hbm, xbuf, ybuf, wgb, wub, wdb, gsem, ssem):
    i = pl.program_id(0)
    n_used = nu_ref[0]

    unroll = 8
    tm = rw_ref.shape[0]
    wpt = xbuf.shape[1] // tm

    def token_rows(ref, first_word_row):
        return ref.at[pl.ds(pl.multiple_of(first_word_row, SUBLANES), wpt), :]

    def gather(idx_ref, n_rows, s):
        def issue(r8, c):
            for u in range(unroll):
                r = r8 * unroll + u
                pltpu.make_async_copy(token_rows(h_hbm, idx_ref[0, 0, r]),
                                      token_rows(xbuf.at[s], r * wpt), gsem.at[s]).start()
            return c
        lax.fori_loop(0, n_rows // unroll, issue, 0)

    def gather_wait(n_rows, s):
        n = pl.multiple_of(n_rows * wpt, SUBLANES)
        pltpu.make_async_copy(h_hbm.at[pl.ds(0, n), :], xbuf.at[s, pl.ds(0, n), :], gsem.at[s]).wait()

    def scatter(n_rows, s):
        def issue(r8, c):
            for u in range(unroll):
                r = r8 * unroll + u
                pltpu.make_async_copy(token_rows(ybuf.at[s], r * wpt),
                                      token_rows(y_hbm, dst_ref[0, 0, r]), ssem.at[s]).start()
            return c
        lax.fori_loop(0, n_rows // unroll, issue, 0)

    def scatter_wait(n_rows, s):
        n = pl.multiple_of(n_rows * wpt, SUBLANES)
        pltpu.make_async_copy(ybuf.at[s, pl.ds(0, n), :], y_hbm.at[pl.ds(0, n), :], ssem.at[s]).wait()

    @pl.when(i == 0)
    def _():
        xbuf[...] = jnp.zeros(xbuf.shape, xbuf.dtype)
        spare = y_hbm.shape[0] - 2 * tm * wpt
        for s in range(2):
            cp = pltpu.make_async_copy(xbuf.at[s], y_hbm.at[pl.ds(spare + s * tm * wpt, tm * wpt), :],
                                       ssem.at[s])
            cp.start()
            cp.wait()
        gather(tok_ref, nv_ref[0], 0)

    def step(slot):
        @pl.when(i + 1 < n_used)
        def _():
            gather(tokn_ref, nv_ref[i + 1], 1 - slot)

        @pl.when(i < n_used)
        def _():
            compute(slot)

    def compute(slot):
        nv = nv_ref[i]
        gather_wait(nv, slot)

        @pl.when((i == 0) | (te_ref[i] != te_ref[jnp.maximum(i - 1, 0)]))
        def _():
            wgb[...] = wg_ref[0, 0].astype(BF16)
            wub[...] = wu_ref[0, 0].astype(BF16)
            wdb[...] = wd_ref[0, 0].astype(BF16)

        @pl.when(i >= 2)
        def _():
            scatter_wait(nv_ref[jnp.maximum(i - 2, 0)], slot)

        x = jnp.concatenate(_unpack_bf16_pairs(_load_token_tiles(xbuf.at[slot], tm)), axis=1).astype(BF16)
        g = jnp.dot(x, wgb[...], preferred_element_type=F32)
        u = jnp.dot(x, wub[...], preferred_element_type=F32)
        hid = (_silu(g) * u).astype(BF16)
        y = jnp.dot(hid, wdb[...], preferred_element_type=F32)
        _store_token_tiles(ybuf.at[slot], _pack_bf16_pairs(y * rw_ref[:, 0:1]))
        scatter(nv, slot)

        @pl.when(i == n_used - 1)
        def _():
            @pl.when(i >= 1)
            def _():
                scatter_wait(nv_ref[jnp.maximum(i - 1, 0)], 1 - slot)
            scatter_wait(nv, slot)

    for slot in range(2):
        pl.when(i % 2 == slot)(functools.partial(step, slot))


def _moe_experts(h, tile_expert, tile_valid, n_used, row_tok, row_dst, row_w, w_gate, w_up, w_down, layer):
    _, E, D, Hd = w_gate.shape
    wpt = D // 2 // LANES
    assert wpt % SUBLANES == 0
    N = h.shape[0] // wpt
    tm = MOE_TILE
    n_tiles = tile_expert.shape[0]
    idx_blk = lambda f: pl.BlockSpec((1, 1, tm), f, memory_space=pltpu.SMEM)
    grid_spec = pltpu.PrefetchScalarGridSpec(
        num_scalar_prefetch=3,
        grid=(n_tiles,),
        in_specs=[
            idx_blk(lambda i, te, nv, nu: (i, 0, 0)),
            idx_blk(lambda i, te, nv, nu: (jnp.minimum(i + 1, n_tiles - 1), 0, 0)),
            idx_blk(lambda i, te, nv, nu: (i, 0, 0)),
            pl.BlockSpec((tm, LANES), lambda i, te, nv, nu: (i, 0)),
            pl.BlockSpec(memory_space=pl.ANY),
            pl.BlockSpec((1, 1, D, Hd), lambda i, te, nv, nu: (layer, te[i], 0, 0)),
            pl.BlockSpec((1, 1, D, Hd), lambda i, te, nv, nu: (layer, te[i], 0, 0)),
            pl.BlockSpec((1, 1, Hd, D), lambda i, te, nv, nu: (layer, te[i], 0, 0)),
        ],
        out_specs=pl.BlockSpec(memory_space=pl.ANY),
        scratch_shapes=[
            pltpu.VMEM((2, tm * wpt, LANES), jnp.uint32), pltpu.VMEM((2, tm * wpt, LANES), jnp.uint32),
            pltpu.VMEM((D, Hd), BF16), pltpu.VMEM((D, Hd), BF16), pltpu.VMEM((Hd, D), BF16),
            pltpu.SemaphoreType.DMA((2,)), pltpu.SemaphoreType.DMA((2,)),
        ],
    )
    return pl.pallas_call(
        _moe_body,
        grid_spec=grid_spec,
        out_shape=jax.ShapeDtypeStruct(((2 * N + 2 * tm) * wpt, LANES), jnp.uint32),
        compiler_params=_cparams(("arbitrary",)),
        name="moe_experts",
    )(tile_expert, tile_valid, n_used, row_tok, row_tok, row_dst, row_w, h, w_gate, w_up, w_down)


def _moe_plan(ids, wts, N, wpt):
    tm = MOE_TILE
    E = N_EXPERTS
    A = 2 * N
    n_tiles = A // tm + E
    e_flat = ids.reshape(A)
    counts = jnp.sum((e_flat[:, None] == jnp.arange(E, dtype=jnp.int32)[None, :]).astype(jnp.int32), axis=0)
    fill = (-counts) % tm
    fj = jnp.arange(tm, dtype=jnp.int32)[None, :]
    fe = jnp.arange(E, dtype=jnp.int32)[:, None]
    fkey = jnp.where(fj < fill[:, None], fe, E).reshape(E * tm)
    n_fill = E * tm
    shift = 20
    assert A + n_fill < (1 << shift) and (E + 1) << shift < (1 << 31)
    idx = jnp.arange(A + n_fill, dtype=jnp.int32)
    keys = (jnp.concatenate([e_flat, fkey]) << shift) | idx
    wgt = jnp.concatenate([wts.reshape(A), jnp.zeros((n_fill,), F32)])
    key_s, w_s = lax.sort((keys, wgt), num_keys=1)
    e_s = (key_s >> shift).reshape(n_tiles, tm)
    a_s = (key_s & ((1 << shift) - 1)).reshape(n_tiles, tm)
    real = a_s < A
    n_used = jnp.sum(counts + fill) // tm
    t_idx = jnp.arange(n_tiles, dtype=jnp.int32)
    te = jnp.minimum(e_s[:, 0], E - 1)
    last_e = jnp.max(jnp.where(t_idx < n_used, te, 0))
    te = jnp.where(t_idx < n_used, te, last_e).astype(jnp.int32)
    tile_valid = jnp.sum(real.astype(jnp.int32), axis=1)
    tile_rows = jnp.where(t_idx < n_used, (tile_valid + 7) // 8 * 8, 0).astype(jnp.int32)
    spare = A + (t_idx[:, None] % 2) * tm + fj
    row_tok = (jnp.where(real, a_s >> 1, 0) * wpt).astype(jnp.int32)
    row_dst = (jnp.where(real, (a_s & 1) * N + (a_s >> 1), spare) * wpt).astype(jnp.int32)
    row_w = jnp.broadcast_to(w_s[:, None], (n_tiles * tm, LANES))
    return (te, tile_rows, n_used.reshape(1).astype(jnp.int32),
            row_tok.reshape(n_tiles, 1, tm), row_dst.reshape(n_tiles, 1, tm), row_w)


def _moe_combine_ln(y0_ref, y1_ref, x_ref, mod_ref, lng_ref, lnb_ref, alpha):
    m = mod_ref[0, 0]
    rows = x_ref.shape[1]
    lo0, hi0 = _unpack_bf16_pairs(_load_token_tiles(y0_ref, rows))
    lo1, hi1 = _unpack_bf16_pairs(_load_token_tiles(y1_ref, rows))
    f = jnp.concatenate([lo0 + lo1, hi0 + hi1], axis=1)
    return _layer_norm_rows(alpha * x_ref[0] + m[5:6] * f, lng_ref[...], lnb_ref[...])


def _moe_out_body(y0_ref, y1_ref, x_ref, mod_ref, lng_ref, lnb_ref, o_ref, *, alpha):
    o_ref[0] = _moe_combine_ln(y0_ref, y1_ref, x_ref, mod_ref, lng_ref, lnb_ref, alpha)


def _moe_out(y, xs, modt, ln_g, ln_b, n_ctx_tiles, alpha):
    B, R, D = xs.shape
    tm = TOKEN_TILE
    rt = R // tm
    wpt = D // 2 // LANES
    if n_ctx_tiles < 0:
        seg = lambda j: 1
    else:
        seg = lambda j: jnp.where(j >= n_ctx_tiles, 1, 0)
    tok = pl.BlockSpec((1, tm, D), lambda b, j: (b, j, 0))
    vec = pl.BlockSpec((1, D), lambda b, j: (0, 0))
    return pl.pallas_call(
        functools.partial(_moe_out_body, alpha=alpha),
        grid=(B, R // tm),
        in_specs=[pl.BlockSpec((tm * wpt, LANES), lambda b, j: (b * rt + j, 0)),
                  pl.BlockSpec((tm * wpt, LANES), lambda b, j: (B * rt + b * rt + j, 0)),
                  tok, pl.BlockSpec((1, 1, 6, D), lambda b, j: (b, seg(j), 0, 0)), vec, vec],
        out_specs=tok,
        out_shape=jax.ShapeDtypeStruct((B, R, D), F32),
        compiler_params=_cparams(("arbitrary", "arbitrary")),
        name="moe_out_ln",
    )(y, y, xs, modt, ln_g.reshape(1, D), ln_b.reshape(1, D))


def _hier_moe_experts(h, ids, wts, B, R, D, w_gate, w_up, w_down, layer):
    N = B * R
    wpt = D // 2 // LANES
    plan = _moe_plan(ids.reshape(N, LANES)[:, :2], wts.reshape(N, LANES)[:, :2], N, wpt)
    return _moe_experts(h.reshape(N * wpt, LANES), *plan, w_gate, w_up, w_down, layer)


GLA_PROJ_COL_TILE = 1024


def _gla_proj_body(y0_ref, y1_ref, x_ref, mod0_ref, lng_ref, lnb_ref, mod_ref, w_ref, w1_ref, w2_ref, b_ref,
                   xo_ref, o_ref, g_ref, *, alpha, q_cols, q_scale):
    xs = _moe_combine_ln(y0_ref, y1_ref, x_ref, mod0_ref, lng_ref, lnb_ref, alpha)
    xo_ref[0] = xs
    m = mod_ref[0, 0]
    z = (xs * (1.0 + m[1:2]) + m[0:1]).astype(BF16)
    ct = GLA_PROJ_COL_TILE
    for c0 in range(0, o_ref.shape[2], ct):
        acc = jnp.dot(z, w_ref[:, c0:c0 + ct], preferred_element_type=F32)
        if c0 < q_cols:
            acc = acc * q_scale
        o_ref[0, :, c0:c0 + ct] = acc.astype(BF16)
    t = jnp.dot(z, w1_ref[...], preferred_element_type=F32)
    pre = jnp.dot(t.astype(BF16), w2_ref[...], preferred_element_type=F32) + b_ref[...]
    g_ref[0] = (jnp.minimum(pre, 0.0) - jnp.log(1.0 + jnp.exp(-jnp.abs(pre)))) * (1.0 / GLA_TAU)


def _gla_proj(y, xs, modt0, ln_g, ln_b, modt, w_bf16, w1, w2, gb, n_ctx_tiles, alpha, q_cols, q_scale):
    B, S, D = xs.shape
    NO = w_bf16.shape[1]
    rank = w1.shape[2]
    kd = w2.shape[2]
    tm = TOKEN_TILE
    rt = S // tm
    wpt = D // 2 // LANES
    assert q_cols % GLA_PROJ_COL_TILE == 0 and NO % GLA_PROJ_COL_TILE == 0
    w1c = jnp.zeros((D, LANES), F32).at[:, :rank].set(w1[0]).at[:, rank:2 * rank].set(w1[1]).astype(BF16)
    w2c = (jnp.zeros((LANES, 2 * kd), F32).at[:rank, :kd].set(w2[0]).at[rank:2 * rank, kd:].set(w2[1])
           .astype(BF16))
    seg = lambda j: jnp.where(j >= n_ctx_tiles, 1, 0)
    const = lambda shape: pl.BlockSpec(shape, lambda b, j: (0, 0))
    tok = pl.BlockSpec((1, tm, D), lambda b, j: (b, j, 0))
    mod = pl.BlockSpec((1, 1, 6, D), lambda b, j: (b, seg(j), 0, 0))
    return pl.pallas_call(
        functools.partial(_gla_proj_body, alpha=alpha, q_cols=q_cols, q_scale=q_scale),
        grid=(B, S // tm),
        in_specs=[
            pl.BlockSpec((tm * wpt, LANES), lambda b, j: (b * rt + j, 0)),
            pl.BlockSpec((tm * wpt, LANES), lambda b, j: (B * rt + b * rt + j, 0)),
            tok, mod, const((1, D)), const((1, D)), mod,
            pl.BlockSpec((D, NO), lambda b, j: (0, 0), pipeline_mode=pl.Buffered(1)),
            const((D, LANES)), const((LANES, 2 * kd)), const((1, 2 * kd)),
        ],
        out_specs=[tok, pl.BlockSpec((1, tm, NO), lambda b, j: (b, j, 0)),
                   pl.BlockSpec((1, tm, 2 * kd), lambda b, j: (b, j, 0))],
        out_shape=[jax.ShapeDtypeStruct((B, S, D), F32), jax.ShapeDtypeStruct((B, S, NO), BF16),
                   jax.ShapeDtypeStruct((B, S, 2 * kd), F32)],
        compiler_params=_cparams(("arbitrary", "arbitrary")),
        name="gla_proj",
    )(y, y, xs, modt0, ln_g.reshape(1, D), ln_b.reshape(1, D), modt, w_bf16, w1c, w2c, gb.reshape(1, 2 * kd))


GLA_LEVELS = (32, 16, 8, 4, 2, 1)
GLA_BATCH_PER_STEP = 2


def _gla_constants():
    L = GLA_CHUNK
    t = np.arange(L)[:, None]
    u = np.arange(L)[None, :]
    mats, masks = [], []
    for bwd in (False, True):
        blocks = []
        blocks.append((u >= t) if bwd else (u <= t))
        blocks.append((u < t) if bwd else (u > t))
        q_lv, k_lv, m_lv = [], [], []
        for m in GLA_LEVELS:
            base = (t // (2 * m)) * (2 * m)
            ubase = (u // (2 * m)) * (2 * m)
            t_hi = (t % (2 * m)) >= m
            if not bwd:
                r = base + m - 1
                q_lv.append(t_hi & (u > r) & (u <= t))
                k_lv.append(~t_hi & (u > t) & (u <= r))
                m_lv.append(t_hi & ((u % (2 * m)) < m) & (base == ubase))
            else:
                r = base + m
                q_lv.append(~t_hi & (u >= t) & (u < r))
                k_lv.append(t_hi & (u >= r) & (u < t))
                m_lv.append(~t_hi & ((u % (2 * m)) >= m) & (base == ubase))
        blocks += [ql | kl for ql, kl in zip(q_lv, k_lv)]
        blocks.append(np.ones((8, L), bool))
        a = np.concatenate(blocks, axis=0).astype(np.float32)
        mats.append(np.concatenate([a, a, a], axis=1))
        m_lv.append(t == u)
        masks.append(np.concatenate(m_lv, axis=0).astype(np.float32))
    return np.stack(mats), np.stack(masks)


LOG2_E = 1.4426950408889634


def _gla_chunk_step(q, k, v, g, a_ref, mk_ref, st_ref, hd):
    L = GLA_CHUNK
    nlv = len(GLA_LEVELS)
    g = g * LOG2_E
    g_hi = g.astype(BF16)
    r1 = g - g_hi.astype(F32)
    g_mid = r1.astype(BF16)
    g_lo = (r1 - g_mid.astype(F32)).astype(BF16)
    gs = jnp.concatenate([g_hi, g_mid, g_lo], axis=0)
    ex = jnp.exp2(jnp.dot(a_ref[...], gs, preferred_element_type=F32))
    qf = q.astype(F32)
    kf = k.astype(F32)
    q_inter = (qf * ex[0:L]).astype(BF16)
    k_state = (kf * ex[L:2 * L]).astype(BF16)
    dec = ex[(2 + nlv) * L:(2 + nlv) * L + 1]
    nt = (((1,), (1,)), ((), ()))
    att = lax.dot_general(q, k, nt, preferred_element_type=F32) * mk_ref[nlv * L:(nlv + 1) * L]
    for lv in range(nlv):
        f = ex[(2 + lv) * L:(3 + lv) * L]
        ql = (qf * f).astype(BF16)
        kl = (kf * f).astype(BF16)
        att = att + lax.dot_general(ql, kl, nt, preferred_element_type=F32) * mk_ref[lv * L:(lv + 1) * L]
    st = st_ref[hd]
    o = jnp.dot(att.astype(BF16), v, preferred_element_type=F32)
    o = o + lax.dot_general(q_inter, st.astype(BF16), nt, preferred_element_type=F32)
    upd = lax.dot_general(v, k_state, (((0,), (0,)), ((), ())), preferred_element_type=F32)
    st_ref[hd] = st * dec + upd
    return o


def _gla_scan_body(qf_ref, kf_ref, vf_ref, gf_ref, qb_ref, kb_ref, vb_ref, gb_ref,
                   af_ref, ab_ref, mf_ref, mb_ref, of_ref, ob_ref, stf_ref, stb_ref):
    @pl.when(pl.program_id(1) == 0)
    def _():
        stf_ref[...] = jnp.zeros(stf_ref.shape, stf_ref.dtype)
        stb_ref[...] = jnp.zeros(stb_ref.shape, stb_ref.dtype)

    _, dv, dk = stf_ref.shape
    for bi in range(GLA_BATCH_PER_STEP):
        for hd in range(GLA_HEADS):
            ks = slice(hd * dk, (hd + 1) * dk)
            vs = slice(hd * dv, (hd + 1) * dv)
            st = bi * GLA_HEADS + hd
            o = _gla_chunk_step(qf_ref[bi, :, ks], kf_ref[bi, :, ks], vf_ref[bi, :, vs], gf_ref[bi, :, ks],
                                af_ref, mf_ref, stf_ref, st)
            of_ref[bi, :, vs] = o.astype(of_ref.dtype)
            o = _gla_chunk_step(qb_ref[bi, :, ks], kb_ref[bi, :, ks], vb_ref[bi, :, vs], gb_ref[bi, :, ks],
                                ab_ref, mb_ref, stb_ref, st)
            ob_ref[bi, :, vs] = o.astype(ob_ref.dtype)


def _gla_scan(proj, lg, C, T):
    B, S, NO = proj.shape
    kd = lg.shape[2] // 2
    D = NO - 2 * kd
    D = D // 2
    dk = kd // GLA_HEADS
    dv = D // GLA_HEADS
    L = GLA_CHUNK
    n_ctx = C // L
    n_all = S // L
    amat, masks = _gla_constants()
    amat = jnp.asarray(amat, BF16)
    masks = jnp.asarray(masks, F32)

    def fwd(c):
        return c

    def bwd(c):
        return jnp.where(c < n_ctx, n_ctx - 1 - c, n_ctx + n_all - 1 - c)

    assert 2 * kd == D

    nb = GLA_BATCH_PER_STEP
    assert B % nb == 0

    def specs(chunk, d):
        return [
            pl.BlockSpec((nb, L, kd), lambda b, c: (b, chunk(c), 0)),
            pl.BlockSpec((nb, L, kd), lambda b, c: (b, chunk(c), 1)),
            pl.BlockSpec((nb, L, D), lambda b, c: (b, chunk(c), 1)),
            pl.BlockSpec((nb, L, kd), lambda b, c: (b, chunk(c), d)),
        ]

    const = lambda arr: pl.BlockSpec(arr.shape[1:], lambda b, c: (0, 0))
    out = jax.ShapeDtypeStruct((B, S, D), BF16)
    state = pltpu.VMEM((nb * GLA_HEADS, dv, dk), F32)
    return pl.pallas_call(
        _gla_scan_body,
        grid=(B // nb, n_all),
        in_specs=specs(fwd, 0) + specs(bwd, 1) + [const(amat), const(amat), const(masks), const(masks)],
        out_specs=[pl.BlockSpec((nb, L, D), lambda b, c: (b, fwd(c), 0)),
                   pl.BlockSpec((nb, L, D), lambda b, c: (b, bwd(c), 0))],
        out_shape=[out, out],
        scratch_shapes=[state, state],
        compiler_params=_cparams(("arbitrary", "arbitrary")),
        name="gla_scan",
    )(proj, proj, proj, lg, proj, proj, proj, lg, amat[0], amat[1], masks[0], masks[1])


def kernel(x, c, ctx, c_ctx, mod_w, mod_b, ln_g, ln_b, attn_w_qkv, attn_w_o, attn_sink,
           gla_w_in, gla_gate_w1, gla_gate_w2, gla_gate_b, gla_norm_g, gla_w_o,
           moe_group_w, moe_group_b, moe_router_w, moe_router_b, moe_w_gate, moe_w_up, moe_w_down):
    B, T, D = x.shape
    C = ctx.shape[1]
    depth = mod_w.shape[0]
    assert depth == 2 and D == N_HEADS * HEAD_DIM
    assert C % TOKEN_TILE == 0 and T % TOKEN_TILE == 0 and T % GRID_W == 0
    alpha = (2 * depth) ** 0.25
    n_ctx_tiles = C // TOKEN_TILE

    m_rows = -(-(B + 1) // 8) * 8
    cond = jnp.zeros((m_rows, D), F32).at[:B].set(c).at[B].set(c_ctx)
    mods = _adaln_mods(cond, mod_w, mod_b)

    def mod_table(l):
        lat = mods[l, :B].reshape(B, 1, 6, D)
        cm = jnp.broadcast_to(mods[l, B].reshape(1, 1, 6, D), (B, 1, 6, D))
        return jnp.concatenate([cm, lat], axis=1)

    modt = mod_table(0)
    cos, sin = _rope_tables(C, T)
    qkv = _qkv_proj(ctx, x, modt, attn_w_qkv[0].astype(BF16), cos, sin, n_ctx_tiles)
    attn = _attention(qkv, attn_sink[0], C, T)
    router = _router_params(moe_group_w[0], moe_group_b[0], moe_router_w[0], moe_router_b[0])
    xs, h, ids, wts = _attn_out(attn, ctx, x, modt, attn_w_o[0].astype(BF16), ln_g[0, 0], ln_b[0, 0], router,
                                n_ctx_tiles, alpha)
    y = _hier_moe_experts(h, ids, wts, B, C + T, D, moe_w_gate, moe_w_up, moe_w_down, 0)

    modt1 = mod_table(1)
    kd = gla_gate_w2.shape[3]
    dk = kd // GLA_HEADS
    xs, proj, lg = _gla_proj(y, xs, modt, ln_g[0, 1], ln_b[0, 1], modt1, gla_w_in[0].astype(BF16),
                             gla_gate_w1[0], gla_gate_w2[0], gla_gate_b[0], n_ctx_tiles, alpha, kd, dk ** -0.5)
    o2 = _gla_scan(proj, lg, C, T)
    router = _router_params(moe_group_w[1], moe_group_b[1], moe_router_w[1], moe_router_b[1])
    xl, h, ids, wts = _gla_out(o2, proj, gla_norm_g[0], xs, modt1, gla_w_o[0].astype(BF16), ln_g[1, 0],
                               ln_b[1, 0], router, n_ctx_tiles, T, alpha)
    y = _hier_moe_experts(h, ids, wts, B, T, D, moe_w_gate, moe_w_up, moe_w_down, 1)
    return _moe_out(y, xl, modt1, ln_g[1, 1], ln_b[1, 1], -1, alpha)
```

```python
import functools

import numpy as np
import jax
import jax.numpy as jnp
from jax import lax
from jax.experimental import pallas as pl
from jax.experimental.pallas import tpu as pltpu

F32 = jnp.float32
BF16 = jnp.bfloat16

N_HEADS = 16
N_KV_HEADS = 4
HEAD_DIM = 128
GROUP = N_HEADS // N_KV_HEADS
WINDOW = 128
GRID_W = 64
ROPE_BASE = 10000.0
GLA_HEADS = 4
GLA_TAU = 16.0
GLA_CHUNK = 64
MOE_GROUPS = 4
MOE_EXPERTS_PER_GROUP = 8
N_EXPERTS = MOE_GROUPS * MOE_EXPERTS_PER_GROUP
LN_EPS = 1e-5
RMS_EPS = 1e-6

V7X_VMEM_LIMIT_BYTES = 56 * 1024 * 1024
LANES = 128
TOKEN_TILE = 256
MOE_TILE = 256
MODS_COL_TILE = 1024


def _cparams(sem):
    return pltpu.CompilerParams(dimension_semantics=sem, vmem_limit_bytes=V7X_VMEM_LIMIT_BYTES)


def _silu(v):
    return v / (1.0 + jnp.exp(-v))


def _pack_bf16_pairs(v):
    n = v.shape[1] // 2
    lo = pltpu.bitcast(v[:, :n].astype(BF16).astype(F32), jnp.uint32)
    hi = pltpu.bitcast(v[:, n:].astype(BF16).astype(F32), jnp.uint32)
    return hi | (lo >> 16)


def _unpack_bf16_pairs(w):
    lo = pltpu.bitcast(w << 16, F32)
    hi = pltpu.bitcast(w & jnp.uint32(0xFFFF0000), F32)
    return lo, hi


SUBLANES = 8


def _store_token_tiles(ref, v):
    rows, width = v.shape
    n = width // LANES
    for s in range(n):
        ref[pl.ds(s, rows, stride=n), :] = v[:, s * LANES:(s + 1) * LANES]


def _load_token_tiles(ref, rows):
    n = ref.shape[0] // rows
    return jnp.concatenate([ref[pl.ds(s, rows, stride=n), :] for s in range(n)], axis=1)


def _mods_body(cond_ref, w_ref, b_ref, o_ref):
    a = _silu(cond_ref[...]).astype(BF16)
    o_ref[0] = jnp.dot(a, w_ref[0].astype(BF16), preferred_element_type=F32) + b_ref[0]


def _adaln_mods(cond, mod_w, mod_b):
    L, D, N6 = mod_w.shape
    M = cond.shape[0]
    tn = MODS_COL_TILE
    return pl.pallas_call(
        _mods_body,
        grid=(L, N6 // tn),
        in_specs=[
            pl.BlockSpec((M, D), lambda l, j: (0, 0)),
            pl.BlockSpec((1, D, tn), lambda l, j: (l, 0, j)),
            pl.BlockSpec((1, 1, tn), lambda l, j: (l, 0, j)),
        ],
        out_specs=pl.BlockSpec((1, M, tn), lambda l, j: (l, 0, j)),
        out_shape=jax.ShapeDtypeStruct((L, M, N6), F32),
        compiler_params=_cparams(("arbitrary", "arbitrary")),
        name="adaln_mods",
    )(cond, mod_w, mod_b.reshape(L, 1, N6))


def _stream_rows(c_ref, x_ref, n_ctx_tiles):
    return jnp.where(pl.program_id(1) < n_ctx_tiles, c_ref[0], x_ref[0])


def _stream_specs(tm, D, n_ctx_tiles):
    return [pl.BlockSpec((1, tm, D), lambda b, j: (b, jnp.minimum(j, n_ctx_tiles - 1), 0)),
            pl.BlockSpec((1, tm, D), lambda b, j: (b, jnp.maximum(j - n_ctx_tiles, 0), 0))]


def _qkv_body(c_ref, x_ref, mod_ref, w_ref, cos_ref, sin_ref, o_ref, *, n_ctx_tiles, n_rot_heads, n_q_heads,
              scale):
    m = mod_ref[0, 0]
    h = (_stream_rows(c_ref, x_ref, n_ctx_tiles) * (1.0 + m[1:2]) + m[0:1]).astype(BF16)
    acc = jnp.dot(h, w_ref[...], preferred_element_type=F32)
    cos = cos_ref[...]
    sin = sin_ref[...]
    lane = lax.broadcasted_iota(jnp.int32, cos.shape, 1)
    first = (lane & 32) == 0
    for hd in range(n_rot_heads):
        y = acc[:, hd * HEAD_DIM:(hd + 1) * HEAD_DIM]
        partner = jnp.where(first, pltpu.roll(y, HEAD_DIM - 32, 1), pltpu.roll(y, 32, 1))
        r = y * cos + partner * sin
        if hd < n_q_heads:
            r = r * scale
        o_ref[0, :, hd * HEAD_DIM:(hd + 1) * HEAD_DIM] = r.astype(BF16)
    rest = n_rot_heads * HEAD_DIM
    o_ref[0, :, rest:] = acc[:, rest:].astype(BF16)


def _rope_tables(C, T):
    half = HEAD_DIM // 2
    pos = np.arange(T)
    inv_freq = ROPE_BASE ** (-np.arange(0, half, 2, dtype=np.float32) / half)
    ang_r = (pos // GRID_W).astype(np.float32)[:, None] * inv_freq
    ang_c = (pos % GRID_W).astype(np.float32)[:, None] * inv_freq
    ang_r = jnp.asarray(ang_r, F32)
    ang_c = jnp.asarray(ang_c, F32)
    cos = jnp.concatenate([jnp.cos(ang_r)] * 2 + [jnp.cos(ang_c)] * 2, axis=-1)
    sin = jnp.concatenate([-jnp.sin(ang_r), jnp.sin(ang_r), -jnp.sin(ang_c), jnp.sin(ang_c)], axis=-1)
    cos = jnp.concatenate([jnp.ones((C, HEAD_DIM), F32), cos], axis=0)
    sin = jnp.concatenate([jnp.zeros((C, HEAD_DIM), F32), sin], axis=0)
    return cos, sin


def _qkv_proj(ctx, x, modt, w_bf16, cos, sin, n_ctx_tiles):
    B, T, D = x.shape
    S = ctx.shape[1] + T
    NO = w_bf16.shape[1]
    tm = TOKEN_TILE
    body = functools.partial(_qkv_body, n_ctx_tiles=n_ctx_tiles, n_rot_heads=N_HEADS + N_KV_HEADS,
                             n_q_heads=N_HEADS, scale=HEAD_DIM ** -0.5)
    seg = lambda j: jnp.where(j >= n_ctx_tiles, 1, 0)
    return pl.pallas_call(
        body,
        grid=(B, S // tm),
        in_specs=_stream_specs(tm, D, n_ctx_tiles) + [
            pl.BlockSpec((1, 1, 6, D), lambda b, j: (b, seg(j), 0, 0)),
            pl.BlockSpec((D, NO), lambda b, j: (0, 0)),
            pl.BlockSpec((tm, HEAD_DIM), lambda b, j: (j, 0)),
            pl.BlockSpec((tm, HEAD_DIM), lambda b, j: (j, 0)),
        ],
        out_specs=pl.BlockSpec((1, tm, NO), lambda b, j: (b, j, 0)),
        out_shape=jax.ShapeDtypeStruct((B, S, NO), BF16),
        compiler_params=_cparams(("arbitrary", "arbitrary")),
        name="qkv_rope",
    )(ctx, x, modt, w_bf16, cos, sin)


ATTN_KV_PER_STEP = 4


def _attn_one_head(sink_ref, head, q, kc, vc, k_win, v_win, n, n_lat_blk, latent):
    blk = WINDOW
    rows = GROUP * blk
    row = lax.broadcasted_iota(jnp.int32, (rows, 1), 0)
    sink = jnp.full((rows, 1), sink_ref[head * GROUP + GROUP - 1], F32)
    for g in range(GROUP - 2, -1, -1):
        sink = jnp.where(row < (g + 1) * blk, sink_ref[head * GROUP + g], sink)
    nt = (((1,), (1,)), ((), ()))
    s_c = lax.dot_general(q, kc, nt, preferred_element_type=F32)
    scores = [(s_c, vc)]
    if latent:
        a = lax.broadcasted_iota(jnp.int32, (rows, blk), 0) % blk
        key = lax.broadcasted_iota(jnp.int32, (rows, blk), 1)
        s0 = lax.dot_general(q, k_win[0], nt, preferred_element_type=F32)
        s0 = jnp.where((key >= a) & (n > 0), s0, -jnp.inf)
        s1 = lax.dot_general(q, k_win[1], nt, preferred_element_type=F32)
        s2 = lax.dot_general(q, k_win[2], nt, preferred_element_type=F32)
        s2 = jnp.where((key <= a) & (n < n_lat_blk - 1), s2, -jnp.inf)
        scores += [(s0, v_win[0]), (s1, v_win[1]), (s2, v_win[2])]
    lane_tiles = [s[:, j * LANES:(j + 1) * LANES] for s, _ in scores for j in range(s.shape[1] // LANES)]
    m_el = lane_tiles[0]
    for t in lane_tiles[1:]:
        m_el = jnp.maximum(m_el, t)
    m = jnp.maximum(jnp.max(m_el, axis=1, keepdims=True), sink)
    acc = jnp.zeros((rows, HEAD_DIM), F32)
    l_el = jnp.zeros((rows, LANES), F32)
    for s, v in scores:
        p = jnp.exp(s - m)
        for j in range(s.shape[1] // LANES):
            l_el = l_el + p[:, j * LANES:(j + 1) * LANES]
        acc = acc + jnp.dot(p.astype(BF16), v, preferred_element_type=F32)
    denom = jnp.exp(sink - m) + jnp.sum(l_el, axis=1, keepdims=True)
    return acc / denom


def _attn_body(sink_ref, q_ref, kc_ref, vc_ref, k0_ref, k1_ref, k2_ref, v0_ref, v1_ref, v2_ref, o_ref,
               *, n_ctx_blk, n_lat_blk):
    kh0 = pl.program_id(1) * ATTN_KV_PER_STEP
    qb = pl.program_id(2)
    blk = WINDOW
    gw = GROUP * HEAD_DIM

    def run(latent):
        n = qb - n_ctx_blk
        for i in range(ATTN_KV_PER_STEP):
            hs = slice(i * HEAD_DIM, (i + 1) * HEAD_DIM)
            q = jnp.concatenate([q_ref[0, :, i * gw + g * HEAD_DIM:i * gw + (g + 1) * HEAD_DIM]
                                 for g in range(GROUP)], axis=0)
            k_win = [r[0, :, hs] for r in (k0_ref, k1_ref, k2_ref)] if latent else None
            v_win = [r[0, :, hs] for r in (v0_ref, v1_ref, v2_ref)] if latent else None
            o = _attn_one_head(sink_ref, kh0 + i, q, kc_ref[0, :, hs], vc_ref[0, :, hs], k_win, v_win,
                               n, n_lat_blk, latent)
            for g in range(GROUP):
                o_ref[0, :, i * gw + g * HEAD_DIM:i * gw + (g + 1) * HEAD_DIM] = (
                    o[g * blk:(g + 1) * blk].astype(o_ref.dtype))

    @pl.when(qb < n_ctx_blk)
    def _():
        run(False)

    @pl.when(qb >= n_ctx_blk)
    def _():
        run(True)


def _attention(qkv, sink, C, T):
    B, S, _ = qkv.shape
    blk = WINDOW
    n_ctx_blk = C // blk
    n_lat_blk = T // blk
    kvs = ATTN_KV_PER_STEP
    kvw = kvs * HEAD_DIM
    k_col = N_HEADS // kvs
    v_col = (N_HEADS + N_KV_HEADS) // kvs

    def win(j, col):
        def im(b, kh, qb):
            n = jnp.maximum(qb - n_ctx_blk, 0)
            return (b, n_ctx_blk + jnp.clip(n + j - 1, 0, n_lat_blk - 1), col + kh)
        return im

    kv_blk = (1, blk, kvw)
    q_blk = (1, blk, kvs * GROUP * HEAD_DIM)
    body = functools.partial(_attn_body, n_ctx_blk=n_ctx_blk, n_lat_blk=n_lat_blk)
    return pl.pallas_call(
        body,
        grid=(B, N_KV_HEADS // kvs, S // blk),
        in_specs=[
            pl.BlockSpec(memory_space=pltpu.SMEM),
            pl.BlockSpec(q_blk, lambda b, kh, qb: (b, qb, kh)),
            pl.BlockSpec((1, C, kvw), lambda b, kh, qb: (b, 0, k_col + kh)),
            pl.BlockSpec((1, C, kvw), lambda b, kh, qb: (b, 0, v_col + kh)),
            pl.BlockSpec(kv_blk, win(0, k_col)), pl.BlockSpec(kv_blk, win(1, k_col)),
            pl.BlockSpec(kv_blk, win(2, k_col)),
            pl.BlockSpec(kv_blk, win(0, v_col)), pl.BlockSpec(kv_blk, win(1, v_col)),
            pl.BlockSpec(kv_blk, win(2, v_col)),
        ],
        out_specs=pl.BlockSpec(q_blk, lambda b, kh, qb: (b, qb, kh)),
        out_shape=jax.ShapeDtypeStruct((B, S, N_HEADS * HEAD_DIM), BF16),
        compiler_params=_cparams(("arbitrary", "arbitrary", "arbitrary")),
        name="window_attention",
    )(sink, qkv, qkv, qkv, qkv, qkv, qkv, qkv, qkv, qkv)


def _layer_norm_rows(r, g, b):
    mu = jnp.mean(r, axis=-1, keepdims=True)
    rc = r - mu
    var = jnp.mean(rc * rc, axis=-1, keepdims=True)
    return rc * lax.rsqrt(var + LN_EPS) * g + b


def _route(logits):
    lane = lax.broadcasted_iota(jnp.int32, logits.shape, 1)
    neg = -jnp.inf

    def first_max(vals):
        mx = jnp.max(vals, axis=1, keepdims=True)
        idx = jnp.min(jnp.where(vals == mx, lane, LANES), axis=1, keepdims=True)
        return mx, idx

    gmask = lane < MOE_GROUPS
    gl = jnp.where(gmask, logits, neg)
    gmax, gidx = first_max(gl)
    gsum = jnp.sum(jnp.where(gmask, jnp.exp(gl - gmax), 0.0), axis=1, keepdims=True)
    g_w = 1.0 / gsum
    lo = MOE_GROUPS + gidx * MOE_EXPERTS_PER_GROUP
    el = jnp.where((lane >= lo) & (lane < lo + MOE_EXPERTS_PER_GROUP), logits, neg)
    v1, i1 = first_max(el)
    v2, i2 = first_max(jnp.where(lane == i1, neg, el))
    e2 = jnp.exp(v2 - v1)
    w1 = g_w / (1.0 + e2)
    w2 = g_w * e2 / (1.0 + e2)
    ids = jnp.where(lane == 0, i1 - MOE_GROUPS, jnp.where(lane == 1, i2 - MOE_GROUPS, 0))
    wts = jnp.where(lane == 0, w1, jnp.where(lane == 1, w2, 0.0))
    return ids, wts


MIX_SUB_TILES = 1


def _mix_out_body(*refs, alpha, gla, n_ctx_tiles):
    wr_ref, br_ref, xo_ref, h_ref, id_ref, wt_ref = refs[-6:]
    refs = refs[:-6]
    tm = xo_ref.shape[1]
    sub = tm // MIX_SUB_TILES
    wpt = h_ref.shape[1] // tm
    for t in range(MIX_SUB_TILES):
        rs = slice(t * sub, (t + 1) * sub)
        if gla:
            of_ref, ob_ref, r_ref, ng_ref, x_ref, mod_ref, w_ref, lng_ref, lnb_ref = refs
            x = x_ref[0, rs]
            o = of_ref[0, rs].astype(F32) + ob_ref[0, rs].astype(F32)
            dv = ng_ref.shape[1]
            parts = []
            for hd in range(GLA_HEADS):
                oh = o[:, hd * dv:(hd + 1) * dv]
                ms = jnp.mean(oh * oh, axis=-1, keepdims=True)
                parts.append(oh * lax.rsqrt(ms + RMS_EPS) * ng_ref[...])
            a = (jnp.concatenate(parts, axis=1) * _silu(r_ref[0, rs].astype(F32))).astype(BF16)
        else:
            a_ref, c_ref, x_ref, mod_ref, w_ref, lng_ref, lnb_ref = refs
            a = a_ref[0, rs]
            x = jnp.where(pl.program_id(1) < n_ctx_tiles, c_ref[0, rs], x_ref[0, rs])
        m = mod_ref[0, 0]
        y = jnp.dot(a, w_ref[...], preferred_element_type=F32)
        xn = _layer_norm_rows(alpha * x + m[2:3] * y, lng_ref[...], lnb_ref[...])
        xo_ref[0, rs] = xn
        h = xn * (1.0 + m[4:5]) + m[3:4]
        _store_token_tiles(h_ref.at[0, pl.ds(t * sub * wpt, sub * wpt)], _pack_bf16_pairs(h))
        ids, wts = _route(jnp.dot(h.astype(BF16), wr_ref[...], preferred_element_type=F32) + br_ref[...])
        id_ref[0, rs] = ids
        wt_ref[0, rs] = wts


def _router_params(wg, bg, we, be):
    D = wg.shape[0]
    pad = LANES - MOE_GROUPS - N_EXPERTS
    wr = jnp.concatenate([wg, we, jnp.zeros((D, pad), F32)], axis=1).astype(BF16)
    br = jnp.concatenate([bg, be, jnp.zeros((pad,), F32)]).reshape(1, LANES)
    return wr, br


def _mix_out_tail(B, R, D, tm):
    tok = pl.BlockSpec((1, tm, D), lambda b, j: (b, j, 0))
    wpt = D // 2 // LANES
    packed = pl.BlockSpec((1, tm * wpt, LANES), lambda b, j: (b, j, 0))
    lane_tile = pl.BlockSpec((1, tm, LANES), lambda b, j: (b, j, 0))
    in_specs = [pl.BlockSpec((D, LANES), lambda b, j: (0, 0)), pl.BlockSpec((1, LANES), lambda b, j: (0, 0))]
    out_specs = [tok, packed, lane_tile, lane_tile]
    out_shape = [jax.ShapeDtypeStruct((B, R, D), F32), jax.ShapeDtypeStruct((B, R * wpt, LANES), jnp.uint32),
                 jax.ShapeDtypeStruct((B, R, LANES), jnp.int32), jax.ShapeDtypeStruct((B, R, LANES), F32)]
    return in_specs, out_specs, out_shape


def _attn_out(attn, ctx, x, modt, w_bf16, ln_g, ln_b, router, n_ctx_tiles, alpha):
    B, S, D = attn.shape
    tm = TOKEN_TILE
    seg = lambda j: jnp.where(j >= n_ctx_tiles, 1, 0)
    tok = pl.BlockSpec((1, tm, D), lambda b, j: (b, j, 0))
    vec = pl.BlockSpec((1, D), lambda b, j: (0, 0))
    r_in, out_specs, out_shape = _mix_out_tail(B, S, D, tm)
    return pl.pallas_call(
        functools.partial(_mix_out_body, alpha=alpha, gla=False, n_ctx_tiles=n_ctx_tiles),
        grid=(B, S // tm),
        in_specs=[tok] + _stream_specs(tm, D, n_ctx_tiles) + [
                  pl.BlockSpec((1, 1, 6, D), lambda b, j: (b, seg(j), 0, 0)),
                  pl.BlockSpec((D, D), lambda b, j: (0, 0)), vec, vec] + r_in,
        out_specs=out_specs,
        out_shape=out_shape,
        compiler_params=_cparams(("arbitrary", "arbitrary")),
        name="attn_out_ln",
    )(attn, ctx, x, modt, w_bf16, ln_g.reshape(1, D), ln_b.reshape(1, D), *router)


def _gla_out(o2, proj, norm_g, xs, modt, w_bf16, ln_g, ln_b, router, n_ctx_tiles, T, alpha):
    B, S, D = xs.shape
    tm = TOKEN_TILE
    dv = D // GLA_HEADS
    r_col = proj.shape[2] // D - 1
    off = n_ctx_tiles
    vec = pl.BlockSpec((1, D), lambda b, j: (0, 0))
    r_in, out_specs, out_shape = _mix_out_tail(B, T, D, tm)
    return pl.pallas_call(
        functools.partial(_mix_out_body, alpha=alpha, gla=True, n_ctx_tiles=n_ctx_tiles),
        grid=(B, T // tm),
        in_specs=[
            pl.BlockSpec((1, tm, D), lambda b, j: (b, j + off, 0)),
            pl.BlockSpec((1, tm, D), lambda b, j: (b, j + off, 0)),
            pl.BlockSpec((1, tm, D), lambda b, j: (b, j + off, r_col)),
            pl.BlockSpec((1, dv), lambda b, j: (0, 0)),
            pl.BlockSpec((1, tm, D), lambda b, j: (b, j + off, 0)),
            pl.BlockSpec((1, 1, 6, D), lambda b, j: (b, 1, 0, 0)),
            pl.BlockSpec((D, D), lambda b, j: (0, 0)), vec, vec,
        ] + r_in,
        out_specs=out_specs,
        out_shape=out_shape,
        compiler_params=_cparams(("arbitrary", "arbitrary")),
        name="gla_out_ln",
    )(o2[0], o2[1], proj, norm_g.reshape(1, dv), xs, modt, w_bf16, ln_g.reshape(1, D), ln_b.reshape(1, D),
      *router)


def _moe_body(te_ref, nv_ref, nu_ref, tok_ref, tokn_ref, dst_ref, rw_ref, h_hbm, wg_ref, wu_ref, wd_ref,
              y_hbm, xbuf, ybuf, wgb, wub, wdb, gsem, ssem):
    i = pl.program_id(0)
    n_used = nu_ref[0]

    unroll = 8
    tm = rw_ref.shape[0]
    wpt = xbuf.shape[1] // tm

    def token_rows(ref, first_word_row):
        return ref.at[pl.ds(pl.multiple_of(first_word_row, SUBLANES), wpt), :]

    def gather(idx_ref, n_rows, s):
        def issue(r8, c):
            for u in range(unroll):
                r = r8 * unroll + u
                pltpu.make_async_copy(token_rows(h_hbm, idx_ref[0, 0, r]),
                                      token_rows(xbuf.at[s], r * wpt), gsem.at[s]).start()
            return c
        lax.fori_loop(0, n_rows // unroll, issue, 0)

    def gather_wait(n_rows, s):
        n = pl.multiple_of(n_rows * wpt, SUBLANES)
        pltpu.make_async_copy(h_hbm.at[pl.ds(0, n), :], xbuf.at[s, pl.ds(0, n), :], gsem.at[s]).wait()

    def scatter(n_rows, s):
        def issue(r8, c):
            for u in range(unroll):
                r = r8 * unroll + u
                pltpu.make_async_copy(token_rows(ybuf.at[s], r * wpt),
                                      token_rows(y_hbm, dst_ref[0, 0, r]), ssem.at[s]).start()
            return c
        lax.fori_loop(0, n_rows // unroll, issue, 0)

    def scatter_wait(n_rows, s):
        n = pl.multiple_of(n_rows * wpt, SUBLANES)
        pltpu.make_async_copy(ybuf.at[s, pl.ds(0, n), :], y_hbm.at[pl.ds(0, n), :], ssem.at[s]).wait()

    @pl.when(i == 0)
    def _():
        xbuf[...] = jnp.zeros(xbuf.shape, xbuf.dtype)
        spare = y_hbm.shape[0] - 2 * tm * wpt
        for s in range(2):
            cp = pltpu.make_async_copy(xbuf.at[s], y_hbm.at[pl.ds(spare + s * tm * wpt, tm * wpt), :],
                                       ssem.at[s])
            cp.start()
            cp.wait()
        gather(tok_ref, nv_ref[0], 0)

    def step(slot):
        @pl.when(i + 1 < n_used)
        def _():
            gather(tokn_ref, nv_ref[i + 1], 1 - slot)

        @pl.when(i < n_used)
        def _():
            compute(slot)

    def compute(slot):
        nv = nv_ref[i]
        gather_wait(nv, slot)

        @pl.when((i == 0) | (te_ref[i] != te_ref[jnp.maximum(i - 1, 0)]))
        def _():
            wgb[...] = wg_ref[0, 0].astype(BF16)
            wub[...] = wu_ref[0, 0].astype(BF16)
            wdb[...] = wd_ref[0, 0].astype(BF16)

        @pl.when(i >= 2)
        def _():
            scatter_wait(nv_ref[jnp.maximum(i - 2, 0)], slot)

        x = jnp.concatenate(_unpack_bf16_pairs(_load_token_tiles(xbuf.at[slot], tm)), axis=1).astype(BF16)
        g = jnp.dot(x, wgb[...], preferred_element_type=F32)
        u = jnp.dot(x, wub[...], preferred_element_type=F32)
        hid = (_silu(g) * u).astype(BF16)
        y = jnp.dot(hid, wdb[...], preferred_element_type=F32)
        _store_token_tiles(ybuf.at[slot], _pack_bf16_pairs(y * rw_ref[:, 0:1]))
        scatter(nv, slot)

        @pl.when(i == n_used - 1)
        def _():
            @pl.when(i >= 1)
            def _():
                scatter_wait(nv_ref[jnp.maximum(i - 1, 0)], 1 - slot)
            scatter_wait(nv, slot)

    for slot in range(2):
        pl.when(i % 2 == slot)(functools.partial(step, slot))


def _moe_experts(h, tile_expert, tile_valid, n_used, row_tok, row_dst, row_w, w_gate, w_up, w_down, layer):
    _, E, D, Hd = w_gate.shape
    wpt = D // 2 // LANES
    assert wpt % SUBLANES == 0
    N = h.shape[0] // wpt
    tm = MOE_TILE
    n_tiles = tile_expert.shape[0]
    idx_blk = lambda f: pl.BlockSpec((1, 1, tm), f, memory_space=pltpu.SMEM)
    grid_spec = pltpu.PrefetchScalarGridSpec(
        num_scalar_prefetch=3,
        grid=(n_tiles,),
        in_specs=[
            idx_blk(lambda i, te, nv, nu: (i, 0, 0)),
            idx_blk(lambda i, te, nv, nu: (jnp.minimum(i + 1, n_tiles - 1), 0, 0)),
            idx_blk(lambda i, te, nv, nu: (i, 0, 0)),
            pl.BlockSpec((tm, LANES), lambda i, te, nv, nu: (i, 0)),
            pl.BlockSpec(memory_space=pl.ANY),
            pl.BlockSpec((1, 1, D, Hd), lambda i, te, nv, nu: (layer, te[i], 0, 0)),
            pl.BlockSpec((1, 1, D, Hd), lambda i, te, nv, nu: (layer, te[i], 0, 0)),
            pl.BlockSpec((1, 1, Hd, D), lambda i, te, nv, nu: (layer, te[i], 0, 0)),
        ],
        out_specs=pl.BlockSpec(memory_space=pl.ANY),
        scratch_shapes=[
            pltpu.VMEM((2, tm * wpt, LANES), jnp.uint32), pltpu.VMEM((2, tm * wpt, LANES), jnp.uint32),
            pltpu.VMEM((D, Hd), BF16), pltpu.VMEM((D, Hd), BF16), pltpu.VMEM((Hd, D), BF16),
            pltpu.SemaphoreType.DMA((2,)), pltpu.SemaphoreType.DMA((2,)),
        ],
    )
    return pl.pallas_call(
        _moe_body,
        grid_spec=grid_spec,
        out_shape=jax.ShapeDtypeStruct(((2 * N + 2 * tm) * wpt, LANES), jnp.uint32),
        compiler_params=_cparams(("arbitrary",)),
        name="moe_experts",
    )(tile_expert, tile_valid, n_used, row_tok, row_tok, row_dst, row_w, h, w_gate, w_up, w_down)


def _moe_plan(ids, wts, N, wpt):
    tm = MOE_TILE
    E = N_EXPERTS
    A = 2 * N
    n_tiles = A // tm + E
    e_flat = ids.reshape(A)
    counts = jnp.sum((e_flat[:, None] == jnp.arange(E, dtype=jnp.int32)[None, :]).astype(jnp.int32), axis=0)
    fill = (-counts) % tm
    fj = jnp.arange(tm, dtype=jnp.int32)[None, :]
    fe = jnp.arange(E, dtype=jnp.int32)[:, None]
    fkey = jnp.where(fj < fill[:, None], fe, E).reshape(E * tm)
    n_fill = E * tm
    shift = 20
    assert A + n_fill < (1 << shift) and (E + 1) << shift < (1 << 31)
    idx = jnp.arange(A + n_fill, dtype=jnp.int32)
    keys = (jnp.concatenate([e_flat, fkey]) << shift) | idx
    wgt = jnp.concatenate([wts.reshape(A), jnp.zeros((n_fill,), F32)])
    key_s, w_s = lax.sort((keys, wgt), num_keys=1)
    e_s = (key_s >> shift).reshape(n_tiles, tm)
    a_s = (key_s & ((1 << shift) - 1)).reshape(n_tiles, tm)
    real = a_s < A
    n_used = jnp.sum(counts + fill) // tm
    t_idx = jnp.arange(n_tiles, dtype=jnp.int32)
    te = jnp.minimum(e_s[:, 0], E - 1)
    last_e = jnp.max(jnp.where(t_idx < n_used, te, 0))
    te = jnp.where(t_idx < n_used, te, last_e).astype(jnp.int32)
    tile_valid = jnp.sum(real.astype(jnp.int32), axis=1)
    tile_rows = jnp.where(t_idx < n_used, (tile_valid + 7) // 8 * 8, 0).astype(jnp.int32)
    spare = A + (t_idx[:, None] % 2) * tm + fj
    row_tok = (jnp.where(real, a_s >> 1, 0) * wpt).astype(jnp.int32)
    row_dst = (jnp.where(real, (a_s & 1) * N + (a_s >> 1), spare) * wpt).astype(jnp.int32)
    row_w = jnp.broadcast_to(w_s[:, None], (n_tiles * tm, LANES))
    return (te, tile_rows, n_used.reshape(1).astype(jnp.int32),
            row_tok.reshape(n_tiles, 1, tm), row_dst.reshape(n_tiles, 1, tm), row_w)


def _moe_combine_ln(y0_ref, y1_ref, x_ref, mod_ref, lng_ref, lnb_ref, alpha):
    m = mod_ref[0, 0]
    rows = x_ref.shape[1]
    lo0, hi0 = _unpack_bf16_pairs(_load_token_tiles(y0_ref, rows))
    lo1, hi1 = _unpack_bf16_pairs(_load_token_tiles(y1_ref, rows))
    f = jnp.concatenate([lo0 + lo1, hi0 + hi1], axis=1)
    return _layer_norm_rows(alpha * x_ref[0] + m[5:6] * f, lng_ref[...], lnb_ref[...])


def _moe_out_body(y0_ref, y1_ref, x_ref, mod_ref, lng_ref, lnb_ref, o_ref, *, alpha):
    o_ref[0] = _moe_combine_ln(y0_ref, y1_ref, x_ref, mod_ref, lng_ref, lnb_ref, alpha)


def _moe_out(y, xs, modt, ln_g, ln_b, n_ctx_tiles, alpha):
    B, R, D = xs.shape
    tm = TOKEN_TILE
    rt = R // tm
    wpt = D // 2 // LANES
    if n_ctx_tiles < 0:
        seg = lambda j: 1
    else:
        seg = lambda j: jnp.where(j >= n_ctx_tiles, 1, 0)
    tok = pl.BlockSpec((1, tm, D), lambda b, j: (b, j, 0))
    vec = pl.BlockSpec((1, D), lambda b, j: (0, 0))
    return pl.pallas_call(
        functools.partial(_moe_out_body, alpha=alpha),
        grid=(B, R // tm),
        in_specs=[pl.BlockSpec((tm * wpt, LANES), lambda b, j: (b * rt + j, 0)),
                  pl.BlockSpec((tm * wpt, LANES), lambda b, j: (B * rt + b * rt + j, 0)),
                  tok, pl.BlockSpec((1, 1, 6, D), lambda b, j: (b, seg(j), 0, 0)), vec, vec],
        out_specs=tok,
        out_shape=jax.ShapeDtypeStruct((B, R, D), F32),
        compiler_params=_cparams(("arbitrary", "arbitrary")),
        name="moe_out_ln",
    )(y, y, xs, modt, ln_g.reshape(1, D), ln_b.reshape(1, D))


def _hier_moe_experts(h, ids, wts, B, R, D, w_gate, w_up, w_down, layer):
    N = B * R
    wpt = D // 2 // LANES
    plan = _moe_plan(ids.reshape(N, LANES)[:, :2], wts.reshape(N, LANES)[:, :2], N, wpt)
    return _moe_experts(h.reshape(N * wpt, LANES), *plan, w_gate, w_up, w_down, layer)


GLA_PROJ_COL_TILE = 1024


def _gla_proj_body(y0_ref, y1_ref, x_ref, mod0_ref, lng_ref, lnb_ref, mod_ref, w_ref, w1_ref, w2_ref, b_ref,
                   xo_ref, o_ref, g_ref, *, alpha, q_cols, q_scale):
    xs = _moe_combine_ln(y0_ref, y1_ref, x_ref, mod0_ref, lng_ref, lnb_ref, alpha)
    xo_ref[0] = xs
    m = mod_ref[0, 0]
    z = (xs * (1.0 + m[1:2]) + m[0:1]).astype(BF16)
    ct = GLA_PROJ_COL_TILE
    for c0 in range(0, o_ref.shape[2], ct):
        acc = jnp.dot(z, w_ref[:, c0:c0 + ct], preferred_element_type=F32)
        if c0 < q_cols:
            acc = acc * q_scale
        o_ref[0, :, c0:c0 + ct] = acc.astype(BF16)
    t = jnp.dot(z, w1_ref[...], preferred_element_type=F32)
    pre = jnp.dot(t.astype(BF16), w2_ref[...], preferred_element_type=F32) + b_ref[...]
    g_ref[0] = (jnp.minimum(pre, 0.0) - jnp.log(1.0 + jnp.exp(-jnp.abs(pre)))) * (1.0 / GLA_TAU)


def _gla_proj(y, xs, modt0, ln_g, ln_b, modt, w_bf16, w1, w2, gb, n_ctx_tiles, alpha, q_cols, q_scale):
    B, S, D = xs.shape
    NO = w_bf16.shape[1]
    rank = w1.shape[2]
    kd = w2.shape[2]
    tm = TOKEN_TILE
    rt = S // tm
    wpt = D // 2 // LANES
    assert q_cols % GLA_PROJ_COL_TILE == 0 and NO % GLA_PROJ_COL_TILE == 0
    w1c = jnp.zeros((D, LANES), F32).at[:, :rank].set(w1[0]).at[:, rank:2 * rank].set(w1[1]).astype(BF16)
    w2c = (jnp.zeros((LANES, 2 * kd), F32).at[:rank, :kd].set(w2[0]).at[rank:2 * rank, kd:].set(w2[1])
           .astype(BF16))
    seg = lambda j: jnp.where(j >= n_ctx_tiles, 1, 0)
    const = lambda shape: pl.BlockSpec(shape, lambda b, j: (0, 0))
    tok = pl.BlockSpec((1, tm, D), lambda b, j: (b, j, 0))
    mod = pl.BlockSpec((1, 1, 6, D), lambda b, j: (b, seg(j), 0, 0))
    return pl.pallas_call(
        functools.partial(_gla_proj_body, alpha=alpha, q_cols=q_cols, q_scale=q_scale),
        grid=(B, S // tm),
        in_specs=[
            pl.BlockSpec((tm * wpt, LANES), lambda b, j: (b * rt + j, 0)),
            pl.BlockSpec((tm * wpt, LANES), lambda b, j: (B * rt + b * rt + j, 0)),
            tok, mod, const((1, D)), const((1, D)), mod,
            pl.BlockSpec((D, NO), lambda b, j: (0, 0), pipeline_mode=pl.Buffered(1)),
            const((D, LANES)), const((LANES, 2 * kd)), const((1, 2 * kd)),
        ],
        out_specs=[tok, pl.BlockSpec((1, tm, NO), lambda b, j: (b, j, 0)),
                   pl.BlockSpec((1, tm, 2 * kd), lambda b, j: (b, j, 0))],
        out_shape=[jax.ShapeDtypeStruct((B, S, D), F32), jax.ShapeDtypeStruct((B, S, NO), BF16),
                   jax.ShapeDtypeStruct((B, S, 2 * kd), F32)],
        compiler_params=_cparams(("arbitrary", "arbitrary")),
        name="gla_proj",
    )(y, y, xs, modt0, ln_g.reshape(1, D), ln_b.reshape(1, D), modt, w_bf16, w1c, w2c, gb.reshape(1, 2 * kd))


GLA_LEVELS = (32, 16, 8, 4, 2, 1)
GLA_BATCH_PER_STEP = 4


def _gla_constants():
    L = GLA_CHUNK
    t = np.arange(L)[:, None]
    u = np.arange(L)[None, :]
    mats, masks = [], []
    for bwd in (False, True):
        blocks = []
        blocks.append((u >= t) if bwd else (u <= t))
        blocks.append((u < t) if bwd else (u > t))
        q_lv, k_lv, m_lv = [], [], []
        for m in GLA_LEVELS:
            base = (t // (2 * m)) * (2 * m)
            ubase = (u // (2 * m)) * (2 * m)
            t_hi = (t % (2 * m)) >= m
            if not bwd:
                r = base + m - 1
                q_lv.append(t_hi & (u > r) & (u <= t))
                k_lv.append(~t_hi & (u > t) & (u <= r))
                m_lv.append(t_hi & ((u % (2 * m)) < m) & (base == ubase))
            else:
                r = base + m
                q_lv.append(~t_hi & (u >= t) & (u < r))
                k_lv.append(t_hi & (u >= r) & (u < t))
                m_lv.append(~t_hi & ((u % (2 * m)) >= m) & (base == ubase))
        blocks += [ql | kl for ql, kl in zip(q_lv, k_lv)]
        blocks.append(np.ones((8, L), bool))
        a = np.concatenate(blocks, axis=0).astype(np.float32)
        mats.append(np.concatenate([a, a, a], axis=1))
        m_lv.append(t == u)
        masks.append(np.concatenate(m_lv, axis=0).astype(np.float32))
    return np.stack(mats), np.stack(masks)


LOG2_E = 1.4426950408889634


def _gla_chunk_step(q, k, v, g, a_ref, mk_ref, st_ref, hd):
    L = GLA_CHUNK
    nlv = len(GLA_LEVELS)
    g = g * LOG2_E
    g_hi = g.astype(BF16)
    r1 = g - g_hi.astype(F32)
    g_mid = r1.astype(BF16)
    g_lo = (r1 - g_mid.astype(F32)).astype(BF16)
    gs = jnp.concatenate([g_hi, g_mid, g_lo], axis=0)
    ex = jnp.exp2(jnp.dot(a_ref[...], gs, preferred_element_type=F32))
    qf = q.astype(F32)
    kf = k.astype(F32)
    q_inter = (qf * ex[0:L]).astype(BF16)
    k_state = (kf * ex[L:2 * L]).astype(BF16)
    dec = ex[(2 + nlv) * L:(2 + nlv) * L + 1]
    nt = (((1,), (1,)), ((), ()))
    att = lax.dot_general(q, k, nt, preferred_element_type=F32) * mk_ref[nlv * L:(nlv + 1) * L]
    for lv in range(nlv):
        f = ex[(2 + lv) * L:(3 + lv) * L]
        ql = (qf * f).astype(BF16)
        kl = (kf * f).astype(BF16)
        att = att + lax.dot_general(ql, kl, nt, preferred_element_type=F32) * mk_ref[lv * L:(lv + 1) * L]
    st = st_ref[hd]
    o = jnp.dot(att.astype(BF16), v, preferred_element_type=F32)
    o = o + lax.dot_general(q_inter, st.astype(BF16), nt, preferred_element_type=F32)
    upd = lax.dot_general(v, k_state, (((0,), (0,)), ((), ())), preferred_element_type=F32)
    st_ref[hd] = st * dec + upd
    return o


def _gla_scan_body(qf_ref, kf_ref, vf_ref, gf_ref, qb_ref, kb_ref, vb_ref, gb_ref,
                   af_ref, ab_ref, mf_ref, mb_ref, of_ref, ob_ref, stf_ref, stb_ref):
    @pl.when(pl.program_id(1) == 0)
    def _():
        stf_ref[...] = jnp.zeros(stf_ref.shape, stf_ref.dtype)
        stb_ref[...] = jnp.zeros(stb_ref.shape, stb_ref.dtype)

    _, dv, dk = stf_ref.shape
    for bi in range(GLA_BATCH_PER_STEP):
        for hd in range(GLA_HEADS):
            ks = slice(hd * dk, (hd + 1) * dk)
            vs = slice(hd * dv, (hd + 1) * dv)
            st = bi * GLA_HEADS + hd
            o = _gla_chunk_step(qf_ref[bi, :, ks], kf_ref[bi, :, ks], vf_ref[bi, :, vs], gf_ref[bi, :, ks],
                                af_ref, mf_ref, stf_ref, st)
            of_ref[bi, :, vs] = o.astype(of_ref.dtype)
            o = _gla_chunk_step(qb_ref[bi, :, ks], kb_ref[bi, :, ks], vb_ref[bi, :, vs], gb_ref[bi, :, ks],
                                ab_ref, mb_ref, stb_ref, st)
            ob_ref[bi, :, vs] = o.astype(ob_ref.dtype)


def _gla_scan(proj, lg, C, T):
    B, S, NO = proj.shape
    kd = lg.shape[2] // 2
    D = NO - 2 * kd
    D = D // 2
    dk = kd // GLA_HEADS
    dv = D // GLA_HEADS
    L = GLA_CHUNK
    n_ctx = C // L
    n_all = S // L
    amat, masks = _gla_constants()
    amat = jnp.asarray(amat, BF16)
    masks = jnp.asarray(masks, F32)

    def fwd(c):
        return c

    def bwd(c):
        return jnp.where(c < n_ctx, n_ctx - 1 - c, n_ctx + n_all - 1 - c)

    assert 2 * kd == D

    nb = GLA_BATCH_PER_STEP
    assert B % nb == 0

    def specs(chunk, d):
        return [
            pl.BlockSpec((nb, L, kd), lambda b, c: (b, chunk(c), 0)),
            pl.BlockSpec((nb, L, kd), lambda b, c: (b, chunk(c), 1)),
            pl.BlockSpec((nb, L, D), lambda b, c: (b, chunk(c), 1)),
            pl.BlockSpec((nb, L, kd), lambda b, c: (b, chunk(c), d)),
        ]

    const = lambda arr: pl.BlockSpec(arr.shape[1:], lambda b, c: (0, 0))
    out = jax.ShapeDtypeStruct((B, S, D), BF16)
    state = pltpu.VMEM((nb * GLA_HEADS, dv, dk), F32)
    return pl.pallas_call(
        _gla_scan_body,
        grid=(B // nb, n_all),
        in_specs=specs(fwd, 0) + specs(bwd, 1) + [const(amat), const(amat), const(masks), const(masks)],
        out_specs=[pl.BlockSpec((nb, L, D), lambda b, c: (b, fwd(c), 0)),
                   pl.BlockSpec((nb, L, D), lambda b, c: (b, bwd(c), 0))],
        out_shape=[out, out],
        scratch_shapes=[state, state],
        compiler_params=_cparams(("arbitrary", "arbitrary")),
        name="gla_scan",
    )(proj, proj, proj, lg, proj, proj, proj, lg, amat[0], amat[1], masks[0], masks[1])


def kernel(x, c, ctx, c_ctx, mod_w, mod_b, ln_g, ln_b, attn_w_qkv, attn_w_o, attn_sink,
           gla_w_in, gla_gate_w1, gla_gate_w2, gla_gate_b, gla_norm_g, gla_w_o,
           moe_group_w, moe_group_b, moe_router_w, moe_router_b, moe_w_gate, moe_w_up, moe_w_down):
    B, T, D = x.shape
    C = ctx.shape[1]
    depth = mod_w.shape[0]
    assert depth == 2 and D == N_HEADS * HEAD_DIM
    assert C % TOKEN_TILE == 0 and T % TOKEN_TILE == 0 and T % GRID_W == 0
    alpha = (2 * depth) ** 0.25
    n_ctx_tiles = C // TOKEN_TILE

    m_rows = -(-(B + 1) // 8) * 8
    cond = jnp.zeros((m_rows, D), F32).at[:B].set(c).at[B].set(c_ctx)
    mods = _adaln_mods(cond, mod_w, mod_b)

    def mod_table(l):
        lat = mods[l, :B].reshape(B, 1, 6, D)
        cm = jnp.broadcast_to(mods[l, B].reshape(1, 1, 6, D), (B, 1, 6, D))
        return jnp.concatenate([cm, lat], axis=1)

    modt = mod_table(0)
    cos, sin = _rope_tables(C, T)
    qkv = _qkv_proj(ctx, x, modt, attn_w_qkv[0].astype(BF16), cos, sin, n_ctx_tiles)
    attn = _attention(qkv, attn_sink[0], C, T)
    router = _router_params(moe_group_w[0], moe_group_b[0], moe_router_w[0], moe_router_b[0])
    xs, h, ids, wts = _attn_out(attn, ctx, x, modt, attn_w_o[0].astype(BF16), ln_g[0, 0], ln_b[0, 0], router,
                                n_ctx_tiles, alpha)
    y = _hier_moe_experts(h, ids, wts, B, C + T, D, moe_w_gate, moe_w_up, moe_w_down, 0)

    modt1 = mod_table(1)
    kd = gla_gate_w2.shape[3]
    dk = kd // GLA_HEADS
    xs, proj, lg = _gla_proj(y, xs, modt, ln_g[0, 1], ln_b[0, 1], modt1, gla_w_in[0].astype(BF16),
                             gla_gate_w1[0], gla_gate_w2[0], gla_gate_b[0], n_ctx_tiles, alpha, kd, dk ** -0.5)
    o2 = _gla_scan(proj, lg, C, T)
    router = _router_params(moe_group_w[1], moe_group_b[1], moe_router_w[1], moe_router_b[1])
    xl, h, ids, wts = _gla_out(o2, proj, gla_norm_g[0], xs, modt1, gla_w_o[0].astype(BF16), ln_g[1, 0],
                               ln_b[1, 0], router, n_ctx_tiles, T, alpha)
    y = _hier_moe_experts(h, ids, wts, B, T, D, moe_w_gate, moe_w_up, moe_w_down, 1)
    return _moe_out(y, xl, modt1, ln_g[1, 1], ln_b[1, 1], -1, alpha)
```
